```python
import math
import jax
import jax.numpy as jnp
from jax import lax
import numpy as np

D_MODEL = 1024
BATCH = 16
SEQ = 2048
DEPTH = 1
DEC_BATCH = 8
DEC_SEQ = 2048
PAST_LEN = 128

N_META = 16
EPS = 1e-6
A_HEADS = 8
A_QK_DIM = 64
A_V_DIM = 2 * A_QK_DIM
A_WIDTH = A_HEADS * A_V_DIM
ROT_DIM = A_QK_DIM // 4
ROPE_THETA = 500000.0
Q_BLOCK = 128
M_HEADS = 4
M_QK_DIM = 128
M_V_DIM = 256
M_WIDTH = M_HEADS * M_V_DIM
CHUNK = 128
N_EXPERTS = 32
TOP_K = 4
D_FF = 1024
SWIGLU_LIMIT = 7.0
SWIGLU_ALPHA = 1.702
MOE_BLOCK = 128
SPLIT_SIZES = (A_HEADS * 2 * A_QK_DIM, A_HEADS * 2 * A_QK_DIM, A_WIDTH, M_HEADS * M_QK_DIM, M_HEADS * M_QK_DIM, M_WIDTH, M_WIDTH, 4 * M_HEADS, D_MODEL, D_MODEL)
N_IN = sum(SPLIT_SIZES)

kernel_name = "hybrid_diffattn_mlstm_moe_encoder"


def rmsnorm(x, w):
    x32 = x.astype(jnp.float32)
    y = x32 * lax.rsqrt(jnp.mean(x32 * x32, axis=-1, keepdims=True) + EPS)
    return (y * w.astype(jnp.float32)).astype(x.dtype)


def rope_partial(t, cos, sin):
    half = ROT_DIM // 2
    tr = t[..., :ROT_DIM].astype(jnp.float32)
    x1, x2 = tr[..., :half], tr[..., half:]
    c = cos[None, :, None, :]
    s = sin[None, :, None, :]
    rot = jnp.concatenate([x1 * c - x2 * s, x2 * c + x1 * s], axis=-1).astype(t.dtype)
    return jnp.concatenate([rot, t[..., ROT_DIM:]], axis=-1)


def diff_attention(q1, q2, k1, k2, v, lam):
    B, L, H, _ = q1.shape
    nqb = -(-L // Q_BLOCK)
    Lp = nqb * Q_BLOCK
    scale = A_QK_DIM ** -0.5

    def blocks(t):
        t = jnp.pad(t, ((0, 0), (0, Lp - L), (0, 0), (0, 0)))
        return jnp.moveaxis(t.reshape(B, nqb, Q_BLOCK, H, t.shape[-1]), 1, 0)

    def one_block(qs):
        qb1, qb2 = qs
        a1 = jax.nn.softmax((jnp.einsum('bqhd,bkhd->bhqk', qb1, k1) * scale).astype(jnp.float32), axis=-1)
        a2 = jax.nn.softmax((jnp.einsum('bqhd,bkhd->bhqk', qb2, k2) * scale).astype(jnp.float32), axis=-1)
        amap = (a1 - lam * a2).astype(v.dtype)
        return jnp.einsum('bhqk,bkhv->bqhv', amap, v)

    o = lax.map(one_block, (blocks(q1), blocks(q2)))
    return jnp.moveaxis(o, 0, 1).reshape(B, Lp, H, A_V_DIM)[:, :L]


def mlstm_chunk(state, inp):
    C, n, m = state
    q, k, v, ig, lf = inp
    T = q.shape[2]
    b = jnp.cumsum(lf, axis=-1)
    dmat = b[..., :, None] - b[..., None, :] + ig[..., None, :]
    causal_in_chunk = jnp.tril(jnp.ones((T, T), dtype=bool))
    dmat = jnp.where(causal_in_chunk, dmat, -jnp.inf)
    inter = b + m[..., None]
    m_t = jnp.maximum(inter, jnp.max(dmat, axis=-1))
    w_inter = jnp.exp(inter - m_t)
    s = jnp.einsum('bhtk,bhsk->bhts', q, k) * jnp.exp(dmat - m_t[..., None])
    num = w_inter[..., None] * jnp.einsum('bhtk,bhkv->bhtv', q, C) + jnp.einsum('bhts,bhsv->bhtv', s, v)
    nq = w_inter * jnp.einsum('bhtk,bhk->bht', q, n) + jnp.sum(s, axis=-1)
    h = num / jnp.maximum(jnp.abs(nq), jnp.exp(-m_t))[..., None]
    b_end = b[..., -1]
    g = b_end[..., None] - b + ig
    m_new = jnp.maximum(b_end + m, jnp.max(g, axis=-1))
    decay = jnp.exp(b_end + m - m_new)
    ws = jnp.exp(g - m_new[..., None])
    C_new = decay[..., None, None] * C + jnp.einsum('bhs,bhsk,bhsv->bhkv', ws, k, v)
    n_new = decay[..., None] * n + jnp.einsum('bhs,bhsk->bhk', ws, k)
    return (C_new, n_new, m_new), h


def mlstm_scan(state, q, k, v, ig, lf):
    B, H, Lr, _ = q.shape
    nc = Lr // CHUNK

    def to_chunks(a):
        return jnp.moveaxis(a.reshape(a.shape[:2] + (nc, CHUNK) + a.shape[3:]), 2, 0)

    state, h = lax.scan(mlstm_chunk, state, (to_chunks(q), to_chunks(k), to_chunks(v), to_chunks(ig), to_chunks(lf)))
    return state, jnp.moveaxis(h, 0, 2).reshape(B, H, Lr, M_V_DIM)


def mlstm_bidirectional(q, k, v, ig_f, lf_f, ig_b, lf_b):
    B, H, L, _ = q.shape
    init = (jnp.zeros((B, H, M_QK_DIM, M_V_DIM), jnp.float32), jnp.zeros((B, H, M_QK_DIM), jnp.float32), jnp.zeros((B, H), jnp.float32))
    meta = lambda a: a[:, :, :N_META]
    real = lambda a: a[:, :, N_META:]
    flip = lambda a: jnp.flip(a, axis=2)
    fwd_in = (q, k, v, ig_f, lf_f)
    bwd_in = (q, k, v, ig_b, lf_b)
    st, h_meta = mlstm_chunk(init, tuple(meta(a) for a in fwd_in))
    _, h_real = mlstm_scan(st, *[real(a) for a in fwd_in])
    h_fwd = jnp.concatenate([h_meta, h_real], axis=2)
    st, hb_real = mlstm_scan(init, *[flip(real(a)) for a in bwd_in])
    _, hb_meta = mlstm_chunk(st, tuple(flip(meta(a)) for a in bwd_in))
    h_bwd = jnp.concatenate([flip(hb_meta), flip(hb_real)], axis=2)
    return h_fwd + h_bwd


def moe_ffn(x, router_w, router_b, w_gate, b_gate, w_up, b_up, w_down, b_down):
    N = x.shape[0]
    logits = (x @ router_w + router_b).astype(jnp.float32)
    top_v, top_i = lax.top_k(logits, TOP_K)
    gates = jax.nn.softmax(top_v, axis=-1)
    flat_e = top_i.reshape(-1)
    flat_t = jnp.repeat(jnp.arange(N, dtype=jnp.int32), TOP_K)
    flat_w = gates.reshape(-1)
    order = jnp.argsort(flat_e)
    se = flat_e[order]
    counts = jnp.zeros((N_EXPERTS,), jnp.int32).at[flat_e].add(1)
    padded = (counts + MOE_BLOCK - 1) // MOE_BLOCK * MOE_BLOCK
    starts = jnp.cumsum(counts) - counts
    pends = jnp.cumsum(padded)
    pstarts = pends - padded
    dest = pstarts[se] + (jnp.arange(N * TOP_K, dtype=jnp.int32) - starts[se])
    n_blocks = (N * TOP_K + N_EXPERTS * (MOE_BLOCK - 1) + MOE_BLOCK - 1) // MOE_BLOCK
    cap = n_blocks * MOE_BLOCK
    slot_t = jnp.zeros((cap,), jnp.int32).at[dest].set(flat_t[order])
    slot_w = jnp.zeros((cap,), jnp.float32).at[dest].set(flat_w[order])
    blk_e = jnp.minimum(jnp.searchsorted(pends, jnp.arange(n_blocks, dtype=jnp.int32) * MOE_BLOCK, side='right'), N_EXPERTS - 1)

    def expert_block(args):
        t, e = args
        xb = x[t]
        gt = jnp.minimum(xb @ w_gate[e] + b_gate[e], SWIGLU_LIMIT)
        up = jnp.clip(xb @ w_up[e] + b_up[e], -SWIGLU_LIMIT, SWIGLU_LIMIT)
        glu = gt * jax.nn.sigmoid(SWIGLU_ALPHA * gt)
        return ((up + 1.0) * glu) @ w_down[e] + b_down[e]

    out = lax.map(expert_block, (slot_t.reshape(n_blocks, MOE_BLOCK), blk_e))
    contrib = out.reshape(cap, x.shape[1]) * slot_w[:, None].astype(x.dtype)
    return jnp.zeros_like(x).at[slot_t].add(contrib)


def hybrid_layer(h, layer_idx, cos, sin, norm1_w, w_in, lambda_q1, lambda_k1, lambda_q2, lambda_k2, attn_subln_w, mlstm_gate_b, mlstm_norm_w, w_br_attn, w_br_mlstm, w_out, norm2_w, router_w, router_b, w_gate, b_gate, w_up, b_up, w_down, b_down):
    B, L, D = h.shape
    xn = rmsnorm(h, norm1_w)
    proj = xn @ w_in
    split_idx = np.cumsum(SPLIT_SIZES)[:-1].tolist()
    a_q, a_k, a_v, m_q, m_k, m_v, m_o, m_g, g_attn, g_mlstm = jnp.split(proj, split_idx, axis=-1)

    lam_init = 0.8 - 0.6 * math.exp(-0.3 * layer_idx)
    q = a_q.reshape(B, L, A_HEADS, 2, A_QK_DIM)
    k = a_k.reshape(B, L, A_HEADS, 2, A_QK_DIM)
    q1, q2 = rope_partial(q[..., 0, :], cos, sin), rope_partial(q[..., 1, :], cos, sin)
    k1, k2 = rope_partial(k[..., 0, :], cos, sin), rope_partial(k[..., 1, :], cos, sin)
    v = a_v.reshape(B, L, A_HEADS, A_V_DIM)
    lam = (jnp.exp(jnp.sum(lambda_q1.astype(jnp.float32) * lambda_k1.astype(jnp.float32)))
           - jnp.exp(jnp.sum(lambda_q2.astype(jnp.float32) * lambda_k2.astype(jnp.float32))) + lam_init)
    o_a = diff_attention(q1, q2, k1, k2, v, lam)
    o_a = rmsnorm(o_a, attn_subln_w) * (1.0 - lam_init)
    p_a = o_a.reshape(B, L, A_WIDTH) @ w_br_attn

    mq = m_q.reshape(B, L, M_HEADS, M_QK_DIM).transpose(0, 2, 1, 3).astype(jnp.float32) * (M_QK_DIM ** -0.5)
    mk = m_k.reshape(B, L, M_HEADS, M_QK_DIM).transpose(0, 2, 1, 3).astype(jnp.float32)
    mv = m_v.reshape(B, L, M_HEADS, M_V_DIM).transpose(0, 2, 1, 3).astype(jnp.float32)
    gts = (m_g.astype(jnp.float32).reshape(B, L, 4, M_HEADS) + mlstm_gate_b.astype(jnp.float32)).transpose(2, 0, 3, 1)
    h_m = mlstm_bidirectional(mq, mk, mv, gts[0], jax.nn.log_sigmoid(gts[1]), gts[2], jax.nn.log_sigmoid(gts[3]))
    h_m = rmsnorm(h_m.transpose(0, 2, 1, 3), mlstm_norm_w.reshape(M_HEADS, M_V_DIM)).astype(m_o.dtype)
    h_m = h_m.reshape(B, L, M_WIDTH) * jax.nn.sigmoid(m_o)
    p_m = h_m @ w_br_mlstm

    merged = jax.nn.sigmoid(g_attn) * p_a + jax.nn.sigmoid(g_mlstm) * p_m
    h = h + merged @ w_out

    xn2 = rmsnorm(h, norm2_w).reshape(B * L, D)
    return h + moe_ffn(xn2, router_w, router_b, w_gate, b_gate, w_up, b_up, w_down, b_down).reshape(B, L, D)


def trunk(x, meta_tokens, norm1_w, w_in, lambda_q1, lambda_k1, lambda_q2, lambda_k2, attn_subln_w, mlstm_gate_b, mlstm_norm_w, w_br_attn, w_br_mlstm, w_out, norm2_w, router_w, router_b, w_gate, b_gate, w_up, b_up, w_down, b_down, final_norm_w):
    B = x.shape[0]
    h = jnp.concatenate([jnp.broadcast_to(meta_tokens[None].astype(x.dtype), (B, N_META, x.shape[2])), x], axis=1)
    L = h.shape[1]
    pos = jnp.arange(L, dtype=jnp.float32)
    inv_freq = ROPE_THETA ** (-jnp.arange(0, ROT_DIM, 2, dtype=jnp.float32) / ROT_DIM)
    ang = pos[:, None] * inv_freq[None, :]
    cos, sin = jnp.cos(ang), jnp.sin(ang)
    for l in range(DEPTH):
        h = hybrid_layer(h, l, cos, sin, norm1_w[l], w_in[l], lambda_q1[l], lambda_k1[l], lambda_q2[l], lambda_k2[l],
                         attn_subln_w[l], mlstm_gate_b[l], mlstm_norm_w[l], w_br_attn[l], w_br_mlstm[l], w_out[l],
                         norm2_w[l], router_w[l], router_b[l], w_gate[l], b_gate[l], w_up[l], b_up[l], w_down[l], b_down[l])
    return rmsnorm(h, final_norm_w)[:, N_META:]


def setup_inputs(seed: int = 0) -> dict:
    key = jax.random.key(seed)
    ks = jax.random.split(key, 26)
    nrm = lambda k, shape, scale: jax.random.normal(k, shape, jnp.float32) * scale
    f_bias = jnp.linspace(3.0, 6.0, M_HEADS)
    zero_b = jnp.zeros((M_HEADS,), jnp.float32)
    gate_base = jnp.stack([zero_b, f_bias, zero_b, f_bias])
    return {
        'x_prompt': nrm(ks[0], (BATCH, SEQ, D_MODEL), 1.0),
        'x_sample': nrm(ks[1], (DEC_BATCH, DEC_SEQ, D_MODEL), 1.0),
        'meta_tokens': nrm(ks[2], (N_META, D_MODEL), 1.0),
        'norm1_w': 1.0 + nrm(ks[3], (DEPTH, D_MODEL), 0.02),
        'w_in': nrm(ks[4], (DEPTH, D_MODEL, N_IN), D_MODEL ** -0.5),
        'lambda_q1': nrm(ks[5], (DEPTH, A_QK_DIM), 0.1),
        'lambda_k1': nrm(ks[6], (DEPTH, A_QK_DIM), 0.1),
        'lambda_q2': nrm(ks[7], (DEPTH, A_QK_DIM), 0.1),
        'lambda_k2': nrm(ks[8], (DEPTH, A_QK_DIM), 0.1),
        'attn_subln_w': 1.0 + nrm(ks[9], (DEPTH, A_V_DIM), 0.02),
        'mlstm_gate_b': gate_base[None] + nrm(ks[10], (DEPTH, 4, M_HEADS), 0.1),
        'mlstm_norm_w': 1.0 + nrm(ks[11], (DEPTH, M_WIDTH), 0.02),
        'w_br_attn': nrm(ks[12], (DEPTH, A_WIDTH, D_MODEL), A_WIDTH ** -0.5),
        'w_br_mlstm': nrm(ks[13], (DEPTH, M_WIDTH, D_MODEL), M_WIDTH ** -0.5),
        'w_out': nrm(ks[14], (DEPTH, D_MODEL, D_MODEL), D_MODEL ** -0.5),
        'norm2_w': 1.0 + nrm(ks[15], (DEPTH, D_MODEL), 0.02),
        'router_w': nrm(ks[16], (DEPTH, D_MODEL, N_EXPERTS), D_MODEL ** -0.5),
        'router_b': nrm(ks[17], (DEPTH, N_EXPERTS), 0.01),
        'w_gate': nrm(ks[18], (DEPTH, N_EXPERTS, D_MODEL, D_FF), D_MODEL ** -0.5),
        'b_gate': nrm(ks[19], (DEPTH, N_EXPERTS, D_FF), 0.01),
        'w_up': nrm(ks[20], (DEPTH, N_EXPERTS, D_MODEL, D_FF), D_MODEL ** -0.5),
        'b_up': nrm(ks[21], (DEPTH, N_EXPERTS, D_FF), 0.01),
        'w_down': nrm(ks[22], (DEPTH, N_EXPERTS, D_FF, D_MODEL), D_FF ** -0.5),
        'b_down': nrm(ks[23], (DEPTH, N_EXPERTS, D_MODEL), 0.01),
        'final_norm_w': 1.0 + nrm(ks[24], (D_MODEL,), 0.02),
    }


def reference(x_prompt, x_sample, meta_tokens, norm1_w, w_in, lambda_q1, lambda_k1, lambda_q2, lambda_k2, attn_subln_w, mlstm_gate_b, mlstm_norm_w, w_br_attn, w_br_mlstm, w_out, norm2_w, router_w, router_b, w_gate, b_gate, w_up, b_up, w_down, b_down, final_norm_w):
    y_prompt = trunk(x_prompt, meta_tokens, norm1_w, w_in, lambda_q1, lambda_k1, lambda_q2, lambda_k2, attn_subln_w,
                     mlstm_gate_b, mlstm_norm_w, w_br_attn, w_br_mlstm, w_out, norm2_w, router_w, router_b,
                     w_gate, b_gate, w_up, b_up, w_down, b_down, final_norm_w)
    y_sample = trunk(x_sample, meta_tokens, norm1_w, w_in, lambda_q1, lambda_k1, lambda_q2, lambda_k2, attn_subln_w,
                     mlstm_gate_b, mlstm_norm_w, w_br_attn, w_br_mlstm, w_out, norm2_w, router_w, router_b,
                     w_gate, b_gate, w_up, b_up, w_down, b_down, final_norm_w)
    return (y_prompt, y_sample)
```

```python
import functools
import math

import jax
import jax.numpy as jnp
import numpy as np
from jax import lax
from jax.experimental import pallas as pl
from jax.experimental.pallas import tpu as pltpu

F32 = jnp.float32
BF16 = jnp.bfloat16

D_MODEL = 1024
N_META = 16
EPS = 1e-6
A_HEADS = 8
A_QK_DIM = 64
A_V_DIM = 128
ROT_DIM = 16
ROPE_THETA = 500000.0
M_HEADS = 4
M_QK_DIM = 128
M_V_DIM = 256
CHUNK = 128
N_EXPERTS = 32
TOP_K = 4
D_FF = 1024
SWIGLU_LIMIT = 7.0
SWIGLU_ALPHA = 1.702
LAM_INIT = 0.8 - 0.6 * math.exp(-0.3 * 0)

LANES = 128
NEG_BIG = -1e30
MOE_ROWS = 256
COL_AQ, COL_AK, COL_AV = 0, 8, 16
COL_MQ, COL_MK = 24, 28
COL_MV, COL_MO = 16, 20
COL_GA, COL_GM = 6, 7
N_MAIN = 8192

VMEM_LIMIT = 56 * 1024 * 1024


def _pick(n, prefs):
    for p in prefs:
        if n % p == 0:
            return p
    raise ValueError(f"no tile for {n}")


def _nt(a, b):
    return lax.dot_general(a, b, (((1,), (1,)), ((), ())), preferred_element_type=F32)


def _dot(a, b):
    return jnp.dot(a, b, preferred_element_type=F32)


def _inproj_kernel(x_ref, n1_ref, w_ref, wg_ref, o_ref, g_ref, xn_ref):
    @pl.when(pl.program_id(1) == 0)
    def _():
        x = x_ref[...]
        ms = jnp.mean(x * x, axis=-1, keepdims=True)
        xn = (x * lax.rsqrt(ms + EPS) * n1_ref[...]).astype(BF16)
        xn_ref[...] = xn
        g_ref[...] = _dot(xn, wg_ref[...])[:, :4 * M_HEADS]

    o_ref[...] = _dot(xn_ref[...], w_ref[...]).astype(BF16)


def _inproj(h0, norm1_w, w_main, w_gates):
    n = h0.shape[0]
    tm = _pick(n, (1152, 768, 512, 384, 256, 128))
    tn = 2048
    return pl.pallas_call(
        _inproj_kernel,
        grid=(n // tm, N_MAIN // tn),
        in_specs=[
            pl.BlockSpec((tm, D_MODEL), lambda i, j: (i, 0)),
            pl.BlockSpec((1, D_MODEL), lambda i, j: (0, 0)),
            pl.BlockSpec((D_MODEL, tn), lambda i, j: (0, j)),
            pl.BlockSpec((D_MODEL, LANES), lambda i, j: (0, 0)),
        ],
        out_specs=[
            pl.BlockSpec((tm, tn), lambda i, j: (i, j)),
            pl.BlockSpec((tm, 4 * M_HEADS), lambda i, j: (i, 0)),
        ],
        out_shape=[
            jax.ShapeDtypeStruct((n, N_MAIN), BF16),
            jax.ShapeDtypeStruct((n, 4 * M_HEADS), F32),
        ],
        scratch_shapes=[pltpu.VMEM((tm, D_MODEL), BF16)],
        compiler_params=pltpu.CompilerParams(
            dimension_semantics=("arbitrary", "arbitrary"), vmem_limit_bytes=VMEM_LIMIT),
        name="inproj",
    )(h0, norm1_w, w_main, w_gates)


def _attn_kernel(q_ref, k_ref, v_ref, cos_ref, sa_ref, sb_ref, neg_ref, lam_ref, sw_ref,
                 o_ref, q1_ref, q2_ref, ks_ref, vs_ref, *, lr):
    l = lr + N_META
    lp = lr + LANES
    scale = A_QK_DIM ** -0.5

    def rope(t):
        return (t * cos_ref[...] + pltpu.roll(t, LANES - ROT_DIM // 2, 1) * sa_ref[...]
                + pltpu.roll(t, ROT_DIM // 2, 1) * sb_ref[...])

    lane = lax.broadcasted_iota(jnp.int32, (l, LANES), 1)
    qr = rope(q_ref[0].astype(F32)) * scale
    q1_ref[0:l, :] = jnp.where(lane < A_QK_DIM, qr, 0.0).astype(BF16)
    q2_ref[0:l, :] = jnp.where(lane >= A_QK_DIM, qr, 0.0).astype(BF16)
    ks_ref[0:l, :] = rope(k_ref[0].astype(F32)).astype(BF16)
    vs_ref[0:l, :] = v_ref[0]
    pad = jnp.zeros((lp - l, LANES), BF16)
    q1_ref[l:lp, :] = pad
    q2_ref[l:lp, :] = pad
    ks_ref[l:lp, :] = pad
    vs_ref[l:lp, :] = pad

    lam = lam_ref[:, 0:1]

    def softmax_parts(s):
        sm = s[:, :lr]
        sl = s[:, lr:] + neg_ref[...]
        m = jnp.maximum(jnp.max(sm, axis=1, keepdims=True), jnp.max(sl, axis=1, keepdims=True))
        pm = jnp.exp(sm - m)
        pl_ = jnp.exp(sl - m)
        tot = jnp.sum(pm, axis=1, keepdims=True) + jnp.sum(pl_, axis=1, keepdims=True)
        return pm, pl_, tot

    def block(r0, tq):
        rows = pl.ds(r0, tq)
        k_all = ks_ref[...]
        p1m, p1l, t1 = softmax_parts(_nt(q1_ref[rows, :], k_all))
        p2m, p2l, t2 = softmax_parts(_nt(q2_ref[rows, :], k_all))
        r1 = 1.0 / t1
        r2 = lam / t2
        am = (p1m * r1 - p2m * r2).astype(BF16)
        al = (p1l * r1 - p2l * r2).astype(BF16)
        o = _dot(am, vs_ref[0:lr, :]) + _dot(al, vs_ref[lr:lp, :])
        o = o * lax.rsqrt(jnp.mean(o * o, axis=-1, keepdims=True) + EPS) * sw_ref[...]
        o_ref[0, rows, :] = o.astype(BF16)

    def body(i, c):
        block(pl.multiple_of(i * LANES, LANES), LANES)
        return c

    lax.fori_loop(0, lr // LANES, body, 0)
    block(lr, N_META)


def _attention(proj3, cos_t, sa_t, sb_t, neg_row, lam_row, sub_w, lr):
    b, l, _ = proj3.shape
    lp = lr + LANES
    const2 = lambda i, h: (0, 0)
    return pl.pallas_call(
        functools.partial(_attn_kernel, lr=lr),
        grid=(b, A_HEADS),
        in_specs=[
            pl.BlockSpec((1, l, LANES), lambda i, h: (i, 0, COL_AQ + h)),
            pl.BlockSpec((1, l, LANES), lambda i, h: (i, 0, COL_AK + h)),
            pl.BlockSpec((1, l, LANES), lambda i, h: (i, 0, COL_AV + h)),
            pl.BlockSpec((l, LANES), const2),
            pl.BlockSpec((l, LANES), const2),
            pl.BlockSpec((l, LANES), const2),
            pl.BlockSpec((1, LANES), const2),
            pl.BlockSpec((1, LANES), const2),
            pl.BlockSpec((1, LANES), const2),
        ],
        out_specs=pl.BlockSpec((1, l, LANES), lambda i, h: (i, 0, h)),
        out_shape=jax.ShapeDtypeStruct((b, l, A_HEADS * A_V_DIM), BF16),
        scratch_shapes=[pltpu.VMEM((lp, LANES), BF16)] * 4,
        compiler_params=pltpu.CompilerParams(
            dimension_semantics=("arbitrary", "arbitrary"), vmem_limit_bytes=VMEM_LIMIT),
        name="diff_attn",
    )(proj3, proj3, proj3, cos_t, sa_t, sb_t, neg_row, lam_row, sub_w)


def _log_sigmoid(x):
    return jnp.minimum(x, 0.0) - jnp.log(1.0 + jnp.exp(-jnp.abs(x)))


def _mlstm_kernel(q_ref, k_ref, v_ref, mo_ref, g_ref, gb_ref, nw_ref, o_ref,
                  cf_ref, cb_ref, nf_ref, nb_ref, mf_ref, mb_ref, hf_ref, hb_ref, *, lr):
    l = lr + N_META
    t = CHUNK
    nc = lr // t
    scale = M_QK_DIM ** -0.5
    row = lax.broadcasted_iota(jnp.int32, (t, t), 0)
    col = lax.broadcasted_iota(jnp.int32, (t, t), 1)
    eye = row == col

    for ref in (cf_ref, cb_ref, nf_ref, nb_ref, mf_ref, mb_ref):
        ref[...] = jnp.zeros(ref.shape, F32)

    def chunk(r0, g0, fwd, c_ref, n_ref, m_ref):
        mask = (col <= row) if fwd else (col >= row)
        gi = 0 if fwd else 2
        q = q_ref[0, pl.ds(r0, t), :]
        k = k_ref[0, pl.ds(r0, t), :]
        v = v_ref[0, pl.ds(r0, t), :]
        ig = g_ref[0, 0, gi:gi + 1, pl.ds(g0, t)] + gb_ref[0, gi:gi + 1, :]
        lf = _log_sigmoid(g_ref[0, 0, gi + 1:gi + 2, pl.ds(g0, t)] + gb_ref[0, gi + 1:gi + 2, :])
        m_prev = m_ref[0:1, 0:1]

        lf_b = jnp.broadcast_to(lf, (t, t))
        ig_b = jnp.broadcast_to(ig, (t, t))
        b_col = jnp.sum(jnp.where(mask, lf_b, 0.0), axis=1, keepdims=True)
        ig_col = jnp.sum(jnp.where(eye, ig_b, 0.0), axis=1, keepdims=True)
        a_col = ig_col - b_col
        b_row = jnp.sum(jnp.where(eye, jnp.broadcast_to(b_col, (t, t)), 0.0), axis=0, keepdims=True)
        a_row = ig - b_row
        dm = jnp.where(mask, jnp.broadcast_to(a_row, (t, t)), NEG_BIG)
        m_col = jnp.maximum(m_prev, jnp.max(dm, axis=1, keepdims=True))
        p = jnp.exp(dm - m_col)
        w_inter = jnp.exp(m_prev - m_col)

        qs = (q.astype(F32) * scale).astype(BF16)
        s = _nt(qs, k) * p
        num = w_inter * _dot(qs, c_ref[...].astype(BF16)) + _dot(s.astype(BF16), v)
        qn = jnp.sum(qs.astype(F32) * n_ref[...], axis=1, keepdims=True)
        nq = w_inter * qn + jnp.sum(s, axis=1, keepdims=True)
        den = jnp.maximum(jnp.abs(nq), jnp.exp(-(b_col + m_col)))
        h = num * (1.0 / den)

        m_end = jnp.maximum(m_prev, jnp.max(a_row, axis=1, keepdims=True))
        b_end = jnp.sum(lf, axis=1, keepdims=True)
        decay = jnp.exp(m_prev - m_end)
        ksc = k.astype(F32) * jnp.exp(a_col - m_end)
        c_ref[...] = decay * c_ref[...] + _dot(ksc.T.astype(BF16), v)
        n_ref[...] = decay * n_ref[...] + jnp.sum(ksc, axis=0, keepdims=True)
        m_ref[...] = jnp.broadcast_to(b_end + m_end, m_ref.shape)
        return h

    meta_r0 = l - t
    hf_ref[lr:l, :] = chunk(meta_r0, lr, True, cf_ref, nf_ref, mf_ref)[t - N_META:, :]

    def body(i, c):
        rf = pl.multiple_of(i * t, t)
        hf_ref[pl.ds(rf, t), :] = chunk(rf, rf, True, cf_ref, nf_ref, mf_ref)
        rb = pl.multiple_of((nc - 1 - i) * t, t)
        hb_ref[pl.ds(rb, t), :] = chunk(rb, rb, False, cb_ref, nb_ref, mb_ref)
        return c

    lax.fori_loop(0, nc, body, 0)
    hb_ref[lr:l, :] = chunk(meta_r0, lr, False, cb_ref, nb_ref, mb_ref)[t - N_META:, :]

    def finish(r0, rows):
        hs = hf_ref[pl.ds(r0, rows), :] + hb_ref[pl.ds(r0, rows), :]
        y = hs * lax.rsqrt(jnp.mean(hs * hs, axis=-1, keepdims=True) + EPS) * nw_ref[0]
        o_ref[0, pl.ds(r0, rows), :] = (y * jax.nn.sigmoid(mo_ref[0, pl.ds(r0, rows), :].astype(F32))).astype(BF16)

    def fbody(i, c):
        finish(pl.multiple_of(i * t, t), t)
        return c

    lax.fori_loop(0, nc, fbody, 0)
    finish(lr, N_META)


def _mlstm(proj3, gates_row, gate_bias, norm_w, lr):
    b, l, _ = proj3.shape
    gl = gates_row.shape[-1]
    return pl.pallas_call(
        functools.partial(_mlstm_kernel, lr=lr),
        grid=(b, M_HEADS),
        in_specs=[
            pl.BlockSpec((1, l, M_QK_DIM), lambda i, h: (i, 0, COL_MQ + h)),
            pl.BlockSpec((1, l, M_QK_DIM), lambda i, h: (i, 0, COL_MK + h)),
            pl.BlockSpec((1, l, M_V_DIM), lambda i, h: (i, 0, COL_MV + h)),
            pl.BlockSpec((1, l, M_V_DIM), lambda i, h: (i, 0, COL_MO + h)),
            pl.BlockSpec((1, 1, 8, gl), lambda i, h: (i, h, 0, 0)),
            pl.BlockSpec((1, 8, LANES), lambda i, h: (h, 0, 0)),
            pl.BlockSpec((1, 1, M_V_DIM), lambda i, h: (h, 0, 0)),
        ],
        out_specs=pl.BlockSpec((1, l, M_V_DIM), lambda i, h: (i, 0, h)),
        out_shape=jax.ShapeDtypeStruct((b, l, M_HEADS * M_V_DIM), BF16),
        scratch_shapes=[
            pltpu.VMEM((M_QK_DIM, M_V_DIM), F32), pltpu.VMEM((M_QK_DIM, M_V_DIM), F32),
            pltpu.VMEM((1, M_QK_DIM), F32), pltpu.VMEM((1, M_QK_DIM), F32),
            pltpu.VMEM((8, LANES), F32), pltpu.VMEM((8, LANES), F32),
            pltpu.VMEM((l, M_V_DIM), F32), pltpu.VMEM((l, M_V_DIM), F32),
        ],
        compiler_params=pltpu.CompilerParams(
            dimension_semantics=("arbitrary", "arbitrary"), vmem_limit_bytes=VMEM_LIMIT),
        name="mlstm",
    )(proj3, proj3, proj3, proj3, gates_row, gate_bias, norm_w)


def _merge_kernel(oa_ref, hm_ref, ga_ref, gm_ref, h_ref, wa_ref, wm_ref, wo_ref, n2_ref,
                  rw_ref, rb_ref, tri_ref,
                  h1_ref, xn_ref, te_ref, tw_ref, rk_ref, cnt_ref, carry_ref, *, tm):
    @pl.when(pl.program_id(0) == 0)
    def _():
        carry_ref[...] = jnp.zeros(carry_ref.shape, F32)

    pa = _dot(oa_ref[...], wa_ref[...])
    pm = _dot(hm_ref[...], wm_ref[...])
    merged = (jax.nn.sigmoid(ga_ref[...].astype(F32)) * pa
              + jax.nn.sigmoid(gm_ref[...].astype(F32)) * pm)
    h1 = h_ref[...] + _dot(merged.astype(BF16), wo_ref[...])
    h1_ref[...] = h1
    xn = (h1 * lax.rsqrt(jnp.mean(h1 * h1, axis=-1, keepdims=True) + EPS) * n2_ref[...]).astype(BF16)
    xn_ref[...] = xn

    logits = _nt(rw_ref[...], xn) + rb_ref[:, 0:1]
    eidx = lax.broadcasted_iota(jnp.int32, (N_EXPERTS, tm), 0)
    work = logits
    vals, hots = [], []
    for _ in range(TOP_K):
        mx = jnp.max(work, axis=0, keepdims=True)
        sel = jnp.min(jnp.where(work == mx, eidx, N_EXPERTS), axis=0, keepdims=True)
        hot = eidx == sel
        vals.append(mx)
        hots.append(hot)
        work = jnp.where(hot, -jnp.inf, work)
    ex = [jnp.exp(v - vals[0]) for v in vals]
    inv = 1.0 / (ex[0] + ex[1] + ex[2] + ex[3])
    chosen = jnp.where(hots[0] | hots[1] | hots[2] | hots[3], 1.0, 0.0)

    carry = carry_ref[:, 0:1]
    ranks = []
    for s in range(tm // LANES):
        sub = chosen[:, s * LANES:(s + 1) * LANES]
        ranks.append(_dot(sub.astype(BF16), tri_ref[...]) + carry)
        carry = carry + jnp.sum(sub, axis=1, keepdims=True)
    before = jnp.concatenate(ranks, axis=1) if len(ranks) > 1 else ranks[0]
    carry_ref[...] = jnp.broadcast_to(carry, carry_ref.shape)
    cnt_ref[...] = jnp.broadcast_to(carry, cnt_ref.shape)

    for kk in range(TOP_K):
        te_ref[kk:kk + 1, :] = jnp.sum(jnp.where(hots[kk], eidx, 0), axis=0, keepdims=True)
        tw_ref[kk:kk + 1, :] = ex[kk] * inv
        rk_ref[kk:kk + 1, :] = jnp.sum(jnp.where(hots[kk], before, 0.0), axis=0,
                                       keepdims=True).astype(jnp.int32)


def _merge(o_a, h_m, proj, h0, w_a, w_m, w_o, norm2_w, rw_t, rb_col, tri):
    n = h0.shape[0]
    tm = _pick(n, (384, 256, 128))
    row = lambda i: (i, 0)
    const = lambda i: (0, 0)
    tok = lambda i: (0, i)
    wspec = pl.BlockSpec((D_MODEL, D_MODEL), const)
    return pl.pallas_call(
        functools.partial(_merge_kernel, tm=tm),
        grid=(n // tm,),
        in_specs=[
            pl.BlockSpec((tm, D_MODEL), row),
            pl.BlockSpec((tm, D_MODEL), row),
            pl.BlockSpec((tm, D_MODEL), lambda i: (i, COL_GA)),
            pl.BlockSpec((tm, D_MODEL), lambda i: (i, COL_GM)),
            pl.BlockSpec((tm, D_MODEL), row),
            wspec, wspec, wspec,
            pl.BlockSpec((1, D_MODEL), const),
            pl.BlockSpec((N_EXPERTS, D_MODEL), const),
            pl.BlockSpec((N_EXPERTS, LANES), const),
            pl.BlockSpec((LANES, LANES), const),
        ],
        out_specs=[
            pl.BlockSpec((tm, D_MODEL), row),
            pl.BlockSpec((tm, D_MODEL), row),
            pl.BlockSpec((TOP_K, tm), tok),
            pl.BlockSpec((TOP_K, tm), tok),
            pl.BlockSpec((TOP_K, tm), tok),
            pl.BlockSpec((N_EXPERTS, LANES), const),
        ],
        out_shape=[
            jax.ShapeDtypeStruct((n, D_MODEL), F32),
            jax.ShapeDtypeStruct((n, D_MODEL), BF16),
            jax.ShapeDtypeStruct((TOP_K, n), jnp.int32),
            jax.ShapeDtypeStruct((TOP_K, n), F32),
            jax.ShapeDtypeStruct((TOP_K, n), jnp.int32),
            jax.ShapeDtypeStruct((N_EXPERTS, LANES), F32),
        ],
        scratch_shapes=[pltpu.VMEM((N_EXPERTS, LANES), F32)],
        compiler_params=pltpu.CompilerParams(
            dimension_semantics=("arbitrary",), vmem_limit_bytes=VMEM_LIMIT),
        name="merge_router",
    )(o_a, h_m, proj, proj, h0, w_a, w_m, w_o, norm2_w, rw_t, rb_col, tri)


def _expert_kernel(be_ref, nu_ref, x_ref, wg_ref, bg_ref, wu_ref, bu_ref, wd_ref, bd_ref, y_ref):
    used = pl.program_id(0) < nu_ref[0]

    @pl.when(used)
    def _():
        x = x_ref[...]
        gt = jnp.minimum(_dot(x, wg_ref[0]) + bg_ref[0], SWIGLU_LIMIT)
        up = jnp.clip(_dot(x, wu_ref[0]) + bu_ref[0], -SWIGLU_LIMIT, SWIGLU_LIMIT)
        glu = gt * jax.nn.sigmoid(SWIGLU_ALPHA * gt)
        act = ((up + 1.0) * glu).astype(BF16)
        y_ref[...] = (_dot(act, wd_ref[0]) + bd_ref[0]).astype(BF16)

    @pl.when(jnp.logical_not(used))
    def _():
        y_ref[...] = jnp.zeros(y_ref.shape, BF16)


def _experts(blk_e, n_used, xs, w_g, b_g, w_u, b_u, w_d, b_d):
    cap = xs.shape[0]
    n_blocks = cap // MOE_ROWS
    wspec = lambda d0, d1: pl.BlockSpec((1, d0, d1), lambda i, be, nu: (be[i], 0, 0))
    return pl.pallas_call(
        _expert_kernel,
        grid_spec=pltpu.PrefetchScalarGridSpec(
            num_scalar_prefetch=2,
            grid=(n_blocks,),
            in_specs=[
                pl.BlockSpec((MOE_ROWS, D_MODEL), lambda i, be, nu: (i, 0)),
                wspec(D_MODEL, D_FF), wspec(1, D_FF),
                wspec(D_MODEL, D_FF), wspec(1, D_FF),
                wspec(D_FF, D_MODEL), wspec(1, D_MODEL),
            ],
            out_specs=pl.BlockSpec((MOE_ROWS, D_MODEL), lambda i, be, nu: (i, 0)),
        ),
        out_shape=jax.ShapeDtypeStruct((cap, D_MODEL), BF16),
        compiler_params=pltpu.CompilerParams(
            dimension_semantics=("arbitrary",), vmem_limit_bytes=VMEM_LIMIT),
        name="experts",
    )(blk_e, n_used, xs, w_g, b_g, w_u, b_u, w_d, b_d)


def _combine_kernel(h_ref, y_ref, w_ref, fw_ref, o_ref):
    acc = h_ref[0]
    for kk in range(TOP_K):
        acc = acc + w_ref[0, :, kk:kk + 1] * y_ref[kk, 0].astype(F32)
    o_ref[0] = acc * lax.rsqrt(jnp.mean(acc * acc, axis=-1, keepdims=True) + EPS) * fw_ref[...]


def _combine(h1, yg, wts, final_w, b0, nb, lr):
    tr = _pick(lr, (512, 256, 128))
    return pl.pallas_call(
        _combine_kernel,
        grid=(nb, lr // tr),
        in_specs=[
            pl.BlockSpec((1, tr, D_MODEL), lambda i, j: (b0 + i, j, 0)),
            pl.BlockSpec((TOP_K, 1, tr, D_MODEL), lambda i, j: (0, b0 + i, j, 0)),
            pl.BlockSpec((1, tr, TOP_K), lambda i, j: (b0 + i, j, 0)),
            pl.BlockSpec((1, D_MODEL), lambda i, j: (0, 0)),
        ],
        out_specs=pl.BlockSpec((1, tr, D_MODEL), lambda i, j: (i, j, 0)),
        out_shape=jax.ShapeDtypeStruct((nb, lr, D_MODEL), F32),
        compiler_params=pltpu.CompilerParams(
            dimension_semantics=("arbitrary", "arbitrary"), vmem_limit_bytes=VMEM_LIMIT),
        name="combine",
    )(h1, yg, wts, final_w)


def _rope_tables(lr):
    l = lr + N_META
    half = ROT_DIM // 2
    pos = jnp.concatenate([jnp.arange(N_META, l, dtype=F32), jnp.arange(N_META, dtype=F32)])
    inv_freq = ROPE_THETA ** (-jnp.arange(0, ROT_DIM, 2, dtype=F32) / ROT_DIM)
    ang = pos[:, None] * inv_freq[None, :]
    cos, sin = jnp.cos(ang), jnp.sin(ang)
    z = jnp.zeros((l, A_QK_DIM - ROT_DIM), F32)
    zh = jnp.zeros((l, half), F32)
    cos64 = jnp.concatenate([cos, cos, z + 1.0], axis=1)
    sa64 = jnp.concatenate([-sin, zh, z], axis=1)
    sb64 = jnp.concatenate([zh, sin, z], axis=1)
    dup = lambda a: jnp.concatenate([a, a], axis=1)
    return dup(cos64), dup(sa64), dup(sb64)


def kernel(x_prompt, x_sample, meta_tokens, norm1_w, w_in, lambda_q1, lambda_k1, lambda_q2, lambda_k2, attn_subln_w, mlstm_gate_b, mlstm_norm_w, w_br_attn, w_br_mlstm, w_out, norm2_w, router_w, router_b, w_gate, b_gate, w_up, b_up, w_down, b_down, final_norm_w):
    bp, lr, _ = x_prompt.shape
    bs = x_sample.shape[0]
    b = bp + bs
    l = lr + N_META
    n = b * l

    w_in0 = w_in[0]
    g0 = 3 * 1024 + 2 * 512 + 2 * 1024
    w_main = jnp.concatenate([w_in0[:, :g0], w_in0[:, g0 + 4 * M_HEADS:]], axis=1).astype(BF16)
    w_gates = jnp.pad(w_in0[:, g0:g0 + 4 * M_HEADS], ((0, 0), (0, LANES - 4 * M_HEADS))).astype(BF16)
    lam = (jnp.exp(jnp.sum(lambda_q1[0] * lambda_k1[0])) - jnp.exp(jnp.sum(lambda_q2[0] * lambda_k2[0]))
           + LAM_INIT)
    lam_row = jnp.full((1, LANES), lam, F32)
    sub_w = (attn_subln_w[0] * (1.0 - LAM_INIT)).reshape(1, A_V_DIM)
    neg_row = jnp.where(jnp.arange(LANES) < N_META, 0.0, NEG_BIG).astype(F32).reshape(1, LANES)
    cos_t, sa_t, sb_t = _rope_tables(lr)
    gate_bias = jnp.broadcast_to(
        jnp.pad(mlstm_gate_b[0].T, ((0, 0), (0, 4)))[:, :, None], (M_HEADS, 8, LANES)).astype(F32)
    norm_w_m = mlstm_norm_w[0].reshape(M_HEADS, 1, M_V_DIM)
    rw_t = router_w[0].T.astype(BF16)
    rb_col = jnp.broadcast_to(router_b[0][:, None], (N_EXPERTS, LANES)).astype(F32)
    tri = (jnp.arange(LANES)[:, None] < jnp.arange(LANES)[None, :]).astype(BF16)

    x_all = jnp.concatenate([x_prompt, x_sample], axis=0)
    meta_b = jnp.broadcast_to(meta_tokens[None].astype(x_all.dtype), (b, N_META, D_MODEL))
    h0 = jnp.concatenate([x_all, meta_b], axis=1).reshape(n, D_MODEL)

    proj, gates = _inproj(h0, norm1_w[0].reshape(1, D_MODEL), w_main, w_gates)
    proj3 = proj.reshape(b, l, N_MAIN)

    o_a = _attention(proj3, cos_t, sa_t, sb_t, neg_row, lam_row, sub_w, lr)

    g4 = gates.reshape(b, l, 4, M_HEADS).transpose(0, 3, 2, 1)
    off = jnp.array([NEG_BIG, 1e4, NEG_BIG, 1e4], F32)[None, None, :, None]
    fill = jnp.broadcast_to(off, (b, M_HEADS, 4, CHUNK - N_META))
    g_row = jnp.concatenate([g4[..., :lr], fill, g4[..., lr:]], axis=-1)
    g_row = jnp.pad(g_row, ((0, 0), (0, 0), (0, 4), (0, 0)))
    h_m = _mlstm(proj3, g_row, gate_bias, norm_w_m, lr)

    h1, xn2, top_e, top_w, rank, cnt = _merge(
        o_a.reshape(n, D_MODEL), h_m.reshape(n, D_MODEL), proj, h0,
        w_br_attn[0].astype(BF16), w_br_mlstm[0].astype(BF16), w_out[0].astype(BF16),
        norm2_w[0].reshape(1, D_MODEL), rw_t, rb_col, tri)

    counts = cnt[:, 0].astype(jnp.int32)
    padded = (counts + MOE_ROWS - 1) // MOE_ROWS * MOE_ROWS
    pends = jnp.cumsum(padded)
    pstarts = pends - padded
    dest = pstarts[top_e] + rank
    n_blocks = (n * TOP_K + N_EXPERTS * (MOE_ROWS - 1) + MOE_ROWS - 1) // MOE_ROWS
    cap = n_blocks * MOE_ROWS
    tok = jnp.broadcast_to(jnp.arange(n, dtype=jnp.int32)[None], (TOP_K, n))
    slot_t = jnp.zeros((cap,), jnp.int32).at[dest.reshape(-1)].set(tok.reshape(-1))
    blk_e = jnp.minimum(
        jnp.searchsorted(pends, jnp.arange(n_blocks, dtype=jnp.int32) * MOE_ROWS, side='right'),
        N_EXPERTS - 1).astype(jnp.int32)
    n_used = (pends[-1:] // MOE_ROWS).astype(jnp.int32)

    xs = jnp.take(xn2, slot_t, axis=0)
    y = _experts(blk_e, n_used, xs,
                 w_gate[0].astype(BF16), b_gate[0].reshape(N_EXPERTS, 1, D_FF),
                 w_up[0].astype(BF16), b_up[0].reshape(N_EXPERTS, 1, D_FF),
                 w_down[0].astype(BF16), b_down[0].reshape(N_EXPERTS, 1, D_MODEL))
    yg = jnp.take(y, dest.reshape(-1), axis=0).reshape(TOP_K, b, l, D_MODEL)
    wts = top_w.T.reshape(b, l, TOP_K)
    h1_3 = h1.reshape(b, l, D_MODEL)
    fw = final_norm_w.reshape(1, D_MODEL)
    y_prompt = _combine(h1_3, yg, wts, fw, 0, bp, lr)
    y_sample = _combine(h1_3, yg, wts, fw, bp, bs, lr)
    return (y_prompt, y_sample)
```

```python
import functools
import math

import jax
import jax.numpy as jnp
import numpy as np
from jax import lax
from jax.experimental import pallas as pl
from jax.experimental.pallas import tpu as pltpu

F32 = jnp.float32
BF16 = jnp.bfloat16

D_MODEL = 1024
N_META = 16
EPS = 1e-6
A_HEADS = 8
A_QK_DIM = 64
A_V_DIM = 128
ROT_DIM = 16
ROPE_THETA = 500000.0
M_HEADS = 4
M_QK_DIM = 128
M_V_DIM = 256
CHUNK = 128
N_EXPERTS = 32
TOP_K = 4
D_FF = 1024
SWIGLU_LIMIT = 7.0
SWIGLU_ALPHA = 1.702
LAM_INIT = 0.8 - 0.6 * math.exp(-0.3 * 0)

LANES = 128
NEG_BIG = -1e30
MOE_ROWS = 256
MLSTM_HPS = 2
COL_AQ, COL_AK, COL_AV = 0, 8, 16
COL_MQ, COL_MK = 24, 28
COL_MV, COL_MO = 16, 20
COL_GA, COL_GM = 6, 7
N_MAIN = 8192

VMEM_LIMIT = 56 * 1024 * 1024


def _pick(n, prefs):
    for p in prefs:
        if n % p == 0:
            return p
    raise ValueError(f"no tile for {n}")


def _nt(a, b):
    return lax.dot_general(a, b, (((1,), (1,)), ((), ())), preferred_element_type=F32)


def _dot(a, b):
    return jnp.dot(a, b, preferred_element_type=F32)


def _inproj_kernel(x_ref, n1_ref, w_ref, wg_ref, o_ref, g_ref, xn_ref):
    @pl.when(pl.program_id(1) == 0)
    def _():
        x = x_ref[...]
        ms = jnp.mean(x * x, axis=-1, keepdims=True)
        xn = (x * lax.rsqrt(ms + EPS) * n1_ref[...]).astype(BF16)
        xn_ref[...] = xn
        g_ref[...] = _dot(xn, wg_ref[...])[:, :4 * M_HEADS]

    o_ref[...] = _dot(xn_ref[...], w_ref[...]).astype(BF16)


def _inproj(h0, norm1_w, w_main, w_gates):
    n = h0.shape[0]
    tm = _pick(n, (1152, 768, 512, 384, 256, 128))
    tn = 2048
    return pl.pallas_call(
        _inproj_kernel,
        grid=(n // tm, N_MAIN // tn),
        in_specs=[
            pl.BlockSpec((tm, D_MODEL), lambda i, j: (i, 0)),
            pl.BlockSpec((1, D_MODEL), lambda i, j: (0, 0)),
            pl.BlockSpec((D_MODEL, tn), lambda i, j: (0, j)),
            pl.BlockSpec((D_MODEL, LANES), lambda i, j: (0, 0)),
        ],
        out_specs=[
            pl.BlockSpec((tm, tn), lambda i, j: (i, j)),
            pl.BlockSpec((tm, 4 * M_HEADS), lambda i, j: (i, 0)),
        ],
        out_shape=[
            jax.ShapeDtypeStruct((n, N_MAIN), BF16),
            jax.ShapeDtypeStruct((n, 4 * M_HEADS), F32),
        ],
        scratch_shapes=[pltpu.VMEM((tm, D_MODEL), BF16)],
        compiler_params=pltpu.CompilerParams(
            dimension_semantics=("arbitrary", "arbitrary"), vmem_limit_bytes=VMEM_LIMIT),
        name="inproj",
    )(h0, norm1_w, w_main, w_gates)


def _attn_kernel(q_ref, k_ref, v_ref, cos_ref, sa_ref, sb_ref, neg_ref, lam_ref, sw_ref,
                 o_ref, q1_ref, q2_ref, vs_ref, kt_ref, sa1_ref, sa2_ref, sb1_ref, sb2_ref, *, lr):
    l = lr + N_META
    lp = lr + LANES
    nblk = lr // LANES
    scale = A_QK_DIM ** -0.5 * math.log2(math.e)

    def rope(t):
        return (t * cos_ref[...] + pltpu.roll(t, LANES - ROT_DIM // 2, 1) * sa_ref[...]
                + pltpu.roll(t, ROT_DIM // 2, 1) * sb_ref[...])

    lane = lax.broadcasted_iota(jnp.int32, (l, LANES), 1)
    qr = rope(q_ref[0].astype(F32)) * scale
    q1_ref[0:l, :] = jnp.where(lane < A_QK_DIM, qr, 0.0).astype(BF16)
    q2_ref[0:l, :] = jnp.where(lane >= A_QK_DIM, qr, 0.0).astype(BF16)
    vs_ref[0:l, :] = v_ref[0]
    pad = jnp.zeros((lp - l, LANES), BF16)
    q1_ref[l:lp, :] = pad
    q2_ref[l:lp, :] = pad
    vs_ref[l:lp, :] = pad
    kr = rope(k_ref[0].astype(F32))
    for c in range(nblk):
        kt_ref[:, c * LANES:(c + 1) * LANES] = kr[c * LANES:(c + 1) * LANES, :].T.astype(BF16)
    k_tail = jnp.concatenate([kr[lr:l, :], jnp.zeros((lp - l, LANES), F32)], axis=0)
    kt_ref[:, lr:lp] = k_tail.T.astype(BF16)

    lam = lam_ref[:, 0:1]

    def scores(r0, s1_ref, s2_ref):
        rows = pl.ds(r0, LANES)
        k_all = kt_ref[...]
        s1_ref[...] = _dot(q1_ref[rows, :], k_all)
        s2_ref[...] = _dot(q2_ref[rows, :], k_all)

    def softmax_parts(s_ref):
        sm = s_ref[:, :lr]
        sl = s_ref[:, lr:] + neg_ref[...]
        m = jnp.maximum(jnp.max(sm, axis=1, keepdims=True), jnp.max(sl, axis=1, keepdims=True))
        pm = jnp.exp2(sm - m)
        pl_ = jnp.exp2(sl - m)
        tot = jnp.sum(pm, axis=1, keepdims=True) + jnp.sum(pl_, axis=1, keepdims=True)
        return pm, pl_, tot

    def finish(r0, nrows, s1_ref, s2_ref):
        p1m, p1l, t1 = softmax_parts(s1_ref)
        p2m, p2l, t2 = softmax_parts(s2_ref)
        c = lam * t1 / t2
        am = (p1m - p2m * c).astype(BF16)
        al = (p1l - p2l * c).astype(BF16)
        o = (_dot(am, vs_ref[0:lr, :]) + _dot(al, vs_ref[lr:lp, :])) * (1.0 / t1)
        o = o * lax.rsqrt(jnp.mean(o * o, axis=-1, keepdims=True) + EPS) * sw_ref[...]
        o_ref[0, pl.ds(r0, nrows), :] = o[:nrows].astype(BF16)

    scores(0, sa1_ref, sa2_ref)

    def body(j, c):
        ra = pl.multiple_of(j * (2 * LANES), LANES)
        rb = ra + LANES
        scores(rb, sb1_ref, sb2_ref)
        finish(ra, LANES, sa1_ref, sa2_ref)
        scores(rb + LANES, sa1_ref, sa2_ref)
        finish(rb, LANES, sb1_ref, sb2_ref)
        return c

    lax.fori_loop(0, nblk // 2, body, 0)
    finish(lr, N_META, sa1_ref, sa2_ref)


def _attention(proj3, cos_t, sa_t, sb_t, neg_row, lam_row, sub_w, lr):
    b, l, _ = proj3.shape
    lp = lr + LANES
    const2 = lambda i, h: (0, 0)
    return pl.pallas_call(
        functools.partial(_attn_kernel, lr=lr),
        grid=(b, A_HEADS),
        in_specs=[
            pl.BlockSpec((1, l, LANES), lambda i, h: (i, 0, COL_AQ + h)),
            pl.BlockSpec((1, l, LANES), lambda i, h: (i, 0, COL_AK + h)),
            pl.BlockSpec((1, l, LANES), lambda i, h: (i, 0, COL_AV + h)),
            pl.BlockSpec((l, LANES), const2),
            pl.BlockSpec((l, LANES), const2),
            pl.BlockSpec((l, LANES), const2),
            pl.BlockSpec((1, LANES), const2),
            pl.BlockSpec((1, LANES), const2),
            pl.BlockSpec((1, LANES), const2),
        ],
        out_specs=pl.BlockSpec((1, l, LANES), lambda i, h: (i, 0, h)),
        out_shape=jax.ShapeDtypeStruct((b, l, A_HEADS * A_V_DIM), BF16),
        scratch_shapes=([pltpu.VMEM((lp, LANES), BF16)] * 3 + [pltpu.VMEM((LANES, lp), BF16)]
                        + [pltpu.VMEM((LANES, lp), F32)] * 4),
        compiler_params=pltpu.CompilerParams(
            dimension_semantics=("arbitrary", "arbitrary"), vmem_limit_bytes=VMEM_LIMIT),
        name="diff_attn",
    )(proj3, proj3, proj3, cos_t, sa_t, sb_t, neg_row, lam_row, sub_w)


def _log_sigmoid(x):
    return jnp.minimum(x, 0.0) - jnp.log(1.0 + jnp.exp(-jnp.abs(x)))


def _mlstm_kernel(q_ref, k_ref, v_ref, mo_ref, g_ref, gb_ref, nw_ref, o_ref,
                  c_ref, m_ref, kt_ref, hf_ref, hb_ref, *, lr):
    l = lr + N_META
    t = CHUNK
    nc = lr // t
    dv = M_V_DIM
    scale = M_QK_DIM ** -0.5
    row = lax.broadcasted_iota(jnp.int32, (t, t), 0)
    col = lax.broadcasted_iota(jnp.int32, (t, t), 1)
    eye = row == col
    ones_tile = jnp.where(col == 0, 1.0, 0.0).astype(BF16)
    meta_r0 = l - t

    c_ref[...] = jnp.zeros(c_ref.shape, F32)
    m_ref[...] = jnp.zeros(m_ref.shape, F32)

    for j in range(MLSTM_HPS):
        for c in range(nc + 1):
            r0 = c * t if c < nc else meta_r0
            kc = k_ref[0, r0:r0 + t, j * M_QK_DIM:(j + 1) * M_QK_DIM]
            kt_ref[j, :, c * t:(c + 1) * t] = kc.astype(F32).T.astype(BF16)

    def chunk(j, r0, g0, fwd):
        d = 0 if fwd else 1
        mask = (col <= row) if fwd else (col >= row)
        gi = 0 if fwd else 2
        q = q_ref[0, pl.ds(r0, t), j * M_QK_DIM:(j + 1) * M_QK_DIM]
        v = v_ref[0, pl.ds(r0, t), j * dv:(j + 1) * dv]
        kt = kt_ref[j, :, pl.ds(g0, t)]
        ig = g_ref[0, j, gi:gi + 1, pl.ds(g0, t)] + gb_ref[j, gi:gi + 1, :]
        lf = _log_sigmoid(g_ref[0, j, gi + 1:gi + 2, pl.ds(g0, t)] + gb_ref[j, gi + 1:gi + 2, :])
        m_prev = m_ref[j, d, 0:1, 0:1]

        b_col = jnp.sum(jnp.where(mask, jnp.broadcast_to(lf, (t, t)), 0.0), axis=1, keepdims=True)
        b_row = jnp.sum(jnp.where(eye, jnp.broadcast_to(b_col, (t, t)), 0.0), axis=0, keepdims=True)
        a_row = ig - b_row
        dm = jnp.where(mask, jnp.broadcast_to(a_row, (t, t)), NEG_BIG)
        m_col = jnp.maximum(m_prev, jnp.max(dm, axis=1, keepdims=True))
        p = jnp.exp(dm - m_col)
        w_inter = jnp.exp(m_prev - m_col)

        qs = (q.astype(F32) * scale).astype(BF16)
        s = _dot(qs, kt) * p
        v_aug = jnp.concatenate([v, ones_tile], axis=1)
        num = w_inter * _dot(qs, c_ref[j, d].astype(BF16)) + _dot(s.astype(BF16), v_aug)
        nq = num[:, dv:dv + 1]
        den = jnp.maximum(jnp.abs(nq), jnp.exp(-(b_col + m_col)))
        h = num[:, :dv] * (1.0 / den)

        m_end = jnp.maximum(m_prev, jnp.max(a_row, axis=1, keepdims=True))
        b_end = jnp.sum(lf, axis=1, keepdims=True)
        decay = jnp.exp(m_prev - m_end)
        kts = (kt.astype(F32) * jnp.exp(a_row - m_end)).astype(BF16)
        c_ref[j, d] = decay * c_ref[j, d] + _dot(kts, v_aug)
        m_ref[j, d] = jnp.broadcast_to(b_end + m_end, (8, LANES))
        return h

    for j in range(MLSTM_HPS):
        hf_ref[lr:l, j * dv:(j + 1) * dv] = chunk(j, meta_r0, lr, True)[t - N_META:, :]

    def body(i, c):
        rf = pl.multiple_of(i * t, t)
        rb = pl.multiple_of((nc - 1 - i) * t, t)
        for j in range(MLSTM_HPS):
            hf_ref[pl.ds(rf, t), j * dv:(j + 1) * dv] = chunk(j, rf, rf, True)
            hb_ref[pl.ds(rb, t), j * dv:(j + 1) * dv] = chunk(j, rb, rb, False)
        return c

    lax.fori_loop(0, nc, body, 0)
    for j in range(MLSTM_HPS):
        hb_ref[lr:l, j * dv:(j + 1) * dv] = chunk(j, meta_r0, lr, False)[t - N_META:, :]

    def finish(r0, rows):
        for j in range(MLSTM_HPS):
            cs = slice(j * dv, (j + 1) * dv)
            hs = hf_ref[pl.ds(r0, rows), cs] + hb_ref[pl.ds(r0, rows), cs]
            y = hs * lax.rsqrt(jnp.mean(hs * hs, axis=-1, keepdims=True) + EPS) * nw_ref[j]
            gate = jax.nn.sigmoid(mo_ref[0, pl.ds(r0, rows), cs].astype(F32))
            o_ref[0, pl.ds(r0, rows), cs] = (y * gate).astype(BF16)

    def fbody(i, c):
        finish(pl.multiple_of(i * t, t), t)
        return c

    lax.fori_loop(0, nc, fbody, 0)
    finish(lr, N_META)


def _mlstm(proj3, gates_row, gate_bias, norm_w, lr):
    b, l, _ = proj3.shape
    gl = gates_row.shape[-1]
    hps = MLSTM_HPS
    qk_w = hps * M_QK_DIM
    v_w = hps * M_V_DIM
    q0 = COL_MQ * LANES // qk_w
    k0 = COL_MK * LANES // qk_w
    v0 = COL_MV * M_V_DIM // v_w
    o0 = COL_MO * M_V_DIM // v_w
    return pl.pallas_call(
        functools.partial(_mlstm_kernel, lr=lr),
        grid=(b, M_HEADS // hps),
        in_specs=[
            pl.BlockSpec((1, l, qk_w), lambda i, h: (i, 0, q0 + h)),
            pl.BlockSpec((1, l, qk_w), lambda i, h: (i, 0, k0 + h)),
            pl.BlockSpec((1, l, v_w), lambda i, h: (i, 0, v0 + h)),
            pl.BlockSpec((1, l, v_w), lambda i, h: (i, 0, o0 + h)),
            pl.BlockSpec((1, hps, 8, gl), lambda i, h: (i, h, 0, 0)),
            pl.BlockSpec((hps, 8, LANES), lambda i, h: (h, 0, 0)),
            pl.BlockSpec((hps, 1, M_V_DIM), lambda i, h: (h, 0, 0)),
        ],
        out_specs=pl.BlockSpec((1, l, v_w), lambda i, h: (i, 0, h)),
        out_shape=jax.ShapeDtypeStruct((b, l, M_HEADS * M_V_DIM), BF16),
        scratch_shapes=[
            pltpu.VMEM((hps, 2, M_QK_DIM, M_V_DIM + LANES), F32),
            pltpu.VMEM((hps, 2, 8, LANES), F32),
            pltpu.VMEM((hps, M_QK_DIM, lr + CHUNK), BF16),
            pltpu.VMEM((l, v_w), F32), pltpu.VMEM((l, v_w), F32),
        ],
        compiler_params=pltpu.CompilerParams(
            dimension_semantics=("arbitrary", "arbitrary"), vmem_limit_bytes=VMEM_LIMIT),
        name="mlstm",
    )(proj3, proj3, proj3, proj3, gates_row, gate_bias, norm_w)


def _merge_kernel(oa_ref, hm_ref, ga_ref, gm_ref, h_ref, wa_ref, wm_ref, wo_ref, n2_ref,
                  rw_ref, rb_ref, tri_ref,
                  h1_ref, xn_ref, te_ref, tw_ref, rk_ref, cnt_ref, carry_ref, *, tm):
    @pl.when(pl.program_id(0) == 0)
    def _():
        carry_ref[...] = jnp.zeros(carry_ref.shape, F32)

    pa = _dot(oa_ref[...], wa_ref[...])
    pm = _dot(hm_ref[...], wm_ref[...])
    merged = (jax.nn.sigmoid(ga_ref[...].astype(F32)) * pa
              + jax.nn.sigmoid(gm_ref[...].astype(F32)) * pm)
    h1 = h_ref[...] + _dot(merged.astype(BF16), wo_ref[...])
    h1_ref[...] = h1
    xn = (h1 * lax.rsqrt(jnp.mean(h1 * h1, axis=-1, keepdims=True) + EPS) * n2_ref[...]).astype(BF16)
    xn_ref[...] = xn

    logits = _nt(rw_ref[...], xn) + rb_ref[:, 0:1]
    eidx = lax.broadcasted_iota(jnp.int32, (N_EXPERTS, tm), 0)
    work = logits
    vals, hots = [], []
    for _ in range(TOP_K):
        mx = jnp.max(work, axis=0, keepdims=True)
        sel = jnp.min(jnp.where(work == mx, eidx, N_EXPERTS), axis=0, keepdims=True)
        hot = eidx == sel
        vals.append(mx)
        hots.append(hot)
        work = jnp.where(hot, -jnp.inf, work)
    ex = [jnp.exp(v - vals[0]) for v in vals]
    inv = 1.0 / (ex[0] + ex[1] + ex[2] + ex[3])
    chosen = jnp.where(hots[0] | hots[1] | hots[2] | hots[3], 1.0, 0.0)

    carry = carry_ref[:, 0:1]
    ranks = []
    for s in range(tm // LANES):
        sub = chosen[:, s * LANES:(s + 1) * LANES]
        ranks.append(_dot(sub.astype(BF16), tri_ref[...]) + carry)
        carry = carry + jnp.sum(sub, axis=1, keepdims=True)
    before = jnp.concatenate(ranks, axis=1) if len(ranks) > 1 else ranks[0]
    carry_ref[...] = jnp.broadcast_to(carry, carry_ref.shape)
    cnt_ref[...] = jnp.broadcast_to(carry, cnt_ref.shape)

    for kk in range(TOP_K):
        te_ref[kk:kk + 1, :] = jnp.sum(jnp.where(hots[kk], eidx, 0), axis=0, keepdims=True)
        tw_ref[kk:kk + 1, :] = ex[kk] * inv
        rk_ref[kk:kk + 1, :] = jnp.sum(jnp.where(hots[kk], before, 0.0), axis=0,
                                       keepdims=True).astype(jnp.int32)


def _merge(o_a, h_m, proj, h0, w_a, w_m, w_o, norm2_w, rw_t, rb_col, tri):
    n = h0.shape[0]
    tm = _pick(n, (384, 256, 128))
    row = lambda i: (i, 0)
    const = lambda i: (0, 0)
    tok = lambda i: (0, i)
    wspec = pl.BlockSpec((D_MODEL, D_MODEL), const)
    return pl.pallas_call(
        functools.partial(_merge_kernel, tm=tm),
        grid=(n // tm,),
        in_specs=[
            pl.BlockSpec((tm, D_MODEL), row),
            pl.BlockSpec((tm, D_MODEL), row),
            pl.BlockSpec((tm, D_MODEL), lambda i: (i, COL_GA)),
            pl.BlockSpec((tm, D_MODEL), lambda i: (i, COL_GM)),
            pl.BlockSpec((tm, D_MODEL), row),
            wspec, wspec, wspec,
            pl.BlockSpec((1, D_MODEL), const),
            pl.BlockSpec((N_EXPERTS, D_MODEL), const),
            pl.BlockSpec((N_EXPERTS, LANES), const),
            pl.BlockSpec((LANES, LANES), const),
        ],
        out_specs=[
            pl.BlockSpec((tm, D_MODEL), row),
            pl.BlockSpec((tm, D_MODEL), row),
            pl.BlockSpec((TOP_K, tm), tok),
            pl.BlockSpec((TOP_K, tm), tok),
            pl.BlockSpec((TOP_K, tm), tok),
            pl.BlockSpec((N_EXPERTS, LANES), const),
        ],
        out_shape=[
            jax.ShapeDtypeStruct((n, D_MODEL), F32),
            jax.ShapeDtypeStruct((n, D_MODEL), BF16),
            jax.ShapeDtypeStruct((TOP_K, n), jnp.int32),
            jax.ShapeDtypeStruct((TOP_K, n), F32),
            jax.ShapeDtypeStruct((TOP_K, n), jnp.int32),
            jax.ShapeDtypeStruct((N_EXPERTS, LANES), F32),
        ],
        scratch_shapes=[pltpu.VMEM((N_EXPERTS, LANES), F32)],
        compiler_params=pltpu.CompilerParams(
            dimension_semantics=("arbitrary",), vmem_limit_bytes=VMEM_LIMIT),
        name="merge_router",
    )(o_a, h_m, proj, proj, h0, w_a, w_m, w_o, norm2_w, rw_t, rb_col, tri)


def _expert_kernel(be_ref, nu_ref, x_ref, wg_ref, bg_ref, wu_ref, bu_ref, wd_ref, bd_ref, y_ref,
                   wg_s, wu_s, wd_s):
    i = pl.program_id(0)
    used = i < nu_ref[0]

    @pl.when(jnp.logical_or(i == 0, be_ref[i] != be_ref[jnp.maximum(i - 1, 0)]))
    def _():
        wg_s[...] = wg_ref[0].astype(BF16)
        wu_s[...] = wu_ref[0].astype(BF16)
        wd_s[...] = wd_ref[0].astype(BF16)

    @pl.when(used)
    def _():
        x = x_ref[...]
        gt = jnp.minimum(_dot(x, wg_s[...]) + bg_ref[0], SWIGLU_LIMIT)
        up = jnp.clip(_dot(x, wu_s[...]) + bu_ref[0], -SWIGLU_LIMIT, SWIGLU_LIMIT)
        glu = gt * jax.nn.sigmoid(SWIGLU_ALPHA * gt)
        act = ((up + 1.0) * glu).astype(BF16)
        y_ref[...] = (_dot(act, wd_s[...]) + bd_ref[0]).astype(BF16)

    @pl.when(jnp.logical_not(used))
    def _():
        y_ref[...] = jnp.zeros(y_ref.shape, BF16)


def _experts(blk_e, n_used, xs, w_g, b_g, w_u, b_u, w_d, b_d):
    cap = xs.shape[0]
    n_blocks = cap // MOE_ROWS
    wspec = lambda d0, d1: pl.BlockSpec((1, d0, d1), lambda i, be, nu: (be[i], 0, 0))
    return pl.pallas_call(
        _expert_kernel,
        grid_spec=pltpu.PrefetchScalarGridSpec(
            num_scalar_prefetch=2,
            grid=(n_blocks,),
            in_specs=[
                pl.BlockSpec((MOE_ROWS, D_MODEL), lambda i, be, nu: (i, 0)),
                wspec(D_MODEL, D_FF), wspec(1, D_FF),
                wspec(D_MODEL, D_FF), wspec(1, D_FF),
                wspec(D_FF, D_MODEL), wspec(1, D_MODEL),
            ],
            out_specs=pl.BlockSpec((MOE_ROWS, D_MODEL), lambda i, be, nu: (i, 0)),
            scratch_shapes=[pltpu.VMEM((D_MODEL, D_FF), BF16), pltpu.VMEM((D_MODEL, D_FF), BF16),
                            pltpu.VMEM((D_FF, D_MODEL), BF16)],
        ),
        out_shape=jax.ShapeDtypeStruct((cap, D_MODEL), BF16),
        compiler_params=pltpu.CompilerParams(
            dimension_semantics=("arbitrary",), vmem_limit_bytes=VMEM_LIMIT),
        name="experts",
    )(blk_e, n_used, xs, w_g, b_g, w_u, b_u, w_d, b_d)


def _combine_kernel(h_ref, y_ref, w_ref, fw_ref, o_ref):
    acc = h_ref[0]
    for kk in range(TOP_K):
        acc = acc + w_ref[0, :, kk:kk + 1] * y_ref[kk, 0].astype(F32)
    o_ref[0] = acc * lax.rsqrt(jnp.mean(acc * acc, axis=-1, keepdims=True) + EPS) * fw_ref[...]


def _combine(h1, yg, wts, final_w, b0, nb, lr):
    tr = _pick(lr, (512, 256, 128))
    return pl.pallas_call(
        _combine_kernel,
        grid=(nb, lr // tr),
        in_specs=[
            pl.BlockSpec((1, tr, D_MODEL), lambda i, j: (b0 + i, j, 0)),
            pl.BlockSpec((TOP_K, 1, tr, D_MODEL), lambda i, j: (0, b0 + i, j, 0)),
            pl.BlockSpec((1, tr, TOP_K), lambda i, j: (b0 + i, j, 0)),
            pl.BlockSpec((1, D_MODEL), lambda i, j: (0, 0)),
        ],
        out_specs=pl.BlockSpec((1, tr, D_MODEL), lambda i, j: (i, j, 0)),
        out_shape=jax.ShapeDtypeStruct((nb, lr, D_MODEL), F32),
        compiler_params=pltpu.CompilerParams(
            dimension_semantics=("arbitrary", "arbitrary"), vmem_limit_bytes=VMEM_LIMIT),
        name="combine",
    )(h1, yg, wts, final_w)


def _rope_tables(lr):
    l = lr + N_META
    half = ROT_DIM // 2
    pos = jnp.concatenate([jnp.arange(N_META, l, dtype=F32), jnp.arange(N_META, dtype=F32)])
    inv_freq = ROPE_THETA ** (-jnp.arange(0, ROT_DIM, 2, dtype=F32) / ROT_DIM)
    ang = pos[:, None] * inv_freq[None, :]
    cos, sin = jnp.cos(ang), jnp.sin(ang)
    z = jnp.zeros((l, A_QK_DIM - ROT_DIM), F32)
    zh = jnp.zeros((l, half), F32)
    cos64 = jnp.concatenate([cos, cos, z + 1.0], axis=1)
    sa64 = jnp.concatenate([-sin, zh, z], axis=1)
    sb64 = jnp.concatenate([zh, sin, z], axis=1)
    dup = lambda a: jnp.concatenate([a, a], axis=1)
    return dup(cos64), dup(sa64), dup(sb64)


def kernel(x_prompt, x_sample, meta_tokens, norm1_w, w_in, lambda_q1, lambda_k1, lambda_q2, lambda_k2, attn_subln_w, mlstm_gate_b, mlstm_norm_w, w_br_attn, w_br_mlstm, w_out, norm2_w, router_w, router_b, w_gate, b_gate, w_up, b_up, w_down, b_down, final_norm_w):
    bp, lr, _ = x_prompt.shape
    bs = x_sample.shape[0]
    b = bp + bs
    l = lr + N_META
    n = b * l

    w_in0 = w_in[0]
    g0 = 3 * 1024 + 2 * 512 + 2 * 1024
    w_main = jnp.concatenate([w_in0[:, :g0], w_in0[:, g0 + 4 * M_HEADS:]], axis=1).astype(BF16)
    w_gates = jnp.pad(w_in0[:, g0:g0 + 4 * M_HEADS], ((0, 0), (0, LANES - 4 * M_HEADS))).astype(BF16)
    lam = (jnp.exp(jnp.sum(lambda_q1[0] * lambda_k1[0])) - jnp.exp(jnp.sum(lambda_q2[0] * lambda_k2[0]))
           + LAM_INIT)
    lam_row = jnp.full((1, LANES), lam, F32)
    sub_w = (attn_subln_w[0] * (1.0 - LAM_INIT)).reshape(1, A_V_DIM)
    neg_row = jnp.where(jnp.arange(LANES) < N_META, 0.0, NEG_BIG).astype(F32).reshape(1, LANES)
    cos_t, sa_t, sb_t = _rope_tables(lr)
    gate_bias = jnp.broadcast_to(
        jnp.pad(mlstm_gate_b[0].T, ((0, 0), (0, 4)))[:, :, None], (M_HEADS, 8, LANES)).astype(F32)
    norm_w_m = mlstm_norm_w[0].reshape(M_HEADS, 1, M_V_DIM)
    rw_t = router_w[0].T.astype(BF16)
    rb_col = jnp.broadcast_to(router_b[0][:, None], (N_EXPERTS, LANES)).astype(F32)
    tri = (jnp.arange(LANES)[:, None] < jnp.arange(LANES)[None, :]).astype(BF16)

    x_all = jnp.concatenate([x_prompt, x_sample], axis=0)
    meta_b = jnp.broadcast_to(meta_tokens[None].astype(x_all.dtype), (b, N_META, D_MODEL))
    h0 = jnp.concatenate([x_all, meta_b], axis=1).reshape(n, D_MODEL)

    proj, gates = _inproj(h0, norm1_w[0].reshape(1, D_MODEL), w_main, w_gates)
    proj3 = proj.reshape(b, l, N_MAIN)

    o_a = _attention(proj3, cos_t, sa_t, sb_t, neg_row, lam_row, sub_w, lr)

    g4 = gates.reshape(b, l, 4, M_HEADS).transpose(0, 3, 2, 1)
    off = jnp.array([NEG_BIG, 1e4, NEG_BIG, 1e4], F32)[None, None, :, None]
    fill = jnp.broadcast_to(off, (b, M_HEADS, 4, CHUNK - N_META))
    g_row = jnp.concatenate([g4[..., :lr], fill, g4[..., lr:]], axis=-1)
    g_row = jnp.pad(g_row, ((0, 0), (0, 0), (0, 4), (0, 0)))
    h_m = _mlstm(proj3, g_row, gate_bias, norm_w_m, lr)

    h1, xn2, top_e, top_w, rank, cnt = _merge(
        o_a.reshape(n, D_MODEL), h_m.reshape(n, D_MODEL), proj, h0,
        w_br_attn[0].astype(BF16), w_br_mlstm[0].astype(BF16), w_out[0].astype(BF16),
        norm2_w[0].reshape(1, D_MODEL), rw_t, rb_col, tri)

    counts = cnt[:, 0].astype(jnp.int32)
    padded = (counts + MOE_ROWS - 1) // MOE_ROWS * MOE_ROWS
    pends = jnp.cumsum(padded)
    pstarts = pends - padded
    e_ids = jnp.arange(N_EXPERTS, dtype=jnp.int32)
    pstart_of = jnp.sum(jnp.where(top_e[..., None] == e_ids, pstarts, 0), axis=-1)
    dest = pstart_of + rank
    n_blocks = (n * TOP_K + N_EXPERTS * (MOE_ROWS - 1) + MOE_ROWS - 1) // MOE_ROWS
    cap = n_blocks * MOE_ROWS
    tok = jnp.broadcast_to(jnp.arange(n, dtype=jnp.int32)[None], (TOP_K, n))
    slot_t = jnp.zeros((cap,), jnp.int32).at[dest.reshape(-1)].set(
        tok.reshape(-1), unique_indices=True, mode='promise_in_bounds')
    blk_start = jnp.arange(n_blocks, dtype=jnp.int32) * MOE_ROWS
    blk_e = jnp.minimum(jnp.sum((pends[None, :] <= blk_start[:, None]).astype(jnp.int32), axis=1),
                        N_EXPERTS - 1)
    n_used = (pends[-1:] // MOE_ROWS).astype(jnp.int32)

    xs = xn2.at[slot_t].get(mode='promise_in_bounds')
    y = _experts(blk_e, n_used, xs,
                 w_gate[0], b_gate[0].reshape(N_EXPERTS, 1, D_FF),
                 w_up[0], b_up[0].reshape(N_EXPERTS, 1, D_FF),
                 w_down[0], b_down[0].reshape(N_EXPERTS, 1, D_MODEL))
    yg = y.at[dest.reshape(-1)].get(mode='promise_in_bounds').reshape(TOP_K, b, l, D_MODEL)
    wts = top_w.T.reshape(b, l, TOP_K)
    h1_3 = h1.reshape(b, l, D_MODEL)
    fw = final_norm_w.reshape(1, D_MODEL)
    y_prompt = _combine(h1_3, yg, wts, fw, 0, bp, lr)
    y_sample = _combine(h1_3, yg, wts, fw, bp, bs, lr)
    return (y_prompt, y_sample)
```

```python
import functools
import math

import jax
import jax.numpy as jnp
import numpy as np
from jax import lax
from jax.experimental import pallas as pl
from jax.experimental.pallas import tpu as pltpu
from jax.experimental.pallas import tpu_sc as plsc

F32 = jnp.float32
BF16 = jnp.bfloat16

D_MODEL = 1024
N_META = 16
EPS = 1e-6
A_HEADS = 8
A_QK_DIM = 64
A_V_DIM = 128
ROT_DIM = 16
ROPE_THETA = 500000.0
M_HEADS = 4
M_QK_DIM = 128
M_V_DIM = 256
CHUNK = 128
N_EXPERTS = 32
TOP_K = 4
D_FF = 1024
SWIGLU_LIMIT = 7.0
SWIGLU_ALPHA = 1.702
LAM_INIT = 0.8 - 0.6 * math.exp(-0.3 * 0)

LANES = 128
NEG_BIG = -1e30
MOE_ROWS = 256
MLSTM_HPS = 2
SC_CORES = 2
SC_SUBCORES = 16
SC_WINDOW = 128
ROW_WORDS = D_MODEL // 2
COL_AQ, COL_AK, COL_AV = 0, 8, 16
COL_MQ, COL_MK = 24, 28
COL_MV, COL_MO = 16, 20
COL_GA, COL_GM = 6, 7
N_MAIN = 8192

VMEM_LIMIT = 56 * 1024 * 1024


def _pick(n, prefs):
    for p in prefs:
        if n % p == 0:
            return p
    raise ValueError(f"no tile for {n}")


def _nt(a, b):
    return lax.dot_general(a, b, (((1,), (1,)), ((), ())), preferred_element_type=F32)


def _dot(a, b):
    return jnp.dot(a, b, preferred_element_type=F32)


def _pack_rows(x):
    w = x.shape[1] // 2
    bits = lambda v: lax.bitcast_convert_type(v.astype(BF16).astype(F32), jnp.uint32)
    lo = lax.shift_right_logical(bits(x[:, :w]), jnp.uint32(16))
    hi = bits(x[:, w:]) & jnp.uint32(0xFFFF0000)
    return lax.bitcast_convert_type(lo | hi, jnp.int32)


def _unpack_rows(wds):
    u = lax.bitcast_convert_type(wds, jnp.uint32)
    lo = lax.bitcast_convert_type(lax.shift_left(u, jnp.uint32(16)), F32)
    hi = lax.bitcast_convert_type(u & jnp.uint32(0xFFFF0000), F32)
    return jnp.concatenate([lo, hi], axis=1)


def _inproj_kernel(x_ref, n1_ref, w_ref, wg_ref, o_ref, g_ref, xn_ref):
    @pl.when(pl.program_id(1) == 0)
    def _():
        x = x_ref[...]
        ms = jnp.mean(x * x, axis=-1, keepdims=True)
        xn = (x * lax.rsqrt(ms + EPS) * n1_ref[...]).astype(BF16)
        xn_ref[...] = xn
        g_ref[...] = _dot(xn, wg_ref[...])[:, :4 * M_HEADS]

    o_ref[...] = _dot(xn_ref[...], w_ref[...]).astype(BF16)


def _inproj(h0, norm1_w, w_main, w_gates):
    n = h0.shape[0]
    tm = _pick(n, (1152, 768, 512, 384, 256, 128))
    tn = 2048
    return pl.pallas_call(
        _inproj_kernel,
        grid=(n // tm, N_MAIN // tn),
        in_specs=[
            pl.BlockSpec((tm, D_MODEL), lambda i, j: (i, 0)),
            pl.BlockSpec((1, D_MODEL), lambda i, j: (0, 0)),
            pl.BlockSpec((D_MODEL, tn), lambda i, j: (0, j)),
            pl.BlockSpec((D_MODEL, LANES), lambda i, j: (0, 0)),
        ],
        out_specs=[
            pl.BlockSpec((tm, tn), lambda i, j: (i, j)),
            pl.BlockSpec((tm, 4 * M_HEADS), lambda i, j: (i, 0)),
        ],
        out_shape=[
            jax.ShapeDtypeStruct((n, N_MAIN), BF16),
            jax.ShapeDtypeStruct((n, 4 * M_HEADS), F32),
        ],
        scratch_shapes=[pltpu.VMEM((tm, D_MODEL), BF16)],
        compiler_params=pltpu.CompilerParams(
            dimension_semantics=("arbitrary", "arbitrary"), vmem_limit_bytes=VMEM_LIMIT),
        name="inproj",
    )(h0, norm1_w, w_main, w_gates)


def _attn_kernel(q_ref, k_ref, v_ref, cos_ref, sa_ref, sb_ref, neg_ref, lam_ref, sw_ref,
                 o_ref, q1_ref, q2_ref, vs_ref, kt_ref, sa1_ref, sa2_ref, sb1_ref, sb2_ref, *, lr):
    l = lr + N_META
    lp = lr + LANES
    nblk = lr // LANES
    scale = A_QK_DIM ** -0.5 * math.log2(math.e)

    def rope(t):
        return (t * cos_ref[...] + pltpu.roll(t, LANES - ROT_DIM // 2, 1) * sa_ref[...]
                + pltpu.roll(t, ROT_DIM // 2, 1) * sb_ref[...])

    lane = lax.broadcasted_iota(jnp.int32, (l, LANES), 1)
    qr = rope(q_ref[0].astype(F32)) * scale
    q1_ref[0:l, :] = jnp.where(lane < A_QK_DIM, qr, 0.0).astype(BF16)
    q2_ref[0:l, :] = jnp.where(lane >= A_QK_DIM, qr, 0.0).astype(BF16)
    vs_ref[0:l, :] = v_ref[0]
    pad = jnp.zeros((lp - l, LANES), BF16)
    q1_ref[l:lp, :] = pad
    q2_ref[l:lp, :] = pad
    vs_ref[l:lp, :] = pad
    kr = rope(k_ref[0].astype(F32))
    for c in range(nblk):
        kt_ref[:, c * LANES:(c + 1) * LANES] = kr[c * LANES:(c + 1) * LANES, :].T.astype(BF16)
    k_tail = jnp.concatenate([kr[lr:l, :], jnp.zeros((lp - l, LANES), F32)], axis=0)
    kt_ref[:, lr:lp] = k_tail.T.astype(BF16)

    lam = lam_ref[:, 0:1]

    def scores(r0, s1_ref, s2_ref):
        rows = pl.ds(r0, LANES)
        k_all = kt_ref[...]
        s1_ref[...] = _dot(q1_ref[rows, :], k_all)
        s2_ref[...] = _dot(q2_ref[rows, :], k_all)

    def softmax_parts(s_ref):
        sm = s_ref[:, :lr]
        sl = s_ref[:, lr:] + neg_ref[...]
        m = jnp.maximum(jnp.max(sm, axis=1, keepdims=True), jnp.max(sl, axis=1, keepdims=True))
        pm = jnp.exp2(sm - m)
        pl_ = jnp.exp2(sl - m)
        tot = jnp.sum(pm, axis=1, keepdims=True) + jnp.sum(pl_, axis=1, keepdims=True)
        return pm, pl_, tot

    def finish(r0, nrows, s1_ref, s2_ref):
        p1m, p1l, t1 = softmax_parts(s1_ref)
        p2m, p2l, t2 = softmax_parts(s2_ref)
        c = lam * t1 / t2
        am = (p1m - p2m * c).astype(BF16)
        al = (p1l - p2l * c).astype(BF16)
        o = (_dot(am, vs_ref[0:lr, :]) + _dot(al, vs_ref[lr:lp, :])) * (1.0 / t1)
        o = o * lax.rsqrt(jnp.mean(o * o, axis=-1, keepdims=True) + EPS) * sw_ref[...]
        o_ref[0, pl.ds(r0, nrows), :] = o[:nrows].astype(BF16)

    scores(0, sa1_ref, sa2_ref)

    def body(j, c):
        ra = pl.multiple_of(j * (2 * LANES), LANES)
        rb = ra + LANES
        scores(rb, sb1_ref, sb2_ref)
        finish(ra, LANES, sa1_ref, sa2_ref)
        scores(rb + LANES, sa1_ref, sa2_ref)
        finish(rb, LANES, sb1_ref, sb2_ref)
        return c

    lax.fori_loop(0, nblk // 2, body, 0)
    finish(lr, N_META, sa1_ref, sa2_ref)


def _attention(proj3, cos_t, sa_t, sb_t, neg_row, lam_row, sub_w, lr):
    b, l, _ = proj3.shape
    lp = lr + LANES
    const2 = lambda i, h: (0, 0)
    return pl.pallas_call(
        functools.partial(_attn_kernel, lr=lr),
        grid=(b, A_HEADS),
        in_specs=[
            pl.BlockSpec((1, l, LANES), lambda i, h: (i, 0, COL_AQ + h)),
            pl.BlockSpec((1, l, LANES), lambda i, h: (i, 0, COL_AK + h)),
            pl.BlockSpec((1, l, LANES), lambda i, h: (i, 0, COL_AV + h)),
            pl.BlockSpec((l, LANES), const2),
            pl.BlockSpec((l, LANES), const2),
            pl.BlockSpec((l, LANES), const2),
            pl.BlockSpec((1, LANES), const2),
            pl.BlockSpec((1, LANES), const2),
            pl.BlockSpec((1, LANES), const2),
        ],
        out_specs=pl.BlockSpec((1, l, LANES), lambda i, h: (i, 0, h)),
        out_shape=jax.ShapeDtypeStruct((b, l, A_HEADS * A_V_DIM), BF16),
        scratch_shapes=([pltpu.VMEM((lp, LANES), BF16)] * 3 + [pltpu.VMEM((LANES, lp), BF16)]
                        + [pltpu.VMEM((LANES, lp), F32)] * 4),
        compiler_params=pltpu.CompilerParams(
            dimension_semantics=("arbitrary", "arbitrary"), vmem_limit_bytes=VMEM_LIMIT),
        name="diff_attn",
    )(proj3, proj3, proj3, cos_t, sa_t, sb_t, neg_row, lam_row, sub_w)


def _log_sigmoid(x):
    return jnp.minimum(x, 0.0) - jnp.log(1.0 + jnp.exp(-jnp.abs(x)))


def _mlstm_kernel(q_ref, k_ref, v_ref, mo_ref, g_ref, gb_ref, nw_ref, o_ref,
                  c_ref, m_ref, kt_ref, hf_ref, hb_ref, *, lr):
    l = lr + N_META
    t = CHUNK
    nc = lr // t
    dv = M_V_DIM
    scale = M_QK_DIM ** -0.5
    row = lax.broadcasted_iota(jnp.int32, (t, t), 0)
    col = lax.broadcasted_iota(jnp.int32, (t, t), 1)
    eye = row == col
    ones_tile = jnp.where(col == 0, 1.0, 0.0).astype(BF16)
    meta_r0 = l - t

    c_ref[...] = jnp.zeros(c_ref.shape, F32)
    m_ref[...] = jnp.zeros(m_ref.shape, F32)

    for j in range(MLSTM_HPS):
        for c in range(nc + 1):
            r0 = c * t if c < nc else meta_r0
            kc = k_ref[0, r0:r0 + t, j * M_QK_DIM:(j + 1) * M_QK_DIM]
            kt_ref[j, :, c * t:(c + 1) * t] = kc.astype(F32).T.astype(BF16)

    def chunk(j, r0, g0, fwd):
        d = 0 if fwd else 1
        mask = (col <= row) if fwd else (col >= row)
        gi = 0 if fwd else 2
        q = q_ref[0, pl.ds(r0, t), j * M_QK_DIM:(j + 1) * M_QK_DIM]
        v = v_ref[0, pl.ds(r0, t), j * dv:(j + 1) * dv]
        kt = kt_ref[j, :, pl.ds(g0, t)]
        ig = g_ref[0, j, gi:gi + 1, pl.ds(g0, t)] + gb_ref[j, gi:gi + 1, :]
        lf = _log_sigmoid(g_ref[0, j, gi + 1:gi + 2, pl.ds(g0, t)] + gb_ref[j, gi + 1:gi + 2, :])
        m_prev = m_ref[j, d, 0:1, 0:1]

        b_col = jnp.sum(jnp.where(mask, jnp.broadcast_to(lf, (t, t)), 0.0), axis=1, keepdims=True)
        b_row = jnp.sum(jnp.where(eye, jnp.broadcast_to(b_col, (t, t)), 0.0), axis=0, keepdims=True)
        a_row = ig - b_row
        dm = jnp.where(mask, jnp.broadcast_to(a_row, (t, t)), NEG_BIG)
        m_col = jnp.maximum(m_prev, jnp.max(dm, axis=1, keepdims=True))
        p = jnp.exp(dm - m_col)
        w_inter = jnp.exp(m_prev - m_col)

        qs = (q.astype(F32) * scale).astype(BF16)
        s = _dot(qs, kt) * p
        v_aug = jnp.concatenate([v, ones_tile], axis=1)
        num = w_inter * _dot(qs, c_ref[j, d].astype(BF16)) + _dot(s.astype(BF16), v_aug)
        nq = num[:, dv:dv + 1]
        den = jnp.maximum(jnp.abs(nq), jnp.exp(-(b_col + m_col)))
        h = num[:, :dv] * (1.0 / den)

        m_end = jnp.maximum(m_prev, jnp.max(a_row, axis=1, keepdims=True))
        b_end = jnp.sum(lf, axis=1, keepdims=True)
        decay = jnp.exp(m_prev - m_end)
        kts = (kt.astype(F32) * jnp.exp(a_row - m_end)).astype(BF16)
        c_ref[j, d] = decay * c_ref[j, d] + _dot(kts, v_aug)
        m_ref[j, d] = jnp.broadcast_to(b_end + m_end, (8, LANES))
        return h

    for j in range(MLSTM_HPS):
        hf_ref[lr:l, j * dv:(j + 1) * dv] = chunk(j, meta_r0, lr, True)[t - N_META:, :]

    def body(i, c):
        rf = pl.multiple_of(i * t, t)
        rb = pl.multiple_of((nc - 1 - i) * t, t)
        for j in range(MLSTM_HPS):
            hf_ref[pl.ds(rf, t), j * dv:(j + 1) * dv] = chunk(j, rf, rf, True)
            hb_ref[pl.ds(rb, t), j * dv:(j + 1) * dv] = chunk(j, rb, rb, False)
        return c

    lax.fori_loop(0, nc, body, 0)
    for j in range(MLSTM_HPS):
        hb_ref[lr:l, j * dv:(j + 1) * dv] = chunk(j, meta_r0, lr, False)[t - N_META:, :]

    def finish(r0, rows):
        for j in range(MLSTM_HPS):
            cs = slice(j * dv, (j + 1) * dv)
            hs = hf_ref[pl.ds(r0, rows), cs] + hb_ref[pl.ds(r0, rows), cs]
            y = hs * lax.rsqrt(jnp.mean(hs * hs, axis=-1, keepdims=True) + EPS) * nw_ref[j]
            gate = jax.nn.sigmoid(mo_ref[0, pl.ds(r0, rows), cs].astype(F32))
            o_ref[0, pl.ds(r0, rows), cs] = (y * gate).astype(BF16)

    def fbody(i, c):
        finish(pl.multiple_of(i * t, t), t)
        return c

    lax.fori_loop(0, nc, fbody, 0)
    finish(lr, N_META)


def _mlstm(proj3, gates_row, gate_bias, norm_w, lr):
    b, l, _ = proj3.shape
    gl = gates_row.shape[-1]
    hps = MLSTM_HPS
    qk_w = hps * M_QK_DIM
    v_w = hps * M_V_DIM
    q0 = COL_MQ * LANES // qk_w
    k0 = COL_MK * LANES // qk_w
    v0 = COL_MV * M_V_DIM // v_w
    o0 = COL_MO * M_V_DIM // v_w
    return pl.pallas_call(
        functools.partial(_mlstm_kernel, lr=lr),
        grid=(b, M_HEADS // hps),
        in_specs=[
            pl.BlockSpec((1, l, qk_w), lambda i, h: (i, 0, q0 + h)),
            pl.BlockSpec((1, l, qk_w), lambda i, h: (i, 0, k0 + h)),
            pl.BlockSpec((1, l, v_w), lambda i, h: (i, 0, v0 + h)),
            pl.BlockSpec((1, l, v_w), lambda i, h: (i, 0, o0 + h)),
            pl.BlockSpec((1, hps, 8, gl), lambda i, h: (i, h, 0, 0)),
            pl.BlockSpec((hps, 8, LANES), lambda i, h: (h, 0, 0)),
            pl.BlockSpec((hps, 1, M_V_DIM), lambda i, h: (h, 0, 0)),
        ],
        out_specs=pl.BlockSpec((1, l, v_w), lambda i, h: (i, 0, h)),
        out_shape=jax.ShapeDtypeStruct((b, l, M_HEADS * M_V_DIM), BF16),
        scratch_shapes=[
            pltpu.VMEM((hps, 2, M_QK_DIM, M_V_DIM + LANES), F32),
            pltpu.VMEM((hps, 2, 8, LANES), F32),
            pltpu.VMEM((hps, M_QK_DIM, lr + CHUNK), BF16),
            pltpu.VMEM((l, v_w), F32), pltpu.VMEM((l, v_w), F32),
        ],
        compiler_params=pltpu.CompilerParams(
            dimension_semantics=("arbitrary", "arbitrary"), vmem_limit_bytes=VMEM_LIMIT),
        name="mlstm",
    )(proj3, proj3, proj3, proj3, gates_row, gate_bias, norm_w)


def _merge_kernel(oa_ref, hm_ref, ga_ref, gm_ref, h_ref, wa_ref, wm_ref, wo_ref, n2_ref,
                  rw_ref, rb_ref, tri_ref,
                  h1_ref, xn_ref, te_ref, tw_ref, rk_ref, cnt_ref, carry_ref, *, tm):
    @pl.when(pl.program_id(0) == 0)
    def _():
        carry_ref[...] = jnp.zeros(carry_ref.shape, F32)

    pa = _dot(oa_ref[...], wa_ref[...])
    pm = _dot(hm_ref[...], wm_ref[...])
    merged = (jax.nn.sigmoid(ga_ref[...].astype(F32)) * pa
              + jax.nn.sigmoid(gm_ref[...].astype(F32)) * pm)
    h1 = h_ref[...] + _dot(merged.astype(BF16), wo_ref[...])
    h1_ref[...] = h1
    xn32 = h1 * lax.rsqrt(jnp.mean(h1 * h1, axis=-1, keepdims=True) + EPS) * n2_ref[...]
    xn = xn32.astype(BF16)
    xn_ref[...] = _pack_rows(xn32)

    logits = _nt(rw_ref[...], xn) + rb_ref[:, 0:1]
    eidx = lax.broadcasted_iota(jnp.int32, (N_EXPERTS, tm), 0)
    work = logits
    vals, hots = [], []
    for _ in range(TOP_K):
        mx = jnp.max(work, axis=0, keepdims=True)
        sel = jnp.min(jnp.where(work == mx, eidx, N_EXPERTS), axis=0, keepdims=True)
        hot = eidx == sel
        vals.append(mx)
        hots.append(hot)
        work = jnp.where(hot, -jnp.inf, work)
    ex = [jnp.exp(v - vals[0]) for v in vals]
    inv = 1.0 / (ex[0] + ex[1] + ex[2] + ex[3])
    chosen = jnp.where(hots[0] | hots[1] | hots[2] | hots[3], 1.0, 0.0)

    carry = carry_ref[:, 0:1]
    ranks = []
    for s in range(tm // LANES):
        sub = chosen[:, s * LANES:(s + 1) * LANES]
        ranks.append(_dot(sub.astype(BF16), tri_ref[...]) + carry)
        carry = carry + jnp.sum(sub, axis=1, keepdims=True)
    before = jnp.concatenate(ranks, axis=1) if len(ranks) > 1 else ranks[0]
    carry_ref[...] = jnp.broadcast_to(carry, carry_ref.shape)
    cnt_ref[...] = jnp.broadcast_to(carry, cnt_ref.shape)

    for kk in range(TOP_K):
        te_ref[kk:kk + 1, :] = jnp.sum(jnp.where(hots[kk], eidx, 0), axis=0, keepdims=True)
        tw_ref[kk:kk + 1, :] = ex[kk] * inv
        rk_ref[kk:kk + 1, :] = jnp.sum(jnp.where(hots[kk], before, 0.0), axis=0,
                                       keepdims=True).astype(jnp.int32)


def _merge(o_a, h_m, proj, h0, w_a, w_m, w_o, norm2_w, rw_t, rb_col, tri):
    n = h0.shape[0]
    tm = _pick(n, (384, 256, 128))
    row = lambda i: (i, 0)
    const = lambda i: (0, 0)
    tok = lambda i: (0, i)
    wspec = pl.BlockSpec((D_MODEL, D_MODEL), const)
    return pl.pallas_call(
        functools.partial(_merge_kernel, tm=tm),
        grid=(n // tm,),
        in_specs=[
            pl.BlockSpec((tm, D_MODEL), row),
            pl.BlockSpec((tm, D_MODEL), row),
            pl.BlockSpec((tm, D_MODEL), lambda i: (i, COL_GA)),
            pl.BlockSpec((tm, D_MODEL), lambda i: (i, COL_GM)),
            pl.BlockSpec((tm, D_MODEL), row),
            wspec, wspec, wspec,
            pl.BlockSpec((1, D_MODEL), const),
            pl.BlockSpec((N_EXPERTS, D_MODEL), const),
            pl.BlockSpec((N_EXPERTS, LANES), const),
            pl.BlockSpec((LANES, LANES), const),
        ],
        out_specs=[
            pl.BlockSpec((tm, D_MODEL), row),
            pl.BlockSpec((tm, ROW_WORDS), row),
            pl.BlockSpec((TOP_K, tm), tok),
            pl.BlockSpec((TOP_K, tm), tok),
            pl.BlockSpec((TOP_K, tm), tok),
            pl.BlockSpec((N_EXPERTS, LANES), const),
        ],
        out_shape=[
            jax.ShapeDtypeStruct((n, D_MODEL), F32),
            jax.ShapeDtypeStruct((n, ROW_WORDS), jnp.int32),
            jax.ShapeDtypeStruct((TOP_K, n), jnp.int32),
            jax.ShapeDtypeStruct((TOP_K, n), F32),
            jax.ShapeDtypeStruct((TOP_K, n), jnp.int32),
            jax.ShapeDtypeStruct((N_EXPERTS, LANES), F32),
        ],
        scratch_shapes=[pltpu.VMEM((N_EXPERTS, LANES), F32)],
        compiler_params=pltpu.CompilerParams(
            dimension_semantics=("arbitrary",), vmem_limit_bytes=VMEM_LIMIT),
        name="merge_router",
    )(o_a, h_m, proj, proj, h0, w_a, w_m, w_o, norm2_w, rw_t, rb_col, tri)


def _expert_kernel(be_ref, nu_ref, x_ref, wg_ref, bg_ref, wu_ref, bu_ref, wd_ref, bd_ref, y_ref,
                   wg_s, wu_s, wd_s):
    i = pl.program_id(0)
    used = i < nu_ref[0]

    @pl.when(jnp.logical_or(i == 0, be_ref[i] != be_ref[jnp.maximum(i - 1, 0)]))
    def _():
        wg_s[...] = wg_ref[0].astype(BF16)
        wu_s[...] = wu_ref[0].astype(BF16)
        wd_s[...] = wd_ref[0].astype(BF16)

    @pl.when(used)
    def _():
        x = _unpack_rows(x_ref[...]).astype(BF16)
        gt = jnp.minimum(_dot(x, wg_s[...]) + bg_ref[0], SWIGLU_LIMIT)
        up = jnp.clip(_dot(x, wu_s[...]) + bu_ref[0], -SWIGLU_LIMIT, SWIGLU_LIMIT)
        glu = gt * jax.nn.sigmoid(SWIGLU_ALPHA * gt)
        act = ((up + 1.0) * glu).astype(BF16)
        y_ref[...] = _pack_rows(_dot(act, wd_s[...]) + bd_ref[0])

    @pl.when(jnp.logical_not(used))
    def _():
        y_ref[...] = jnp.zeros(y_ref.shape, jnp.int32)


def _experts(blk_e, n_used, xs, w_g, b_g, w_u, b_u, w_d, b_d):
    cap = xs.shape[0]
    n_blocks = cap // MOE_ROWS
    wspec = lambda d0, d1: pl.BlockSpec((1, d0, d1), lambda i, be, nu: (be[i], 0, 0))
    return pl.pallas_call(
        _expert_kernel,
        grid_spec=pltpu.PrefetchScalarGridSpec(
            num_scalar_prefetch=2,
            grid=(n_blocks,),
            in_specs=[
                pl.BlockSpec((MOE_ROWS, ROW_WORDS), lambda i, be, nu: (i, 0)),
                wspec(D_MODEL, D_FF), wspec(1, D_FF),
                wspec(D_MODEL, D_FF), wspec(1, D_FF),
                wspec(D_FF, D_MODEL), wspec(1, D_MODEL),
            ],
            out_specs=pl.BlockSpec((MOE_ROWS, ROW_WORDS), lambda i, be, nu: (i, 0)),
            scratch_shapes=[pltpu.VMEM((D_MODEL, D_FF), BF16), pltpu.VMEM((D_MODEL, D_FF), BF16),
                            pltpu.VMEM((D_FF, D_MODEL), BF16)],
        ),
        out_shape=jax.ShapeDtypeStruct((cap, ROW_WORDS), jnp.int32),
        compiler_params=pltpu.CompilerParams(
            dimension_semantics=("arbitrary",), vmem_limit_bytes=VMEM_LIMIT),
        name="experts",
    )(blk_e, n_used, xs, w_g, b_g, w_u, b_u, w_d, b_d)


def _sc_worker_windows(n_win):
    return -(-n_win // (SC_CORES * SC_SUBCORES))


def _sc_dispatch(x_words, dest3, cap):
    n = x_words.shape[0]
    n_win = n // SC_WINDOW
    per = _sc_worker_windows(n_win)

    def body(x_hbm, d_hbm, o_hbm, idx_v, rows_v):
        wid = lax.axis_index("s") * SC_CORES + lax.axis_index("c")

        @pl.loop(0, per)
        def _(i):
            win = jnp.minimum(wid * per + i, n_win - 1)
            pltpu.sync_copy(d_hbm.at[win], idx_v)
            pltpu.sync_copy(x_hbm.at[pl.ds(win * SC_WINDOW, SC_WINDOW)], rows_v)
            for kk in range(TOP_K):
                pltpu.sync_copy(rows_v, o_hbm.at[idx_v.at[kk]])

    return pl.kernel(
        body,
        out_type=jax.ShapeDtypeStruct((cap, ROW_WORDS), jnp.int32),
        mesh=plsc.VectorSubcoreMesh(core_axis_name="c", subcore_axis_name="s"),
        scratch_types=[pltpu.VMEM((TOP_K, SC_WINDOW), jnp.int32),
                       pltpu.VMEM((SC_WINDOW, ROW_WORDS), jnp.int32)],
        name="moe_dispatch",
    )(x_words, dest3)


def _sc_gather(y_words, idx2):
    n_win = idx2.shape[0]
    per = _sc_worker_windows(n_win)

    def body(y_hbm, i_hbm, o_hbm, idx_v, rows_v, sem):
        wid = lax.axis_index("s") * SC_CORES + lax.axis_index("c")

        @pl.loop(0, per)
        def _(i):
            win = jnp.minimum(wid * per + i, n_win - 1)
            pltpu.sync_copy(i_hbm.at[win], idx_v)
            pltpu.async_copy(y_hbm.at[idx_v], rows_v, sem).wait()
            pltpu.sync_copy(rows_v, o_hbm.at[pl.ds(win * SC_WINDOW, SC_WINDOW)])

    return pl.kernel(
        body,
        out_type=jax.ShapeDtypeStruct((n_win * SC_WINDOW, ROW_WORDS), jnp.int32),
        mesh=plsc.VectorSubcoreMesh(core_axis_name="c", subcore_axis_name="s"),
        scratch_types=[pltpu.VMEM((SC_WINDOW,), jnp.int32),
                       pltpu.VMEM((SC_WINDOW, ROW_WORDS), jnp.int32),
                       pltpu.SemaphoreType.DMA],
        name="moe_gather",
    )(y_words, idx2)


def _combine_kernel(h_ref, y_ref, w_ref, fw_ref, o_ref):
    acc = h_ref[0]
    for kk in range(TOP_K):
        acc = acc + w_ref[0, :, kk:kk + 1] * _unpack_rows(y_ref[kk, 0])
    o_ref[0] = acc * lax.rsqrt(jnp.mean(acc * acc, axis=-1, keepdims=True) + EPS) * fw_ref[...]


def _combine(h1, yg, wts, final_w, b0, nb, lr):
    tr = _pick(lr, (512, 256, 128))
    return pl.pallas_call(
        _combine_kernel,
        grid=(nb, lr // tr),
        in_specs=[
            pl.BlockSpec((1, tr, D_MODEL), lambda i, j: (b0 + i, j, 0)),
            pl.BlockSpec((TOP_K, 1, tr, ROW_WORDS), lambda i, j: (0, b0 + i, j, 0)),
            pl.BlockSpec((1, tr, TOP_K), lambda i, j: (b0 + i, j, 0)),
            pl.BlockSpec((1, D_MODEL), lambda i, j: (0, 0)),
        ],
        out_specs=pl.BlockSpec((1, tr, D_MODEL), lambda i, j: (i, j, 0)),
        out_shape=jax.ShapeDtypeStruct((nb, lr, D_MODEL), F32),
        compiler_params=pltpu.CompilerParams(
            dimension_semantics=("arbitrary", "arbitrary"), vmem_limit_bytes=VMEM_LIMIT),
        name="combine",
    )(h1, yg, wts, final_w)


def _rope_tables(lr):
    l = lr + N_META
    half = ROT_DIM // 2
    pos = jnp.concatenate([jnp.arange(N_META, l, dtype=F32), jnp.arange(N_META, dtype=F32)])
    inv_freq = ROPE_THETA ** (-jnp.arange(0, ROT_DIM, 2, dtype=F32) / ROT_DIM)
    ang = pos[:, None] * inv_freq[None, :]
    cos, sin = jnp.cos(ang), jnp.sin(ang)
    z = jnp.zeros((l, A_QK_DIM - ROT_DIM), F32)
    zh = jnp.zeros((l, half), F32)
    cos64 = jnp.concatenate([cos, cos, z + 1.0], axis=1)
    sa64 = jnp.concatenate([-sin, zh, z], axis=1)
    sb64 = jnp.concatenate([zh, sin, z], axis=1)
    dup = lambda a: jnp.concatenate([a, a], axis=1)
    return dup(cos64), dup(sa64), dup(sb64)


def kernel(x_prompt, x_sample, meta_tokens, norm1_w, w_in, lambda_q1, lambda_k1, lambda_q2, lambda_k2, attn_subln_w, mlstm_gate_b, mlstm_norm_w, w_br_attn, w_br_mlstm, w_out, norm2_w, router_w, router_b, w_gate, b_gate, w_up, b_up, w_down, b_down, final_norm_w):
    bp, lr, _ = x_prompt.shape
    bs = x_sample.shape[0]
    b = bp + bs
    l = lr + N_META
    n = b * l

    w_in0 = w_in[0]
    g0 = 3 * 1024 + 2 * 512 + 2 * 1024
    w_main = jnp.concatenate([w_in0[:, :g0], w_in0[:, g0 + 4 * M_HEADS:]], axis=1).astype(BF16)
    w_gates = jnp.pad(w_in0[:, g0:g0 + 4 * M_HEADS], ((0, 0), (0, LANES - 4 * M_HEADS))).astype(BF16)
    lam = (jnp.exp(jnp.sum(lambda_q1[0] * lambda_k1[0])) - jnp.exp(jnp.sum(lambda_q2[0] * lambda_k2[0]))
           + LAM_INIT)
    lam_row = jnp.full((1, LANES), lam, F32)
    sub_w = (attn_subln_w[0] * (1.0 - LAM_INIT)).reshape(1, A_V_DIM)
    neg_row = jnp.where(jnp.arange(LANES) < N_META, 0.0, NEG_BIG).astype(F32).reshape(1, LANES)
    cos_t, sa_t, sb_t = _rope_tables(lr)
    gate_bias = jnp.broadcast_to(
        jnp.pad(mlstm_gate_b[0].T, ((0, 0), (0, 4)))[:, :, None], (M_HEADS, 8, LANES)).astype(F32)
    norm_w_m = mlstm_norm_w[0].reshape(M_HEADS, 1, M_V_DIM)
    rw_t = router_w[0].T.astype(BF16)
    rb_col = jnp.broadcast_to(router_b[0][:, None], (N_EXPERTS, LANES)).astype(F32)
    tri = (jnp.arange(LANES)[:, None] < jnp.arange(LANES)[None, :]).astype(BF16)

    x_all = jnp.concatenate([x_prompt, x_sample], axis=0)
    meta_b = jnp.broadcast_to(meta_tokens[None].astype(x_all.dtype), (b, N_META, D_MODEL))
    h0 = jnp.concatenate([x_all, meta_b], axis=1).reshape(n, D_MODEL)

    proj, gates = _inproj(h0, norm1_w[0].reshape(1, D_MODEL), w_main, w_gates)
    proj3 = proj.reshape(b, l, N_MAIN)

    o_a = _attention(proj3, cos_t, sa_t, sb_t, neg_row, lam_row, sub_w, lr)

    g4 = gates.reshape(b, l, 4, M_HEADS).transpose(0, 3, 2, 1)
    off = jnp.array([NEG_BIG, 1e4, NEG_BIG, 1e4], F32)[None, None, :, None]
    fill = jnp.broadcast_to(off, (b, M_HEADS, 4, CHUNK - N_META))
    g_row = jnp.concatenate([g4[..., :lr], fill, g4[..., lr:]], axis=-1)
    g_row = jnp.pad(g_row, ((0, 0), (0, 0), (0, 4), (0, 0)))
    h_m = _mlstm(proj3, g_row, gate_bias, norm_w_m, lr)

    h1, xn2, top_e, top_w, rank, cnt = _merge(
        o_a.reshape(n, D_MODEL), h_m.reshape(n, D_MODEL), proj, h0,
        w_br_attn[0].astype(BF16), w_br_mlstm[0].astype(BF16), w_out[0].astype(BF16),
        norm2_w[0].reshape(1, D_MODEL), rw_t, rb_col, tri)

    counts = cnt[:, 0].astype(jnp.int32)
    padded = (counts + MOE_ROWS - 1) // MOE_ROWS * MOE_ROWS
    pends = jnp.cumsum(padded)
    pstarts = pends - padded
    e_ids = jnp.arange(N_EXPERTS, dtype=jnp.int32)
    pstart_of = jnp.sum(jnp.where(top_e[..., None] == e_ids, pstarts, 0), axis=-1)
    dest = pstart_of + rank
    n_blocks = (n * TOP_K + N_EXPERTS * (MOE_ROWS - 1) + MOE_ROWS - 1) // MOE_ROWS
    cap = n_blocks * MOE_ROWS
    blk_start = jnp.arange(n_blocks, dtype=jnp.int32) * MOE_ROWS
    blk_e = jnp.minimum(jnp.sum((pends[None, :] <= blk_start[:, None]).astype(jnp.int32), axis=1),
                        N_EXPERTS - 1)
    n_used = (pends[-1:] // MOE_ROWS).astype(jnp.int32)

    dest_win = dest.reshape(TOP_K, n // SC_WINDOW, SC_WINDOW).transpose(1, 0, 2)
    xs = _sc_dispatch(xn2, dest_win, cap)
    y = _experts(blk_e, n_used, xs,
                 w_gate[0], b_gate[0].reshape(N_EXPERTS, 1, D_FF),
                 w_up[0], b_up[0].reshape(N_EXPERTS, 1, D_FF),
                 w_down[0], b_down[0].reshape(N_EXPERTS, 1, D_MODEL))
    yg = _sc_gather(y, dest.reshape(TOP_K * n // SC_WINDOW, SC_WINDOW)).reshape(TOP_K, b, l, ROW_WORDS)
    wts = top_w.T.reshape(b, l, TOP_K)
    h1_3 = h1.reshape(b, l, D_MODEL)
    fw = final_norm_w.reshape(1, D_MODEL)
    y_prompt = _combine(h1_3, yg, wts, fw, 0, bp, lr)
    y_sample = _combine(h1_3, yg, wts, fw, bp, bs, lr)
    return (y_prompt, y_sample)
```

```python
import functools
import math

import jax
import jax.numpy as jnp
import numpy as np
from jax import lax
from jax.experimental import pallas as pl
from jax.experimental.pallas import tpu as pltpu
from jax.experimental.pallas import tpu_sc as plsc

F32 = jnp.float32
BF16 = jnp.bfloat16

D_MODEL = 1024
N_META = 16
EPS = 1e-6
A_HEADS = 8
A_QK_DIM = 64
A_V_DIM = 128
ROT_DIM = 16
ROPE_THETA = 500000.0
M_HEADS = 4
M_QK_DIM = 128
M_V_DIM = 256
CHUNK = 128
N_EXPERTS = 32
TOP_K = 4
D_FF = 1024
SWIGLU_LIMIT = 7.0
SWIGLU_ALPHA = 1.702
LAM_INIT = 0.8 - 0.6 * math.exp(-0.3 * 0)

LANES = 128
NEG_BIG = -1e30
MOE_ROWS = 256
SC_CORES = 2
SC_SUBCORES = 16
SC_WINDOW = 128
ROW_WORDS = D_MODEL // 2
COL_AQ, COL_AK, COL_AV = 0, 8, 16
COL_MQ, COL_MK = 24, 28
COL_MV, COL_MO = 16, 20
COL_GA, COL_GM = 6, 7
N_MAIN = 8192

VMEM_LIMIT = 56 * 1024 * 1024


def _pick(n, prefs):
    for p in prefs:
        if n % p == 0:
            return p
    raise ValueError(f"no tile for {n}")


def _nt(a, b):
    return lax.dot_general(a, b, (((1,), (1,)), ((), ())), preferred_element_type=F32)


def _dot(a, b):
    return jnp.dot(a, b, preferred_element_type=F32)


def _pack_rows(x):
    w = x.shape[1] // 2
    bits = lambda v: lax.bitcast_convert_type(v.astype(BF16).astype(F32), jnp.uint32)
    lo = lax.shift_right_logical(bits(x[:, :w]), jnp.uint32(16))
    hi = bits(x[:, w:]) & jnp.uint32(0xFFFF0000)
    return lax.bitcast_convert_type(lo | hi, jnp.int32)


def _unpack_rows(wds):
    u = lax.bitcast_convert_type(wds, jnp.uint32)
    lo = lax.bitcast_convert_type(lax.shift_left(u, jnp.uint32(16)), F32)
    hi = lax.bitcast_convert_type(u & jnp.uint32(0xFFFF0000), F32)
    return jnp.concatenate([lo, hi], axis=1)


def _inproj_kernel(x_ref, n1_ref, w_ref, wg_ref, o_ref, g_ref, xn_ref):
    @pl.when(pl.program_id(1) == 0)
    def _():
        x = x_ref[...]
        ms = jnp.mean(x * x, axis=-1, keepdims=True)
        xn = (x * lax.rsqrt(ms + EPS) * n1_ref[...]).astype(BF16)
        xn_ref[...] = xn
        g_ref[...] = _dot(xn, wg_ref[...])[:, :4 * M_HEADS]

    o_ref[...] = _dot(xn_ref[...], w_ref[...]).astype(BF16)


def _inproj(h0, norm1_w, w_main, w_gates):
    n = h0.shape[0]
    tm = _pick(n, (1152, 768, 512, 384, 256, 128))
    tn = 2048
    return pl.pallas_call(
        _inproj_kernel,
        grid=(n // tm, N_MAIN // tn),
        in_specs=[
            pl.BlockSpec((tm, D_MODEL), lambda i, j: (i, 0)),
            pl.BlockSpec((1, D_MODEL), lambda i, j: (0, 0)),
            pl.BlockSpec((D_MODEL, tn), lambda i, j: (0, j)),
            pl.BlockSpec((D_MODEL, LANES), lambda i, j: (0, 0)),
        ],
        out_specs=[
            pl.BlockSpec((tm, tn), lambda i, j: (i, j)),
            pl.BlockSpec((tm, 4 * M_HEADS), lambda i, j: (i, 0)),
        ],
        out_shape=[
            jax.ShapeDtypeStruct((n, N_MAIN), BF16),
            jax.ShapeDtypeStruct((n, 4 * M_HEADS), F32),
        ],
        scratch_shapes=[pltpu.VMEM((tm, D_MODEL), BF16)],
        compiler_params=pltpu.CompilerParams(
            dimension_semantics=("arbitrary", "arbitrary"), vmem_limit_bytes=VMEM_LIMIT),
        name="inproj",
    )(h0, norm1_w, w_main, w_gates)


def _attn_kernel(q_ref, k_ref, v_ref, cos_ref, sa_ref, sb_ref, neg_ref, lam_ref, sw_ref,
                 o_ref, q1_ref, q2_ref, vs_ref, kt_ref, sa1_ref, sa2_ref, sb1_ref, sb2_ref, *, lr):
    l = lr + N_META
    lp = lr + LANES
    nblk = lr // LANES
    scale = A_QK_DIM ** -0.5 * math.log2(math.e)

    def rope(t):
        return (t * cos_ref[...] + pltpu.roll(t, LANES - ROT_DIM // 2, 1) * sa_ref[...]
                + pltpu.roll(t, ROT_DIM // 2, 1) * sb_ref[...])

    lane = lax.broadcasted_iota(jnp.int32, (l, LANES), 1)
    qr = rope(q_ref[0].astype(F32)) * scale
    q1_ref[0:l, :] = jnp.where(lane < A_QK_DIM, qr, 0.0).astype(BF16)
    q2_ref[0:l, :] = jnp.where(lane >= A_QK_DIM, qr, 0.0).astype(BF16)
    vs_ref[0:l, :] = v_ref[0]
    pad = jnp.zeros((lp - l, LANES), BF16)
    q1_ref[l:lp, :] = pad
    q2_ref[l:lp, :] = pad
    vs_ref[l:lp, :] = pad
    kr = rope(k_ref[0].astype(F32))
    for c in range(nblk):
        kt_ref[:, c * LANES:(c + 1) * LANES] = kr[c * LANES:(c + 1) * LANES, :].T.astype(BF16)
    k_tail = jnp.concatenate([kr[lr:l, :], jnp.zeros((lp - l, LANES), F32)], axis=0)
    kt_ref[:, lr:lp] = k_tail.T.astype(BF16)

    lam = lam_ref[:, 0:1]

    def scores(r0, s1_ref, s2_ref):
        rows = pl.ds(r0, LANES)
        k_all = kt_ref[...]
        s1_ref[...] = _dot(q1_ref[rows, :], k_all)
        s2_ref[...] = _dot(q2_ref[rows, :], k_all)

    def softmax_parts(s_ref):
        sm = s_ref[:, :lr]
        sl = s_ref[:, lr:] + neg_ref[...]
        m = jnp.maximum(jnp.max(sm, axis=1, keepdims=True), jnp.max(sl, axis=1, keepdims=True))
        pm = jnp.exp2(sm - m)
        pl_ = jnp.exp2(sl - m)
        tot = jnp.sum(pm, axis=1, keepdims=True) + jnp.sum(pl_, axis=1, keepdims=True)
        return pm, pl_, tot

    def finish(r0, nrows, s1_ref, s2_ref):
        p1m, p1l, t1 = softmax_parts(s1_ref)
        p2m, p2l, t2 = softmax_parts(s2_ref)
        c = lam * t1 / t2
        am = (p1m - p2m * c).astype(BF16)
        al = (p1l - p2l * c).astype(BF16)
        o = (_dot(am, vs_ref[0:lr, :]) + _dot(al, vs_ref[lr:lp, :])) * (1.0 / t1)
        o = o * lax.rsqrt(jnp.mean(o * o, axis=-1, keepdims=True) + EPS) * sw_ref[...]
        o_ref[0, pl.ds(r0, nrows), :] = o[:nrows].astype(BF16)

    scores(0, sa1_ref, sa2_ref)

    def body(j, c):
        ra = pl.multiple_of(j * (2 * LANES), LANES)
        rb = ra + LANES
        scores(rb, sb1_ref, sb2_ref)
        finish(ra, LANES, sa1_ref, sa2_ref)
        scores(rb + LANES, sa1_ref, sa2_ref)
        finish(rb, LANES, sb1_ref, sb2_ref)
        return c

    lax.fori_loop(0, nblk // 2, body, 0)
    finish(lr, N_META, sa1_ref, sa2_ref)


def _attention(proj3, cos_t, sa_t, sb_t, neg_row, lam_row, sub_w, lr):
    b, l, _ = proj3.shape
    lp = lr + LANES
    const2 = lambda i, h: (0, 0)
    return pl.pallas_call(
        functools.partial(_attn_kernel, lr=lr),
        grid=(b, A_HEADS),
        in_specs=[
            pl.BlockSpec((1, l, LANES), lambda i, h: (i, 0, COL_AQ + h)),
            pl.BlockSpec((1, l, LANES), lambda i, h: (i, 0, COL_AK + h)),
            pl.BlockSpec((1, l, LANES), lambda i, h: (i, 0, COL_AV + h)),
            pl.BlockSpec((l, LANES), const2),
            pl.BlockSpec((l, LANES), const2),
            pl.BlockSpec((l, LANES), const2),
            pl.BlockSpec((1, LANES), const2),
            pl.BlockSpec((1, LANES), const2),
            pl.BlockSpec((1, LANES), const2),
        ],
        out_specs=pl.BlockSpec((1, l, LANES), lambda i, h: (i, 0, h)),
        out_shape=jax.ShapeDtypeStruct((b, l, A_HEADS * A_V_DIM), BF16),
        scratch_shapes=([pltpu.VMEM((lp, LANES), BF16)] * 3 + [pltpu.VMEM((LANES, lp), BF16)]
                        + [pltpu.VMEM((LANES, lp), F32)] * 4),
        compiler_params=pltpu.CompilerParams(
            dimension_semantics=("arbitrary", "arbitrary"), vmem_limit_bytes=VMEM_LIMIT),
        name="diff_attn",
    )(proj3, proj3, proj3, cos_t, sa_t, sb_t, neg_row, lam_row, sub_w)


def _log_sigmoid(x):
    return jnp.minimum(x, 0.0) - jnp.log(1.0 + jnp.exp(-jnp.abs(x)))


def _mlstm_kernel(q_ref, k_ref, v_ref, mo_ref, g_ref, gb_ref, nw_ref, o_ref,
                  c_ref, m_ref, kt_ref, h0_ref, u_ref, bcol_ref, amax_ref, aend_ref, bend_ref,
                  hf_ref, hb_ref, *, lr):
    l = lr + N_META
    t = CHUNK
    nc = lr // t
    dv = M_V_DIM
    scale = M_QK_DIM ** -0.5
    row = lax.broadcasted_iota(jnp.int32, (t, t), 0)
    col = lax.broadcasted_iota(jnp.int32, (t, t), 1)
    eye = row == col
    ones_tile = jnp.where(col == 0, 1.0, 0.0).astype(BF16)
    meta_r0 = l - t

    c_ref[...] = jnp.zeros(c_ref.shape, F32)
    m_ref[...] = jnp.zeros(m_ref.shape, F32)

    for c in range(nc + 1):
        r0 = c * t if c < nc else meta_r0
        kt_ref[:, c * t:(c + 1) * t] = k_ref[0, r0:r0 + t, :].astype(F32).T.astype(BF16)

    def scaled_q(r0):
        return (q_ref[0, pl.ds(r0, t), :].astype(F32) * scale).astype(BF16)

    def intra(ci, r0, g0):
        v_aug = jnp.concatenate([v_ref[0, pl.ds(r0, t), :], ones_tile], axis=1)
        kt = kt_ref[:, pl.ds(g0, t)]
        s_raw = _dot(scaled_q(r0), kt)
        for d, mask in ((0, col <= row), (1, col >= row)):
            gi = 2 * d
            ig = g_ref[0, 0, gi:gi + 1, pl.ds(g0, t)] + gb_ref[0, gi:gi + 1, :]
            lf = _log_sigmoid(g_ref[0, 0, gi + 1:gi + 2, pl.ds(g0, t)] + gb_ref[0, gi + 1:gi + 2, :])
            b_col = jnp.sum(jnp.where(mask, jnp.broadcast_to(lf, (t, t)), 0.0), axis=1, keepdims=True)
            b_row = jnp.sum(jnp.where(eye, jnp.broadcast_to(b_col, (t, t)), 0.0), axis=0, keepdims=True)
            a_row = ig - b_row
            dm = jnp.where(mask, jnp.broadcast_to(a_row, (t, t)), NEG_BIG)
            amax = jnp.max(dm, axis=1, keepdims=True)
            s0 = s_raw * jnp.exp(dm - amax)
            h0_ref[d, pl.ds(g0, t), :] = _dot(s0.astype(BF16), v_aug)
            a_end = jnp.max(a_row, axis=1, keepdims=True)
            kts = (kt.astype(F32) * jnp.exp(a_row - a_end)).astype(BF16)
            u_ref[d, ci] = _dot(kts, v_aug)
            bcol_ref[d, pl.ds(g0, t), :] = b_col
            amax_ref[d, pl.ds(g0, t), :] = amax
            aend_ref[d, ci] = jnp.broadcast_to(a_end, (8, LANES))
            bend_ref[d, ci] = jnp.broadcast_to(jnp.sum(lf, axis=1, keepdims=True), (8, LANES))

    ua = _pick(nc, (4, 2, 1))

    def abody(i, c):
        for u in range(ua):
            ci = ua * i + u
            r0 = pl.multiple_of(ci * t, t)
            intra(ci, r0, r0)
        return c

    lax.fori_loop(0, nc // ua, abody, 0)
    intra(nc, meta_r0, lr)

    def step(d, ci, r0, g0):
        m_prev = m_ref[d, 0:1, 0:1]
        amax = amax_ref[d, pl.ds(g0, t), :]
        m_col = jnp.maximum(m_prev, amax)
        num = (jnp.exp(m_prev - m_col) * _dot(scaled_q(r0), c_ref[d].astype(BF16))
               + jnp.exp(amax - m_col) * h0_ref[d, pl.ds(g0, t), :])
        den = jnp.maximum(jnp.abs(num[:, dv:dv + 1]), jnp.exp(-(bcol_ref[d, pl.ds(g0, t), :] + m_col)))
        a_end = aend_ref[d, ci, 0:1, 0:1]
        m_end = jnp.maximum(m_prev, a_end)
        c_ref[d] = jnp.exp(m_prev - m_end) * c_ref[d] + jnp.exp(a_end - m_end) * u_ref[d, ci]
        m_ref[d] = jnp.broadcast_to(bend_ref[d, ci, 0:1, 0:1] + m_end, (8, LANES))
        return num[:, :dv] * (1.0 / den)

    hf_ref[lr:l, :] = step(0, nc, meta_r0, lr)[t - N_META:, :]

    ub = _pick(nc, (2, 1))

    def bbody(i, c):
        for u in range(ub):
            jf = ub * i + u
            rf = pl.multiple_of(jf * t, t)
            hf_ref[pl.ds(rf, t), :] = step(0, jf, rf, rf)
            jb = nc - 1 - jf
            rb = pl.multiple_of(jb * t, t)
            hb_ref[pl.ds(rb, t), :] = step(1, jb, rb, rb)
        return c

    lax.fori_loop(0, nc // ub, bbody, 0)
    hb_ref[lr:l, :] = step(1, nc, meta_r0, lr)[t - N_META:, :]

    def finish(r0, rows):
        hs = hf_ref[pl.ds(r0, rows), :] + hb_ref[pl.ds(r0, rows), :]
        y = hs * lax.rsqrt(jnp.mean(hs * hs, axis=-1, keepdims=True) + EPS) * nw_ref[0]
        gate = jax.nn.sigmoid(mo_ref[0, pl.ds(r0, rows), :].astype(F32))
        o_ref[0, pl.ds(r0, rows), :] = (y * gate).astype(BF16)

    def fbody(i, c):
        for u in range(ua):
            finish(pl.multiple_of((ua * i + u) * t, t), t)
        return c

    lax.fori_loop(0, nc // ua, fbody, 0)
    finish(lr, N_META)


def _mlstm(proj3, gates_row, gate_bias, norm_w, lr):
    b, l, _ = proj3.shape
    gl = gates_row.shape[-1]
    nw = lr // CHUNK + 1
    aug = M_V_DIM + LANES
    return pl.pallas_call(
        functools.partial(_mlstm_kernel, lr=lr),
        grid=(b, M_HEADS),
        in_specs=[
            pl.BlockSpec((1, l, M_QK_DIM), lambda i, h: (i, 0, COL_MQ + h)),
            pl.BlockSpec((1, l, M_QK_DIM), lambda i, h: (i, 0, COL_MK + h)),
            pl.BlockSpec((1, l, M_V_DIM), lambda i, h: (i, 0, COL_MV + h)),
            pl.BlockSpec((1, l, M_V_DIM), lambda i, h: (i, 0, COL_MO + h)),
            pl.BlockSpec((1, 1, 8, gl), lambda i, h: (i, h, 0, 0)),
            pl.BlockSpec((1, 8, LANES), lambda i, h: (h, 0, 0)),
            pl.BlockSpec((1, 1, M_V_DIM), lambda i, h: (h, 0, 0)),
        ],
        out_specs=pl.BlockSpec((1, l, M_V_DIM), lambda i, h: (i, 0, h)),
        out_shape=jax.ShapeDtypeStruct((b, l, M_HEADS * M_V_DIM), BF16),
        scratch_shapes=[
            pltpu.VMEM((2, M_QK_DIM, aug), F32),
            pltpu.VMEM((2, 8, LANES), F32),
            pltpu.VMEM((M_QK_DIM, gl), BF16),
            pltpu.VMEM((2, gl, aug), F32),
            pltpu.VMEM((2, nw, M_QK_DIM, aug), F32),
            pltpu.VMEM((2, gl, 1), F32),
            pltpu.VMEM((2, gl, 1), F32),
            pltpu.VMEM((2, nw, 8, LANES), F32),
            pltpu.VMEM((2, nw, 8, LANES), F32),
            pltpu.VMEM((l, M_V_DIM), F32), pltpu.VMEM((l, M_V_DIM), F32),
        ],
        compiler_params=pltpu.CompilerParams(
            dimension_semantics=("arbitrary", "arbitrary"), vmem_limit_bytes=VMEM_LIMIT),
        name="mlstm",
    )(proj3, proj3, proj3, proj3, gates_row, gate_bias, norm_w)


def _merge_kernel(oa_ref, hm_ref, ga_ref, gm_ref, h_ref, wa_ref, wm_ref, wo_ref, n2_ref,
                  rw_ref, rb_ref, tri_ref,
                  h1_ref, xn_ref, te_ref, tw_ref, rk_ref, cnt_ref, carry_ref, *, tm):
    @pl.when(pl.program_id(0) == 0)
    def _():
        carry_ref[...] = jnp.zeros(carry_ref.shape, F32)

    pa = _dot(oa_ref[...], wa_ref[...])
    pm = _dot(hm_ref[...], wm_ref[...])
    merged = (jax.nn.sigmoid(ga_ref[...].astype(F32)) * pa
              + jax.nn.sigmoid(gm_ref[...].astype(F32)) * pm)
    h1 = h_ref[...] + _dot(merged.astype(BF16), wo_ref[...])
    h1_ref[...] = h1
    xn32 = h1 * lax.rsqrt(jnp.mean(h1 * h1, axis=-1, keepdims=True) + EPS) * n2_ref[...]
    xn = xn32.astype(BF16)
    xn_ref[...] = _pack_rows(xn32)

    logits = _nt(rw_ref[...], xn) + rb_ref[:, 0:1]
    eidx = lax.broadcasted_iota(jnp.int32, (N_EXPERTS, tm), 0)
    work = logits
    vals, hots = [], []
    for _ in range(TOP_K):
        mx = jnp.max(work, axis=0, keepdims=True)
        sel = jnp.min(jnp.where(work == mx, eidx, N_EXPERTS), axis=0, keepdims=True)
        hot = eidx == sel
        vals.append(mx)
        hots.append(hot)
        work = jnp.where(hot, -jnp.inf, work)
    ex = [jnp.exp(v - vals[0]) for v in vals]
    inv = 1.0 / (ex[0] + ex[1] + ex[2] + ex[3])
    chosen = jnp.where(hots[0] | hots[1] | hots[2] | hots[3], 1.0, 0.0)

    carry = carry_ref[:, 0:1]
    ranks = []
    for s in range(tm // LANES):
        sub = chosen[:, s * LANES:(s + 1) * LANES]
        ranks.append(_dot(sub.astype(BF16), tri_ref[...]) + carry)
        carry = carry + jnp.sum(sub, axis=1, keepdims=True)
    before = jnp.concatenate(ranks, axis=1) if len(ranks) > 1 else ranks[0]
    carry_ref[...] = jnp.broadcast_to(carry, carry_ref.shape)
    cnt_ref[...] = jnp.broadcast_to(carry, cnt_ref.shape)

    for kk in range(TOP_K):
        te_ref[kk:kk + 1, :] = jnp.sum(jnp.where(hots[kk], eidx, 0), axis=0, keepdims=True)
        tw_ref[kk:kk + 1, :] = ex[kk] * inv
        rk_ref[kk:kk + 1, :] = jnp.sum(jnp.where(hots[kk], before, 0.0), axis=0,
                                       keepdims=True).astype(jnp.int32)


def _merge(o_a, h_m, proj, h0, w_a, w_m, w_o, norm2_w, rw_t, rb_col, tri):
    n = h0.shape[0]
    tm = _pick(n, (384, 256, 128))
    row = lambda i: (i, 0)
    const = lambda i: (0, 0)
    tok = lambda i: (0, i)
    wspec = pl.BlockSpec((D_MODEL, D_MODEL), const)
    return pl.pallas_call(
        functools.partial(_merge_kernel, tm=tm),
        grid=(n // tm,),
        in_specs=[
            pl.BlockSpec((tm, D_MODEL), row),
            pl.BlockSpec((tm, D_MODEL), row),
            pl.BlockSpec((tm, D_MODEL), lambda i: (i, COL_GA)),
            pl.BlockSpec((tm, D_MODEL), lambda i: (i, COL_GM)),
            pl.BlockSpec((tm, D_MODEL), row),
            wspec, wspec, wspec,
            pl.BlockSpec((1, D_MODEL), const),
            pl.BlockSpec((N_EXPERTS, D_MODEL), const),
            pl.BlockSpec((N_EXPERTS, LANES), const),
            pl.BlockSpec((LANES, LANES), const),
        ],
        out_specs=[
            pl.BlockSpec((tm, D_MODEL), row),
            pl.BlockSpec((tm, ROW_WORDS), row),
            pl.BlockSpec((TOP_K, tm), tok),
            pl.BlockSpec((TOP_K, tm), tok),
            pl.BlockSpec((TOP_K, tm), tok),
            pl.BlockSpec((N_EXPERTS, LANES), const),
        ],
        out_shape=[
            jax.ShapeDtypeStruct((n, D_MODEL), F32),
            jax.ShapeDtypeStruct((n, ROW_WORDS), jnp.int32),
            jax.ShapeDtypeStruct((TOP_K, n), jnp.int32),
            jax.ShapeDtypeStruct((TOP_K, n), F32),
            jax.ShapeDtypeStruct((TOP_K, n), jnp.int32),
            jax.ShapeDtypeStruct((N_EXPERTS, LANES), F32),
        ],
        scratch_shapes=[pltpu.VMEM((N_EXPERTS, LANES), F32)],
        compiler_params=pltpu.CompilerParams(
            dimension_semantics=("arbitrary",), vmem_limit_bytes=VMEM_LIMIT),
        name="merge_router",
    )(o_a, h_m, proj, proj, h0, w_a, w_m, w_o, norm2_w, rw_t, rb_col, tri)


def _expert_kernel(be_ref, nu_ref, x_ref, wg_ref, bg_ref, wu_ref, bu_ref, wd_ref, bd_ref, y_ref,
                   wg_s, wu_s, wd_s):
    i = pl.program_id(0)
    used = i < nu_ref[0]

    @pl.when(jnp.logical_or(i == 0, be_ref[i] != be_ref[jnp.maximum(i - 1, 0)]))
    def _():
        wg_s[...] = wg_ref[0].astype(BF16)
        wu_s[...] = wu_ref[0].astype(BF16)
        wd_s[...] = wd_ref[0].astype(BF16)

    @pl.when(used)
    def _():
        x = _unpack_rows(x_ref[...]).astype(BF16)
        gt = jnp.minimum(_dot(x, wg_s[...]) + bg_ref[0], SWIGLU_LIMIT)
        up = jnp.clip(_dot(x, wu_s[...]) + bu_ref[0], -SWIGLU_LIMIT, SWIGLU_LIMIT)
        glu = gt * jax.nn.sigmoid(SWIGLU_ALPHA * gt)
        act = ((up + 1.0) * glu).astype(BF16)
        y_ref[...] = _pack_rows(_dot(act, wd_s[...]) + bd_ref[0])

    @pl.when(jnp.logical_not(used))
    def _():
        y_ref[...] = jnp.zeros(y_ref.shape, jnp.int32)


def _experts(blk_e, n_used, xs, w_g, b_g, w_u, b_u, w_d, b_d):
    cap = xs.shape[0]
    n_blocks = cap // MOE_ROWS
    wspec = lambda d0, d1: pl.BlockSpec((1, d0, d1), lambda i, be, nu: (be[i], 0, 0))
    return pl.pallas_call(
        _expert_kernel,
        grid_spec=pltpu.PrefetchScalarGridSpec(
            num_scalar_prefetch=2,
            grid=(n_blocks,),
            in_specs=[
                pl.BlockSpec((MOE_ROWS, ROW_WORDS), lambda i, be, nu: (i, 0)),
                wspec(D_MODEL, D_FF), wspec(1, D_FF),
                wspec(D_MODEL, D_FF), wspec(1, D_FF),
                wspec(D_FF, D_MODEL), wspec(1, D_MODEL),
            ],
            out_specs=pl.BlockSpec((MOE_ROWS, ROW_WORDS), lambda i, be, nu: (i, 0)),
            scratch_shapes=[pltpu.VMEM((D_MODEL, D_FF), BF16), pltpu.VMEM((D_MODEL, D_FF), BF16),
                            pltpu.VMEM((D_FF, D_MODEL), BF16)],
        ),
        out_shape=jax.ShapeDtypeStruct((cap, ROW_WORDS), jnp.int32),
        compiler_params=pltpu.CompilerParams(
            dimension_semantics=("arbitrary",), vmem_limit_bytes=VMEM_LIMIT),
        name="experts",
    )(blk_e, n_used, xs, w_g, b_g, w_u, b_u, w_d, b_d)


def _sc_worker_windows(n_win):
    return -(-n_win // (SC_CORES * SC_SUBCORES))


def _sc_dispatch(x_words, dest3, cap):
    n = x_words.shape[0]
    n_win = n // SC_WINDOW
    per = _sc_worker_windows(n_win)

    def body(x_hbm, d_hbm, o_hbm, idx_v, rows_v):
        wid = lax.axis_index("s") * SC_CORES + lax.axis_index("c")

        @pl.loop(0, per)
        def _(i):
            win = jnp.minimum(wid * per + i, n_win - 1)
            pltpu.sync_copy(d_hbm.at[win], idx_v)
            pltpu.sync_copy(x_hbm.at[pl.ds(win * SC_WINDOW, SC_WINDOW)], rows_v)
            for kk in range(TOP_K):
                pltpu.sync_copy(rows_v, o_hbm.at[idx_v.at[kk]])

    return pl.kernel(
        body,
        out_type=jax.ShapeDtypeStruct((cap, ROW_WORDS), jnp.int32),
        mesh=plsc.VectorSubcoreMesh(core_axis_name="c", subcore_axis_name="s"),
        scratch_types=[pltpu.VMEM((TOP_K, SC_WINDOW), jnp.int32),
                       pltpu.VMEM((SC_WINDOW, ROW_WORDS), jnp.int32)],
        name="moe_dispatch",
    )(x_words, dest3)


def _sc_gather(y_words, idx2):
    n_win = idx2.shape[0]
    per = _sc_worker_windows(n_win)

    def body(y_hbm, i_hbm, o_hbm, idx_v, rows_v, sem):
        wid = lax.axis_index("s") * SC_CORES + lax.axis_index("c")

        @pl.loop(0, per)
        def _(i):
            win = jnp.minimum(wid * per + i, n_win - 1)
            pltpu.sync_copy(i_hbm.at[win], idx_v)
            pltpu.async_copy(y_hbm.at[idx_v], rows_v, sem).wait()
            pltpu.sync_copy(rows_v, o_hbm.at[pl.ds(win * SC_WINDOW, SC_WINDOW)])

    return pl.kernel(
        body,
        out_type=jax.ShapeDtypeStruct((n_win * SC_WINDOW, ROW_WORDS), jnp.int32),
        mesh=plsc.VectorSubcoreMesh(core_axis_name="c", subcore_axis_name="s"),
        scratch_types=[pltpu.VMEM((SC_WINDOW,), jnp.int32),
                       pltpu.VMEM((SC_WINDOW, ROW_WORDS), jnp.int32),
                       pltpu.SemaphoreType.DMA],
        name="moe_gather",
    )(y_words, idx2)


def _combine_kernel(h_ref, y_ref, w_ref, fw_ref, o_ref):
    acc = h_ref[0]
    for kk in range(TOP_K):
        acc = acc + w_ref[0, :, kk:kk + 1] * _unpack_rows(y_ref[kk, 0])
    o_ref[0] = acc * lax.rsqrt(jnp.mean(acc * acc, axis=-1, keepdims=True) + EPS) * fw_ref[...]


def _combine(h1, yg, wts, final_w, b0, nb, lr):
    tr = _pick(lr, (512, 256, 128))
    return pl.pallas_call(
        _combine_kernel,
        grid=(nb, lr // tr),
        in_specs=[
            pl.BlockSpec((1, tr, D_MODEL), lambda i, j: (b0 + i, j, 0)),
            pl.BlockSpec((TOP_K, 1, tr, ROW_WORDS), lambda i, j: (0, b0 + i, j, 0)),
            pl.BlockSpec((1, tr, TOP_K), lambda i, j: (b0 + i, j, 0)),
            pl.BlockSpec((1, D_MODEL), lambda i, j: (0, 0)),
        ],
        out_specs=pl.BlockSpec((1, tr, D_MODEL), lambda i, j: (i, j, 0)),
        out_shape=jax.ShapeDtypeStruct((nb, lr, D_MODEL), F32),
        compiler_params=pltpu.CompilerParams(
            dimension_semantics=("arbitrary", "arbitrary"), vmem_limit_bytes=VMEM_LIMIT),
        name="combine",
    )(h1, yg, wts, final_w)


def _rope_tables(lr):
    l = lr + N_META
    half = ROT_DIM // 2
    pos = jnp.concatenate([jnp.arange(N_META, l, dtype=F32), jnp.arange(N_META, dtype=F32)])
    inv_freq = ROPE_THETA ** (-jnp.arange(0, ROT_DIM, 2, dtype=F32) / ROT_DIM)
    ang = pos[:, None] * inv_freq[None, :]
    cos, sin = jnp.cos(ang), jnp.sin(ang)
    z = jnp.zeros((l, A_QK_DIM - ROT_DIM), F32)
    zh = jnp.zeros((l, half), F32)
    cos64 = jnp.concatenate([cos, cos, z + 1.0], axis=1)
    sa64 = jnp.concatenate([-sin, zh, z], axis=1)
    sb64 = jnp.concatenate([zh, sin, z], axis=1)
    dup = lambda a: jnp.concatenate([a, a], axis=1)
    return dup(cos64), dup(sa64), dup(sb64)


def kernel(x_prompt, x_sample, meta_tokens, norm1_w, w_in, lambda_q1, lambda_k1, lambda_q2, lambda_k2, attn_subln_w, mlstm_gate_b, mlstm_norm_w, w_br_attn, w_br_mlstm, w_out, norm2_w, router_w, router_b, w_gate, b_gate, w_up, b_up, w_down, b_down, final_norm_w):
    bp, lr, _ = x_prompt.shape
    bs = x_sample.shape[0]
    b = bp + bs
    l = lr + N_META
    n = b * l

    w_in0 = w_in[0]
    g0 = 3 * 1024 + 2 * 512 + 2 * 1024
    w_main = jnp.concatenate([w_in0[:, :g0], w_in0[:, g0 + 4 * M_HEADS:]], axis=1).astype(BF16)
    w_gates = jnp.pad(w_in0[:, g0:g0 + 4 * M_HEADS], ((0, 0), (0, LANES - 4 * M_HEADS))).astype(BF16)
    lam = (jnp.exp(jnp.sum(lambda_q1[0] * lambda_k1[0])) - jnp.exp(jnp.sum(lambda_q2[0] * lambda_k2[0]))
           + LAM_INIT)
    lam_row = jnp.full((1, LANES), lam, F32)
    sub_w = (attn_subln_w[0] * (1.0 - LAM_INIT)).reshape(1, A_V_DIM)
    neg_row = jnp.where(jnp.arange(LANES) < N_META, 0.0, NEG_BIG).astype(F32).reshape(1, LANES)
    cos_t, sa_t, sb_t = _rope_tables(lr)
    gate_bias = jnp.broadcast_to(
        jnp.pad(mlstm_gate_b[0].T, ((0, 0), (0, 4)))[:, :, None], (M_HEADS, 8, LANES)).astype(F32)
    norm_w_m = mlstm_norm_w[0].reshape(M_HEADS, 1, M_V_DIM)
    rw_t = router_w[0].T.astype(BF16)
    rb_col = jnp.broadcast_to(router_b[0][:, None], (N_EXPERTS, LANES)).astype(F32)
    tri = (jnp.arange(LANES)[:, None] < jnp.arange(LANES)[None, :]).astype(BF16)

    x_all = jnp.concatenate([x_prompt, x_sample], axis=0)
    meta_b = jnp.broadcast_to(meta_tokens[None].astype(x_all.dtype), (b, N_META, D_MODEL))
    h0 = jnp.concatenate([x_all, meta_b], axis=1).reshape(n, D_MODEL)

    proj, gates = _inproj(h0, norm1_w[0].reshape(1, D_MODEL), w_main, w_gates)
    proj3 = proj.reshape(b, l, N_MAIN)

    o_a = _attention(proj3, cos_t, sa_t, sb_t, neg_row, lam_row, sub_w, lr)

    g4 = gates.reshape(b, l, 4, M_HEADS).transpose(0, 3, 2, 1)
    off = jnp.array([NEG_BIG, 1e4, NEG_BIG, 1e4], F32)[None, None, :, None]
    fill = jnp.broadcast_to(off, (b, M_HEADS, 4, CHUNK - N_META))
    g_row = jnp.concatenate([g4[..., :lr], fill, g4[..., lr:]], axis=-1)
    g_row = jnp.pad(g_row, ((0, 0), (0, 0), (0, 4), (0, 0)))
    h_m = _mlstm(proj3, g_row, gate_bias, norm_w_m, lr)

    h1, xn2, top_e, top_w, rank, cnt = _merge(
        o_a.reshape(n, D_MODEL), h_m.reshape(n, D_MODEL), proj, h0,
        w_br_attn[0].astype(BF16), w_br_mlstm[0].astype(BF16), w_out[0].astype(BF16),
        norm2_w[0].reshape(1, D_MODEL), rw_t, rb_col, tri)

    counts = cnt[:, 0].astype(jnp.int32)
    padded = (counts + MOE_ROWS - 1) // MOE_ROWS * MOE_ROWS
    pends = jnp.cumsum(padded)
    pstarts = pends - padded
    e_ids = jnp.arange(N_EXPERTS, dtype=jnp.int32)
    pstart_of = jnp.sum(jnp.where(top_e[..., None] == e_ids, pstarts, 0), axis=-1)
    dest = pstart_of + rank
    n_blocks = (n * TOP_K + N_EXPERTS * (MOE_ROWS - 1) + MOE_ROWS - 1) // MOE_ROWS
    cap = n_blocks * MOE_ROWS
    blk_start = jnp.arange(n_blocks, dtype=jnp.int32) * MOE_ROWS
    blk_e = jnp.minimum(jnp.sum((pends[None, :] <= blk_start[:, None]).astype(jnp.int32), axis=1),
                        N_EXPERTS - 1)
    n_used = (pends[-1:] // MOE_ROWS).astype(jnp.int32)

    dest_win = dest.reshape(TOP_K, n // SC_WINDOW, SC_WINDOW).transpose(1, 0, 2)
    xs = _sc_dispatch(xn2, dest_win, cap)
    y = _experts(blk_e, n_used, xs,
                 w_gate[0], b_gate[0].reshape(N_EXPERTS, 1, D_FF),
                 w_up[0], b_up[0].reshape(N_EXPERTS, 1, D_FF),
                 w_down[0], b_down[0].reshape(N_EXPERTS, 1, D_MODEL))
    yg = _sc_gather(y, dest.reshape(TOP_K * n // SC_WINDOW, SC_WINDOW)).reshape(TOP_K, b, l, ROW_WORDS)
    wts = top_w.T.reshape(b, l, TOP_K)
    h1_3 = h1.reshape(b, l, D_MODEL)
    fw = final_norm_w.reshape(1, D_MODEL)
    y_prompt = _combine(h1_3, yg, wts, fw, 0, bp, lr)
    y_sample = _combine(h1_3, yg, wts, fw, bp, bs, lr)
    return (y_prompt, y_sample)
```

```python
import functools
import math

import jax
import jax.numpy as jnp
import numpy as np
from jax import lax
from jax.experimental import pallas as pl
from jax.experimental.pallas import tpu as pltpu
from jax.experimental.pallas import tpu_sc as plsc

F32 = jnp.float32
BF16 = jnp.bfloat16

D_MODEL = 1024
N_META = 16
EPS = 1e-6
A_HEADS = 8
A_QK_DIM = 64
A_V_DIM = 128
ROT_DIM = 16
ROPE_THETA = 500000.0
M_HEADS = 4
M_QK_DIM = 128
M_V_DIM = 256
CHUNK = 128
N_EXPERTS = 32
TOP_K = 4
D_FF = 1024
SWIGLU_LIMIT = 7.0
SWIGLU_ALPHA = 1.702
LAM_INIT = 0.8 - 0.6 * math.exp(-0.3 * 0)

LANES = 128
NEG_BIG = -1e30
MOE_ROWS = 512
SC_CORES = 2
SC_SUBCORES = 16
SC_WINDOW = 128
ROW_WORDS = D_MODEL // 2
COL_AQ, COL_AK, COL_AV = 0, 8, 16
COL_MQ, COL_MK = 24, 28
COL_MV, COL_MO = 16, 20
COL_GA, COL_GM = 6, 7
N_MAIN = 8192

VMEM_LIMIT = 56 * 1024 * 1024


def _pick(n, prefs):
    for p in prefs:
        if n % p == 0:
            return p
    raise ValueError(f"no tile for {n}")


def _nt(a, b):
    return lax.dot_general(a, b, (((1,), (1,)), ((), ())), preferred_element_type=F32)


def _dot(a, b):
    return jnp.dot(a, b, preferred_element_type=F32)


def _pack_rows(x):
    w = x.shape[1] // 2
    bits = lambda v: lax.bitcast_convert_type(v.astype(BF16).astype(F32), jnp.uint32)
    lo = lax.shift_right_logical(bits(x[:, :w]), jnp.uint32(16))
    hi = bits(x[:, w:]) & jnp.uint32(0xFFFF0000)
    return lax.bitcast_convert_type(lo | hi, jnp.int32)


def _unpack_rows(wds):
    u = lax.bitcast_convert_type(wds, jnp.uint32)
    lo = lax.bitcast_convert_type(lax.shift_left(u, jnp.uint32(16)), F32)
    hi = lax.bitcast_convert_type(u & jnp.uint32(0xFFFF0000), F32)
    return jnp.concatenate([lo, hi], axis=1)


def _inproj_kernel(x_ref, n1_ref, w_ref, wg_ref, o_ref, g_ref, xn_ref):
    @pl.when(pl.program_id(1) == 0)
    def _():
        x = x_ref[...]
        ms = jnp.mean(x * x, axis=-1, keepdims=True)
        xn = (x * lax.rsqrt(ms + EPS) * n1_ref[...]).astype(BF16)
        xn_ref[...] = xn
        g_ref[...] = _dot(xn, wg_ref[...])[:, :4 * M_HEADS]

    o_ref[...] = _dot(xn_ref[...], w_ref[...]).astype(BF16)


def _inproj(h0, norm1_w, w_main, w_gates):
    n = h0.shape[0]
    tm = _pick(n, (1152, 768, 512, 384, 256, 128))
    tn = 2048
    return pl.pallas_call(
        _inproj_kernel,
        grid=(n // tm, N_MAIN // tn),
        in_specs=[
            pl.BlockSpec((tm, D_MODEL), lambda i, j: (i, 0)),
            pl.BlockSpec((1, D_MODEL), lambda i, j: (0, 0)),
            pl.BlockSpec((D_MODEL, tn), lambda i, j: (0, j)),
            pl.BlockSpec((D_MODEL, LANES), lambda i, j: (0, 0)),
        ],
        out_specs=[
            pl.BlockSpec((tm, tn), lambda i, j: (i, j)),
            pl.BlockSpec((tm, 4 * M_HEADS), lambda i, j: (i, 0)),
        ],
        out_shape=[
            jax.ShapeDtypeStruct((n, N_MAIN), BF16),
            jax.ShapeDtypeStruct((n, 4 * M_HEADS), F32),
        ],
        scratch_shapes=[pltpu.VMEM((tm, D_MODEL), BF16)],
        compiler_params=pltpu.CompilerParams(
            dimension_semantics=("arbitrary", "arbitrary"), vmem_limit_bytes=VMEM_LIMIT),
        name="inproj",
    )(h0, norm1_w, w_main, w_gates)


def _attn_kernel(q_ref, k_ref, v_ref, cos_ref, sa_ref, sb_ref, neg_ref, lam_ref, sw_ref,
                 o_ref, q1_ref, q2_ref, vs_ref, kt_ref, sa1_ref, sa2_ref, sb1_ref, sb2_ref, *, lr):
    l = lr + N_META
    lp = lr + LANES
    nblk = lr // LANES
    scale = A_QK_DIM ** -0.5 * math.log2(math.e)

    def rope(t):
        return (t * cos_ref[...] + pltpu.roll(t, LANES - ROT_DIM // 2, 1) * sa_ref[...]
                + pltpu.roll(t, ROT_DIM // 2, 1) * sb_ref[...])

    lane = lax.broadcasted_iota(jnp.int32, (l, LANES), 1)
    qr = rope(q_ref[0].astype(F32)) * scale
    q1_ref[0:l, :] = jnp.where(lane < A_QK_DIM, qr, 0.0).astype(BF16)
    q2_ref[0:l, :] = jnp.where(lane >= A_QK_DIM, qr, 0.0).astype(BF16)
    vs_ref[0:l, :] = v_ref[0]
    pad = jnp.zeros((lp - l, LANES), BF16)
    q1_ref[l:lp, :] = pad
    q2_ref[l:lp, :] = pad
    vs_ref[l:lp, :] = pad
    kr = rope(k_ref[0].astype(F32))
    for c in range(nblk):
        kt_ref[:, c * LANES:(c + 1) * LANES] = kr[c * LANES:(c + 1) * LANES, :].T.astype(BF16)
    k_tail = jnp.concatenate([kr[lr:l, :], jnp.zeros((lp - l, LANES), F32)], axis=0)
    kt_ref[:, lr:lp] = k_tail.T.astype(BF16)

    lam = lam_ref[:, 0:1]

    def scores(r0, s1_ref, s2_ref):
        rows = pl.ds(r0, LANES)
        k_all = kt_ref[...]
        s1_ref[...] = _dot(q1_ref[rows, :], k_all)
        s2_ref[...] = _dot(q2_ref[rows, :], k_all)

    def softmax_parts(s_ref):
        sm = s_ref[:, :lr]
        sl = s_ref[:, lr:] + neg_ref[...]
        m = jnp.maximum(jnp.max(sm, axis=1, keepdims=True), jnp.max(sl, axis=1, keepdims=True))
        pm = jnp.exp2(sm - m)
        pl_ = jnp.exp2(sl - m)
        tot = jnp.sum(pm, axis=1, keepdims=True) + jnp.sum(pl_, axis=1, keepdims=True)
        return pm, pl_, tot

    def finish(r0, nrows, s1_ref, s2_ref):
        p1m, p1l, t1 = softmax_parts(s1_ref)
        p2m, p2l, t2 = softmax_parts(s2_ref)
        c = lam * t1 / t2
        am = (p1m - p2m * c).astype(BF16)
        al = (p1l - p2l * c).astype(BF16)
        o = (_dot(am, vs_ref[0:lr, :]) + _dot(al, vs_ref[lr:lp, :])) * (1.0 / t1)
        o = o * lax.rsqrt(jnp.mean(o * o, axis=-1, keepdims=True) + EPS) * sw_ref[...]
        o_ref[0, pl.ds(r0, nrows), :] = o[:nrows].astype(BF16)

    scores(0, sa1_ref, sa2_ref)

    pairs = _pick(nblk // 2, (4, 2, 1))

    def body(j, c):
        for u in range(pairs):
            ra = pl.multiple_of((j * pairs + u) * (2 * LANES), LANES)
            rb = ra + LANES
            scores(rb, sb1_ref, sb2_ref)
            finish(ra, LANES, sa1_ref, sa2_ref)
            scores(rb + LANES, sa1_ref, sa2_ref)
            finish(rb, LANES, sb1_ref, sb2_ref)
        return c

    lax.fori_loop(0, nblk // (2 * pairs), body, 0)
    finish(lr, N_META, sa1_ref, sa2_ref)


def _attention(proj3, cos_t, sa_t, sb_t, neg_row, lam_row, sub_w, lr):
    b, l, _ = proj3.shape
    lp = lr + LANES
    const2 = lambda i, h: (0, 0)
    return pl.pallas_call(
        functools.partial(_attn_kernel, lr=lr),
        grid=(b, A_HEADS),
        in_specs=[
            pl.BlockSpec((1, l, LANES), lambda i, h: (i, 0, COL_AQ + h)),
            pl.BlockSpec((1, l, LANES), lambda i, h: (i, 0, COL_AK + h)),
            pl.BlockSpec((1, l, LANES), lambda i, h: (i, 0, COL_AV + h)),
            pl.BlockSpec((l, LANES), const2),
            pl.BlockSpec((l, LANES), const2),
            pl.BlockSpec((l, LANES), const2),
            pl.BlockSpec((1, LANES), const2),
            pl.BlockSpec((1, LANES), const2),
            pl.BlockSpec((1, LANES), const2),
        ],
        out_specs=pl.BlockSpec((1, l, LANES), lambda i, h: (i, 0, h)),
        out_shape=jax.ShapeDtypeStruct((b, l, A_HEADS * A_V_DIM), BF16),
        scratch_shapes=([pltpu.VMEM((lp, LANES), BF16)] * 3 + [pltpu.VMEM((LANES, lp), BF16)]
                        + [pltpu.VMEM((LANES, lp), F32)] * 4),
        compiler_params=pltpu.CompilerParams(
            dimension_semantics=("arbitrary", "arbitrary"), vmem_limit_bytes=VMEM_LIMIT),
        name="diff_attn",
    )(proj3, proj3, proj3, cos_t, sa_t, sb_t, neg_row, lam_row, sub_w)


def _log_sigmoid(x):
    return jnp.minimum(x, 0.0) - jnp.log(1.0 + jnp.exp(-jnp.abs(x)))


def _mlstm_kernel(q_ref, k_ref, v_ref, mo_ref, g_ref, gb_ref, nw_ref, o_ref,
                  c_ref, m_ref, kt_ref, h0_ref, u_ref, bcol_ref, amax_ref, aend_ref, bend_ref,
                  hf_ref, hb_ref, *, lr):
    l = lr + N_META
    t = CHUNK
    nc = lr // t
    dv = M_V_DIM
    scale = M_QK_DIM ** -0.5
    row = lax.broadcasted_iota(jnp.int32, (t, t), 0)
    col = lax.broadcasted_iota(jnp.int32, (t, t), 1)
    eye = row == col
    ones_tile = jnp.where(col == 0, 1.0, 0.0).astype(BF16)
    meta_r0 = l - t

    c_ref[...] = jnp.zeros(c_ref.shape, F32)
    m_ref[...] = jnp.zeros(m_ref.shape, F32)

    for c in range(nc + 1):
        r0 = c * t if c < nc else meta_r0
        kt_ref[:, c * t:(c + 1) * t] = k_ref[0, r0:r0 + t, :].astype(F32).T.astype(BF16)

    def scaled_q(r0):
        return (q_ref[0, pl.ds(r0, t), :].astype(F32) * scale).astype(BF16)

    def intra(ci, r0, g0):
        v_aug = jnp.concatenate([v_ref[0, pl.ds(r0, t), :], ones_tile], axis=1)
        kt = kt_ref[:, pl.ds(g0, t)]
        s_raw = _dot(scaled_q(r0), kt)
        for d, mask in ((0, col <= row), (1, col >= row)):
            gi = 2 * d
            ig = g_ref[0, 0, gi:gi + 1, pl.ds(g0, t)] + gb_ref[0, gi:gi + 1, :]
            lf = _log_sigmoid(g_ref[0, 0, gi + 1:gi + 2, pl.ds(g0, t)] + gb_ref[0, gi + 1:gi + 2, :])
            b_col = jnp.sum(jnp.where(mask, jnp.broadcast_to(lf, (t, t)), 0.0), axis=1, keepdims=True)
            b_row = jnp.sum(jnp.where(eye, jnp.broadcast_to(b_col, (t, t)), 0.0), axis=0, keepdims=True)
            a_row = ig - b_row
            dm = jnp.where(mask, jnp.broadcast_to(a_row, (t, t)), NEG_BIG)
            amax = jnp.max(dm, axis=1, keepdims=True)
            s0 = s_raw * jnp.exp(dm - amax)
            h0_ref[d, pl.ds(g0, t), :] = _dot(s0.astype(BF16), v_aug)
            a_end = jnp.max(a_row, axis=1, keepdims=True)
            kts = (kt.astype(F32) * jnp.exp(a_row - a_end)).astype(BF16)
            u_ref[d, ci] = _dot(kts, v_aug)
            bcol_ref[d, pl.ds(g0, t), :] = b_col
            amax_ref[d, pl.ds(g0, t), :] = amax
            aend_ref[d, ci] = jnp.broadcast_to(a_end, (8, LANES))
            bend_ref[d, ci] = jnp.broadcast_to(jnp.sum(lf, axis=1, keepdims=True), (8, LANES))

    ua = _pick(nc, (4, 2, 1))

    def abody(i, c):
        for u in range(ua):
            ci = ua * i + u
            r0 = pl.multiple_of(ci * t, t)
            intra(ci, r0, r0)
        return c

    lax.fori_loop(0, nc // ua, abody, 0)
    intra(nc, meta_r0, lr)

    def step(d, ci, r0, g0):
        m_prev = m_ref[d, 0:1, 0:1]
        amax = amax_ref[d, pl.ds(g0, t), :]
        m_col = jnp.maximum(m_prev, amax)
        num = (jnp.exp(m_prev - m_col) * _dot(scaled_q(r0), c_ref[d].astype(BF16))
               + jnp.exp(amax - m_col) * h0_ref[d, pl.ds(g0, t), :])
        den = jnp.maximum(jnp.abs(num[:, dv:dv + 1]), jnp.exp(-(bcol_ref[d, pl.ds(g0, t), :] + m_col)))
        a_end = aend_ref[d, ci, 0:1, 0:1]
        m_end = jnp.maximum(m_prev, a_end)
        c_ref[d] = jnp.exp(m_prev - m_end) * c_ref[d] + jnp.exp(a_end - m_end) * u_ref[d, ci]
        m_ref[d] = jnp.broadcast_to(bend_ref[d, ci, 0:1, 0:1] + m_end, (8, LANES))
        return num[:, :dv] * (1.0 / den)

    hf_ref[lr:l, :] = step(0, nc, meta_r0, lr)[t - N_META:, :]

    ub = _pick(nc, (2, 1))

    def bbody(i, c):
        for u in range(ub):
            jf = ub * i + u
            rf = pl.multiple_of(jf * t, t)
            hf_ref[pl.ds(rf, t), :] = step(0, jf, rf, rf)
            jb = nc - 1 - jf
            rb = pl.multiple_of(jb * t, t)
            hb_ref[pl.ds(rb, t), :] = step(1, jb, rb, rb)
        return c

    lax.fori_loop(0, nc // ub, bbody, 0)
    hb_ref[lr:l, :] = step(1, nc, meta_r0, lr)[t - N_META:, :]

    def finish(r0, rows):
        hs = hf_ref[pl.ds(r0, rows), :] + hb_ref[pl.ds(r0, rows), :]
        y = hs * lax.rsqrt(jnp.mean(hs * hs, axis=-1, keepdims=True) + EPS) * nw_ref[0]
        gate = jax.nn.sigmoid(mo_ref[0, pl.ds(r0, rows), :].astype(F32))
        o_ref[0, pl.ds(r0, rows), :] = (y * gate).astype(BF16)

    def fbody(i, c):
        for u in range(ua):
            finish(pl.multiple_of((ua * i + u) * t, t), t)
        return c

    lax.fori_loop(0, nc // ua, fbody, 0)
    finish(lr, N_META)


def _mlstm(proj3, gates_row, gate_bias, norm_w, lr):
    b, l, _ = proj3.shape
    gl = gates_row.shape[-1]
    nw = lr // CHUNK + 1
    aug = M_V_DIM + LANES
    return pl.pallas_call(
        functools.partial(_mlstm_kernel, lr=lr),
        grid=(b, M_HEADS),
        in_specs=[
            pl.BlockSpec((1, l, M_QK_DIM), lambda i, h: (i, 0, COL_MQ + h)),
            pl.BlockSpec((1, l, M_QK_DIM), lambda i, h: (i, 0, COL_MK + h)),
            pl.BlockSpec((1, l, M_V_DIM), lambda i, h: (i, 0, COL_MV + h)),
            pl.BlockSpec((1, l, M_V_DIM), lambda i, h: (i, 0, COL_MO + h)),
            pl.BlockSpec((1, 1, 8, gl), lambda i, h: (i, h, 0, 0)),
            pl.BlockSpec((1, 8, LANES), lambda i, h: (h, 0, 0)),
            pl.BlockSpec((1, 1, M_V_DIM), lambda i, h: (h, 0, 0)),
        ],
        out_specs=pl.BlockSpec((1, l, M_V_DIM), lambda i, h: (i, 0, h)),
        out_shape=jax.ShapeDtypeStruct((b, l, M_HEADS * M_V_DIM), BF16),
        scratch_shapes=[
            pltpu.VMEM((2, M_QK_DIM, aug), F32),
            pltpu.VMEM((2, 8, LANES), F32),
            pltpu.VMEM((M_QK_DIM, gl), BF16),
            pltpu.VMEM((2, gl, aug), F32),
            pltpu.VMEM((2, nw, M_QK_DIM, aug), F32),
            pltpu.VMEM((2, gl, 1), F32),
            pltpu.VMEM((2, gl, 1), F32),
            pltpu.VMEM((2, nw, 8, LANES), F32),
            pltpu.VMEM((2, nw, 8, LANES), F32),
            pltpu.VMEM((l, M_V_DIM), F32), pltpu.VMEM((l, M_V_DIM), F32),
        ],
        compiler_params=pltpu.CompilerParams(
            dimension_semantics=("arbitrary", "arbitrary"), vmem_limit_bytes=VMEM_LIMIT),
        name="mlstm",
    )(proj3, proj3, proj3, proj3, gates_row, gate_bias, norm_w)


def _merge_kernel(oa_ref, hm_ref, ga_ref, gm_ref, h_ref, wa_ref, wm_ref, wo_ref, n2_ref,
                  rw_ref, rb_ref, tri_ref,
                  h1_ref, xn_ref, te_ref, tw_ref, rk_ref, cnt_ref, carry_ref, *, tm):
    @pl.when(pl.program_id(0) == 0)
    def _():
        carry_ref[...] = jnp.zeros(carry_ref.shape, F32)

    pa = _dot(oa_ref[...], wa_ref[...])
    pm = _dot(hm_ref[...], wm_ref[...])
    merged = (jax.nn.sigmoid(ga_ref[...].astype(F32)) * pa
              + jax.nn.sigmoid(gm_ref[...].astype(F32)) * pm)
    h1 = h_ref[...] + _dot(merged.astype(BF16), wo_ref[...])
    h1_ref[...] = h1
    xn32 = h1 * lax.rsqrt(jnp.mean(h1 * h1, axis=-1, keepdims=True) + EPS) * n2_ref[...]
    xn = xn32.astype(BF16)
    xn_ref[...] = _pack_rows(xn32)

    logits = _nt(rw_ref[...], xn) + rb_ref[:, 0:1]
    eidx = lax.broadcasted_iota(jnp.int32, (N_EXPERTS, tm), 0)
    work = logits
    vals, hots = [], []
    for _ in range(TOP_K):
        mx = jnp.max(work, axis=0, keepdims=True)
        sel = jnp.min(jnp.where(work == mx, eidx, N_EXPERTS), axis=0, keepdims=True)
        hot = eidx == sel
        vals.append(mx)
        hots.append(hot)
        work = jnp.where(hot, -jnp.inf, work)
    ex = [jnp.exp(v - vals[0]) for v in vals]
    inv = 1.0 / (ex[0] + ex[1] + ex[2] + ex[3])
    chosen = jnp.where(hots[0] | hots[1] | hots[2] | hots[3], 1.0, 0.0)

    carry = carry_ref[:, 0:1]
    ranks = []
    for s in range(tm // LANES):
        sub = chosen[:, s * LANES:(s + 1) * LANES]
        ranks.append(_dot(sub.astype(BF16), tri_ref[...]) + carry)
        carry = carry + jnp.sum(sub, axis=1, keepdims=True)
    before = jnp.concatenate(ranks, axis=1) if len(ranks) > 1 else ranks[0]
    carry_ref[...] = jnp.broadcast_to(carry, carry_ref.shape)
    cnt_ref[...] = jnp.broadcast_to(carry, cnt_ref.shape)

    for kk in range(TOP_K):
        te_ref[kk:kk + 1, :] = jnp.sum(jnp.where(hots[kk], eidx, 0), axis=0, keepdims=True)
        tw_ref[kk:kk + 1, :] = ex[kk] * inv
        rk_ref[kk:kk + 1, :] = jnp.sum(jnp.where(hots[kk], before, 0.0), axis=0,
                                       keepdims=True).astype(jnp.int32)


def _merge(o_a, h_m, proj, h0, w_a, w_m, w_o, norm2_w, rw_t, rb_col, tri):
    n = h0.shape[0]
    tm = _pick(n, (384, 256, 128))
    row = lambda i: (i, 0)
    const = lambda i: (0, 0)
    tok = lambda i: (0, i)
    wspec = pl.BlockSpec((D_MODEL, D_MODEL), const)
    return pl.pallas_call(
        functools.partial(_merge_kernel, tm=tm),
        grid=(n // tm,),
        in_specs=[
            pl.BlockSpec((tm, D_MODEL), row),
            pl.BlockSpec((tm, D_MODEL), row),
            pl.BlockSpec((tm, D_MODEL), lambda i: (i, COL_GA)),
            pl.BlockSpec((tm, D_MODEL), lambda i: (i, COL_GM)),
            pl.BlockSpec((tm, D_MODEL), row),
            wspec, wspec, wspec,
            pl.BlockSpec((1, D_MODEL), const),
            pl.BlockSpec((N_EXPERTS, D_MODEL), const),
            pl.BlockSpec((N_EXPERTS, LANES), const),
            pl.BlockSpec((LANES, LANES), const),
        ],
        out_specs=[
            pl.BlockSpec((tm, D_MODEL), row),
            pl.BlockSpec((tm, ROW_WORDS), row),
            pl.BlockSpec((TOP_K, tm), tok),
            pl.BlockSpec((TOP_K, tm), tok),
            pl.BlockSpec((TOP_K, tm), tok),
            pl.BlockSpec((N_EXPERTS, LANES), const),
        ],
        out_shape=[
            jax.ShapeDtypeStruct((n, D_MODEL), F32),
            jax.ShapeDtypeStruct((n, ROW_WORDS), jnp.int32),
            jax.ShapeDtypeStruct((TOP_K, n), jnp.int32),
            jax.ShapeDtypeStruct((TOP_K, n), F32),
            jax.ShapeDtypeStruct((TOP_K, n), jnp.int32),
            jax.ShapeDtypeStruct((N_EXPERTS, LANES), F32),
        ],
        scratch_shapes=[pltpu.VMEM((N_EXPERTS, LANES), F32)],
        compiler_params=pltpu.CompilerParams(
            dimension_semantics=("arbitrary",), vmem_limit_bytes=VMEM_LIMIT),
        name="merge_router",
    )(o_a, h_m, proj, proj, h0, w_a, w_m, w_o, norm2_w, rw_t, rb_col, tri)


def _expert_kernel(be_ref, nu_ref, x_ref, wg_ref, bg_ref, wu_ref, bu_ref, wd_ref, bd_ref, y_ref,
                   wg_s, wu_s, wd_s):
    i = pl.program_id(0)
    used = i < nu_ref[0]

    @pl.when(jnp.logical_or(i == 0, be_ref[i] != be_ref[jnp.maximum(i - 1, 0)]))
    def _():
        wg_s[...] = wg_ref[0].astype(BF16)
        wu_s[...] = wu_ref[0].astype(BF16)
        wd_s[...] = wd_ref[0].astype(BF16)

    @pl.when(used)
    def _():
        x = _unpack_rows(x_ref[...]).astype(BF16)
        gt = jnp.minimum(_dot(x, wg_s[...]) + bg_ref[0], SWIGLU_LIMIT)
        up = jnp.clip(_dot(x, wu_s[...]) + bu_ref[0], -SWIGLU_LIMIT, SWIGLU_LIMIT)
        glu = gt * jax.nn.sigmoid(SWIGLU_ALPHA * gt)
        act = ((up + 1.0) * glu).astype(BF16)
        y_ref[...] = _pack_rows(_dot(act, wd_s[...]) + bd_ref[0])

    @pl.when(jnp.logical_not(used))
    def _():
        y_ref[...] = jnp.zeros(y_ref.shape, jnp.int32)


def _experts(blk_e, n_used, xs, w_g, b_g, w_u, b_u, w_d, b_d):
    cap = xs.shape[0]
    n_blocks = cap // MOE_ROWS
    wspec = lambda d0, d1: pl.BlockSpec((1, d0, d1), lambda i, be, nu: (be[i], 0, 0))
    return pl.pallas_call(
        _expert_kernel,
        grid_spec=pltpu.PrefetchScalarGridSpec(
            num_scalar_prefetch=2,
            grid=(n_blocks,),
            in_specs=[
                pl.BlockSpec((MOE_ROWS, ROW_WORDS), lambda i, be, nu: (i, 0)),
                wspec(D_MODEL, D_FF), wspec(1, D_FF),
                wspec(D_MODEL, D_FF), wspec(1, D_FF),
                wspec(D_FF, D_MODEL), wspec(1, D_MODEL),
            ],
            out_specs=pl.BlockSpec((MOE_ROWS, ROW_WORDS), lambda i, be, nu: (i, 0)),
            scratch_shapes=[pltpu.VMEM((D_MODEL, D_FF), BF16), pltpu.VMEM((D_MODEL, D_FF), BF16),
                            pltpu.VMEM((D_FF, D_MODEL), BF16)],
        ),
        out_shape=jax.ShapeDtypeStruct((cap, ROW_WORDS), jnp.int32),
        compiler_params=pltpu.CompilerParams(
            dimension_semantics=("arbitrary",), vmem_limit_bytes=VMEM_LIMIT),
        name="experts",
    )(blk_e, n_used, xs, w_g, b_g, w_u, b_u, w_d, b_d)


def _sc_worker_windows(n_win):
    return -(-n_win // (SC_CORES * SC_SUBCORES))


def _sc_dispatch(x_words, dest3, cap):
    n = x_words.shape[0]
    n_win = n // SC_WINDOW
    per = _sc_worker_windows(n_win)

    def body(x_hbm, d_hbm, o_hbm, idx_v, rows_v):
        wid = lax.axis_index("s") * SC_CORES + lax.axis_index("c")

        @pl.loop(0, per)
        def _(i):
            win = jnp.minimum(wid * per + i, n_win - 1)
            pltpu.sync_copy(d_hbm.at[win], idx_v)
            pltpu.sync_copy(x_hbm.at[pl.ds(win * SC_WINDOW, SC_WINDOW)], rows_v)
            for kk in range(TOP_K):
                pltpu.sync_copy(rows_v, o_hbm.at[idx_v.at[kk]])

    return pl.kernel(
        body,
        out_type=jax.ShapeDtypeStruct((cap, ROW_WORDS), jnp.int32),
        mesh=plsc.VectorSubcoreMesh(core_axis_name="c", subcore_axis_name="s"),
        scratch_types=[pltpu.VMEM((TOP_K, SC_WINDOW), jnp.int32),
                       pltpu.VMEM((SC_WINDOW, ROW_WORDS), jnp.int32)],
        name="moe_dispatch",
    )(x_words, dest3)


def _sc_gather(y_words, idx2):
    n_win = idx2.shape[0]
    per = _sc_worker_windows(n_win)

    def body(y_hbm, i_hbm, o_hbm, idx_v, rows_v, sem):
        wid = lax.axis_index("s") * SC_CORES + lax.axis_index("c")

        @pl.loop(0, per)
        def _(i):
            win = jnp.minimum(wid * per + i, n_win - 1)
            pltpu.sync_copy(i_hbm.at[win], idx_v)
            pltpu.async_copy(y_hbm.at[idx_v], rows_v, sem).wait()
            pltpu.sync_copy(rows_v, o_hbm.at[pl.ds(win * SC_WINDOW, SC_WINDOW)])

    return pl.kernel(
        body,
        out_type=jax.ShapeDtypeStruct((n_win * SC_WINDOW, ROW_WORDS), jnp.int32),
        mesh=plsc.VectorSubcoreMesh(core_axis_name="c", subcore_axis_name="s"),
        scratch_types=[pltpu.VMEM((SC_WINDOW,), jnp.int32),
                       pltpu.VMEM((SC_WINDOW, ROW_WORDS), jnp.int32),
                       pltpu.SemaphoreType.DMA],
        name="moe_gather",
    )(y_words, idx2)


def _combine_kernel(h_ref, y_ref, w_ref, fw_ref, o_ref):
    acc = h_ref[0]
    for kk in range(TOP_K):
        acc = acc + w_ref[0, :, kk:kk + 1] * _unpack_rows(y_ref[kk, 0])
    o_ref[0] = acc * lax.rsqrt(jnp.mean(acc * acc, axis=-1, keepdims=True) + EPS) * fw_ref[...]


def _combine(h1, yg, wts, final_w, b0, nb, lr):
    tr = _pick(lr, (512, 256, 128))
    return pl.pallas_call(
        _combine_kernel,
        grid=(nb, lr // tr),
        in_specs=[
            pl.BlockSpec((1, tr, D_MODEL), lambda i, j: (b0 + i, j, 0)),
            pl.BlockSpec((TOP_K, 1, tr, ROW_WORDS), lambda i, j: (0, b0 + i, j, 0)),
            pl.BlockSpec((1, tr, TOP_K), lambda i, j: (b0 + i, j, 0)),
            pl.BlockSpec((1, D_MODEL), lambda i, j: (0, 0)),
        ],
        out_specs=pl.BlockSpec((1, tr, D_MODEL), lambda i, j: (i, j, 0)),
        out_shape=jax.ShapeDtypeStruct((nb, lr, D_MODEL), F32),
        compiler_params=pltpu.CompilerParams(
            dimension_semantics=("arbitrary", "arbitrary"), vmem_limit_bytes=VMEM_LIMIT),
        name="combine",
    )(h1, yg, wts, final_w)


def _rope_tables(lr):
    l = lr + N_META
    half = ROT_DIM // 2
    pos = jnp.concatenate([jnp.arange(N_META, l, dtype=F32), jnp.arange(N_META, dtype=F32)])
    inv_freq = ROPE_THETA ** (-jnp.arange(0, ROT_DIM, 2, dtype=F32) / ROT_DIM)
    ang = pos[:, None] * inv_freq[None, :]
    cos, sin = jnp.cos(ang), jnp.sin(ang)
    z = jnp.zeros((l, A_QK_DIM - ROT_DIM), F32)
    zh = jnp.zeros((l, half), F32)
    cos64 = jnp.concatenate([cos, cos, z + 1.0], axis=1)
    sa64 = jnp.concatenate([-sin, zh, z], axis=1)
    sb64 = jnp.concatenate([zh, sin, z], axis=1)
    dup = lambda a: jnp.concatenate([a, a], axis=1)
    return dup(cos64), dup(sa64), dup(sb64)


def kernel(x_prompt, x_sample, meta_tokens, norm1_w, w_in, lambda_q1, lambda_k1, lambda_q2, lambda_k2, attn_subln_w, mlstm_gate_b, mlstm_norm_w, w_br_attn, w_br_mlstm, w_out, norm2_w, router_w, router_b, w_gate, b_gate, w_up, b_up, w_down, b_down, final_norm_w):
    bp, lr, _ = x_prompt.shape
    bs = x_sample.shape[0]
    b = bp + bs
    l = lr + N_META
    n = b * l

    w_in0 = w_in[0]
    g0 = 3 * 1024 + 2 * 512 + 2 * 1024
    w_main = jnp.concatenate([w_in0[:, :g0], w_in0[:, g0 + 4 * M_HEADS:]], axis=1).astype(BF16)
    w_gates = jnp.pad(w_in0[:, g0:g0 + 4 * M_HEADS], ((0, 0), (0, LANES - 4 * M_HEADS))).astype(BF16)
    lam = (jnp.exp(jnp.sum(lambda_q1[0] * lambda_k1[0])) - jnp.exp(jnp.sum(lambda_q2[0] * lambda_k2[0]))
           + LAM_INIT)
    lam_row = jnp.full((1, LANES), lam, F32)
    sub_w = (attn_subln_w[0] * (1.0 - LAM_INIT)).reshape(1, A_V_DIM)
    neg_row = jnp.where(jnp.arange(LANES) < N_META, 0.0, NEG_BIG).astype(F32).reshape(1, LANES)
    cos_t, sa_t, sb_t = _rope_tables(lr)
    gate_bias = jnp.broadcast_to(
        jnp.pad(mlstm_gate_b[0].T, ((0, 0), (0, 4)))[:, :, None], (M_HEADS, 8, LANES)).astype(F32)
    norm_w_m = mlstm_norm_w[0].reshape(M_HEADS, 1, M_V_DIM)
    rw_t = router_w[0].T.astype(BF16)
    rb_col = jnp.broadcast_to(router_b[0][:, None], (N_EXPERTS, LANES)).astype(F32)
    tri = (jnp.arange(LANES)[:, None] < jnp.arange(LANES)[None, :]).astype(BF16)

    x_all = jnp.concatenate([x_prompt, x_sample], axis=0)
    meta_b = jnp.broadcast_to(meta_tokens[None].astype(x_all.dtype), (b, N_META, D_MODEL))
    h0 = jnp.concatenate([x_all, meta_b], axis=1).reshape(n, D_MODEL)

    proj, gates = _inproj(h0, norm1_w[0].reshape(1, D_MODEL), w_main, w_gates)
    proj3 = proj.reshape(b, l, N_MAIN)

    o_a = _attention(proj3, cos_t, sa_t, sb_t, neg_row, lam_row, sub_w, lr)

    g4 = gates.reshape(b, l, 4, M_HEADS).transpose(0, 3, 2, 1)
    off = jnp.array([NEG_BIG, 1e4, NEG_BIG, 1e4], F32)[None, None, :, None]
    fill = jnp.broadcast_to(off, (b, M_HEADS, 4, CHUNK - N_META))
    g_row = jnp.concatenate([g4[..., :lr], fill, g4[..., lr:]], axis=-1)
    g_row = jnp.pad(g_row, ((0, 0), (0, 0), (0, 4), (0, 0)))
    h_m = _mlstm(proj3, g_row, gate_bias, norm_w_m, lr)

    h1, xn2, top_e, top_w, rank, cnt = _merge(
        o_a.reshape(n, D_MODEL), h_m.reshape(n, D_MODEL), proj, h0,
        w_br_attn[0].astype(BF16), w_br_mlstm[0].astype(BF16), w_out[0].astype(BF16),
        norm2_w[0].reshape(1, D_MODEL), rw_t, rb_col, tri)

    counts = cnt[:, 0].astype(jnp.int32)
    padded = (counts + MOE_ROWS - 1) // MOE_ROWS * MOE_ROWS
    pends = jnp.cumsum(padded)
    pstarts = pends - padded
    e_ids = jnp.arange(N_EXPERTS, dtype=jnp.int32)
    pstart_of = jnp.sum(jnp.where(top_e[..., None] == e_ids, pstarts, 0), axis=-1)
    dest = pstart_of + rank
    n_blocks = (n * TOP_K + N_EXPERTS * (MOE_ROWS - 1) + MOE_ROWS - 1) // MOE_ROWS
    cap = n_blocks * MOE_ROWS
    blk_start = jnp.arange(n_blocks, dtype=jnp.int32) * MOE_ROWS
    blk_e = jnp.minimum(jnp.sum((pends[None, :] <= blk_start[:, None]).astype(jnp.int32), axis=1),
                        N_EXPERTS - 1)
    n_used = (pends[-1:] // MOE_ROWS).astype(jnp.int32)

    dest_win = dest.reshape(TOP_K, n // SC_WINDOW, SC_WINDOW).transpose(1, 0, 2)
    xs = _sc_dispatch(xn2, dest_win, cap)
    y = _experts(blk_e, n_used, xs,
                 w_gate[0], b_gate[0].reshape(N_EXPERTS, 1, D_FF),
                 w_up[0], b_up[0].reshape(N_EXPERTS, 1, D_FF),
                 w_down[0], b_down[0].reshape(N_EXPERTS, 1, D_MODEL))
    yg = _sc_gather(y, dest.reshape(TOP_K * n // SC_WINDOW, SC_WINDOW)).reshape(TOP_K, b, l, ROW_WORDS)
    wts = top_w.T.reshape(b, l, TOP_K)
    h1_3 = h1.reshape(b, l, D_MODEL)
    fw = final_norm_w.reshape(1, D_MODEL)
    y_prompt = _combine(h1_3, yg, wts, fw, 0, bp, lr)
    y_sample = _combine(h1_3, yg, wts, fw, bp, bs, lr)
    return (y_prompt, y_sample)
```

```python
import functools
import math

import jax
import jax.numpy as jnp
import numpy as np
from jax import lax
from jax.experimental import pallas as pl
from jax.experimental.pallas import tpu as pltpu
from jax.experimental.pallas import tpu_sc as plsc

F32 = jnp.float32
BF16 = jnp.bfloat16

D_MODEL = 1024
N_META = 16
EPS = 1e-6
A_HEADS = 8
A_QK_DIM = 64
A_V_DIM = 128
ROT_DIM = 16
ROPE_THETA = 500000.0
M_HEADS = 4
M_QK_DIM = 128
M_V_DIM = 256
CHUNK = 128
N_EXPERTS = 32
TOP_K = 4
D_FF = 1024
SWIGLU_LIMIT = 7.0
SWIGLU_ALPHA = 1.702
LAM_INIT = 0.8 - 0.6 * math.exp(-0.3 * 0)

LANES = 128
NEG_BIG = -1e30
MOE_ROWS = 512
SC_CORES = 2
SC_SUBCORES = 16
SC_WINDOW = 128
ROW_WORDS = D_MODEL // 2
COL_AQ, COL_AK, COL_AV = 0, 8, 16
COL_MQ, COL_MK = 24, 28
COL_MV, COL_MO = 16, 20
COL_GA, COL_GM = 6, 7
N_MAIN = 8192

VMEM_LIMIT = 56 * 1024 * 1024


def _pick(n, prefs):
    for p in prefs:
        if n % p == 0:
            return p
    raise ValueError(f"no tile for {n}")


def _nt(a, b):
    return lax.dot_general(a, b, (((1,), (1,)), ((), ())), preferred_element_type=F32)


def _dot(a, b):
    return jnp.dot(a, b, preferred_element_type=F32)


def _pack_rows(x):
    w = x.shape[1] // 2
    bits = lambda v: lax.bitcast_convert_type(v.astype(BF16).astype(F32), jnp.uint32)
    lo = lax.shift_right_logical(bits(x[:, :w]), jnp.uint32(16))
    hi = bits(x[:, w:]) & jnp.uint32(0xFFFF0000)
    return lax.bitcast_convert_type(lo | hi, jnp.int32)


def _unpack_rows(wds):
    u = lax.bitcast_convert_type(wds, jnp.uint32)
    lo = lax.bitcast_convert_type(lax.shift_left(u, jnp.uint32(16)), F32)
    hi = lax.bitcast_convert_type(u & jnp.uint32(0xFFFF0000), F32)
    return jnp.concatenate([lo, hi], axis=1)


def _inproj_kernel(x_ref, n1_ref, w_ref, wg_ref, o_ref, g_ref, xn_ref):
    @pl.when(pl.program_id(1) == 0)
    def _():
        x = x_ref[...]
        ms = jnp.mean(x * x, axis=-1, keepdims=True)
        xn = (x * lax.rsqrt(ms + EPS) * n1_ref[...]).astype(BF16)
        xn_ref[...] = xn
        g_ref[...] = _dot(xn, wg_ref[...])[:, :4 * M_HEADS]

    o_ref[...] = _dot(xn_ref[...], w_ref[...]).astype(BF16)


def _inproj(h0, norm1_w, w_main, w_gates):
    n = h0.shape[0]
    tm = _pick(n, (1152, 768, 512, 384, 256, 128))
    tn = 2048
    return pl.pallas_call(
        _inproj_kernel,
        grid=(n // tm, N_MAIN // tn),
        in_specs=[
            pl.BlockSpec((tm, D_MODEL), lambda i, j: (i, 0)),
            pl.BlockSpec((1, D_MODEL), lambda i, j: (0, 0)),
            pl.BlockSpec((D_MODEL, tn), lambda i, j: (0, j)),
            pl.BlockSpec((D_MODEL, LANES), lambda i, j: (0, 0)),
        ],
        out_specs=[
            pl.BlockSpec((tm, tn), lambda i, j: (i, j)),
            pl.BlockSpec((tm, 4 * M_HEADS), lambda i, j: (i, 0)),
        ],
        out_shape=[
            jax.ShapeDtypeStruct((n, N_MAIN), BF16),
            jax.ShapeDtypeStruct((n, 4 * M_HEADS), F32),
        ],
        scratch_shapes=[pltpu.VMEM((tm, D_MODEL), BF16)],
        compiler_params=pltpu.CompilerParams(
            dimension_semantics=("arbitrary", "arbitrary"), vmem_limit_bytes=VMEM_LIMIT),
        name="inproj",
    )(h0, norm1_w, w_main, w_gates)


def _attn_kernel(q_ref, k_ref, v_ref, cos_ref, sa_ref, sb_ref, neg_ref, lam_ref, sw_ref,
                 o_ref, q1_ref, q2_ref, vs_ref, kt_ref, sa1_ref, sa2_ref, sb1_ref, sb2_ref, *, lr):
    l = lr + N_META
    lp = lr + LANES
    nblk = lr // LANES
    scale = A_QK_DIM ** -0.5 * math.log2(math.e)

    def rope(t):
        return (t * cos_ref[...] + pltpu.roll(t, LANES - ROT_DIM // 2, 1) * sa_ref[...]
                + pltpu.roll(t, ROT_DIM // 2, 1) * sb_ref[...])

    lane = lax.broadcasted_iota(jnp.int32, (l, LANES), 1)
    qr = rope(q_ref[0].astype(F32)) * scale
    q1_ref[0:l, :] = jnp.where(lane < A_QK_DIM, qr, 0.0).astype(BF16)
    q2_ref[0:l, :] = jnp.where(lane >= A_QK_DIM, qr, 0.0).astype(BF16)
    vs_ref[0:l, :] = v_ref[0]
    pad = jnp.zeros((lp - l, LANES), BF16)
    q1_ref[l:lp, :] = pad
    q2_ref[l:lp, :] = pad
    vs_ref[l:lp, :] = pad
    kr = rope(k_ref[0].astype(F32))
    for c in range(nblk):
        kt_ref[:, c * LANES:(c + 1) * LANES] = kr[c * LANES:(c + 1) * LANES, :].T.astype(BF16)
    k_tail = jnp.concatenate([kr[lr:l, :], jnp.zeros((lp - l, LANES), F32)], axis=0)
    kt_ref[:, lr:lp] = k_tail.T.astype(BF16)

    lam = lam_ref[:, 0:1]

    def scores(r0, s1_ref, s2_ref):
        rows = pl.ds(r0, LANES)
        k_all = kt_ref[...]
        s1_ref[...] = _dot(q1_ref[rows, :], k_all)
        s2_ref[...] = _dot(q2_ref[rows, :], k_all)

    def softmax_parts(s_ref, nrows):
        sm = s_ref[0:nrows, :lr]
        sl = s_ref[0:nrows, lr:] + neg_ref[...]
        m = jnp.maximum(jnp.max(sm, axis=1, keepdims=True), jnp.max(sl, axis=1, keepdims=True))
        pm = jnp.exp2(sm - m)
        pl_ = jnp.exp2(sl - m)
        tot = jnp.sum(pm, axis=1, keepdims=True) + jnp.sum(pl_, axis=1, keepdims=True)
        return pm, pl_, tot

    def finish(r0, nrows, s1_ref, s2_ref):
        p1m, p1l, t1 = softmax_parts(s1_ref, nrows)
        p2m, p2l, t2 = softmax_parts(s2_ref, nrows)
        c = lam * t1 / t2
        am = (p1m - p2m * c).astype(BF16)
        al = (p1l - p2l * c).astype(BF16)
        o = (_dot(am, vs_ref[0:lr, :]) + _dot(al, vs_ref[lr:lp, :])) * (1.0 / t1)
        o = o * lax.rsqrt(jnp.mean(o * o, axis=-1, keepdims=True) + EPS) * sw_ref[...]
        o_ref[0, pl.ds(r0, nrows), :] = o.astype(BF16)

    scores(0, sa1_ref, sa2_ref)

    pairs = _pick(nblk // 2, (8, 4, 2, 1))

    def body(j, c):
        for u in range(pairs):
            ra = pl.multiple_of((j * pairs + u) * (2 * LANES), LANES)
            rb = ra + LANES
            scores(rb, sb1_ref, sb2_ref)
            finish(ra, LANES, sa1_ref, sa2_ref)
            scores(rb + LANES, sa1_ref, sa2_ref)
            finish(rb, LANES, sb1_ref, sb2_ref)
        return c

    lax.fori_loop(0, nblk // (2 * pairs), body, 0)
    finish(lr, N_META, sa1_ref, sa2_ref)


def _attention(proj3, cos_t, sa_t, sb_t, neg_row, lam_row, sub_w, lr):
    b, l, _ = proj3.shape
    lp = lr + LANES
    const2 = lambda i, h: (0, 0)
    return pl.pallas_call(
        functools.partial(_attn_kernel, lr=lr),
        grid=(b, A_HEADS),
        in_specs=[
            pl.BlockSpec((1, l, LANES), lambda i, h: (i, 0, COL_AQ + h)),
            pl.BlockSpec((1, l, LANES), lambda i, h: (i, 0, COL_AK + h)),
            pl.BlockSpec((1, l, LANES), lambda i, h: (i, 0, COL_AV + h)),
            pl.BlockSpec((l, LANES), const2),
            pl.BlockSpec((l, LANES), const2),
            pl.BlockSpec((l, LANES), const2),
            pl.BlockSpec((1, LANES), const2),
            pl.BlockSpec((1, LANES), const2),
            pl.BlockSpec((1, LANES), const2),
        ],
        out_specs=pl.BlockSpec((1, l, LANES), lambda i, h: (i, 0, h)),
        out_shape=jax.ShapeDtypeStruct((b, l, A_HEADS * A_V_DIM), BF16),
        scratch_shapes=([pltpu.VMEM((lp, LANES), BF16)] * 3 + [pltpu.VMEM((LANES, lp), BF16)]
                        + [pltpu.VMEM((LANES, lp), F32)] * 4),
        compiler_params=pltpu.CompilerParams(
            dimension_semantics=("arbitrary", "arbitrary"), vmem_limit_bytes=VMEM_LIMIT),
        name="diff_attn",
    )(proj3, proj3, proj3, cos_t, sa_t, sb_t, neg_row, lam_row, sub_w)


def _log_sigmoid(x):
    return jnp.minimum(x, 0.0) - jnp.log(1.0 + jnp.exp(-jnp.abs(x)))


def _mlstm_kernel(q_ref, k_ref, v_ref, mo_ref, g_ref, gb_ref, nw_ref, o_ref,
                  c_ref, m_ref, kt_ref, h0_ref, u_ref, bcol_ref, amax_ref, aend_ref, bend_ref,
                  hf_ref, hb_ref, *, lr):
    l = lr + N_META
    t = CHUNK
    nc = lr // t
    dv = M_V_DIM
    scale = M_QK_DIM ** -0.5
    row = lax.broadcasted_iota(jnp.int32, (t, t), 0)
    col = lax.broadcasted_iota(jnp.int32, (t, t), 1)
    eye = row == col
    ones_tile = jnp.where(col == 0, 1.0, 0.0).astype(BF16)
    meta_r0 = l - t

    c_ref[...] = jnp.zeros(c_ref.shape, F32)
    m_ref[...] = jnp.zeros(m_ref.shape, F32)

    for c in range(nc + 1):
        r0 = c * t if c < nc else meta_r0
        kt_ref[:, c * t:(c + 1) * t] = k_ref[0, r0:r0 + t, :].astype(F32).T.astype(BF16)

    def scaled_q(r0):
        return (q_ref[0, pl.ds(r0, t), :].astype(F32) * scale).astype(BF16)

    def intra(ci, r0, g0):
        v_aug = jnp.concatenate([v_ref[0, pl.ds(r0, t), :], ones_tile], axis=1)
        kt = kt_ref[:, pl.ds(g0, t)]
        s_raw = _dot(scaled_q(r0), kt)
        for d, mask in ((0, col <= row), (1, col >= row)):
            gi = 2 * d
            ig = g_ref[0, 0, gi:gi + 1, pl.ds(g0, t)] + gb_ref[0, gi:gi + 1, :]
            lf = _log_sigmoid(g_ref[0, 0, gi + 1:gi + 2, pl.ds(g0, t)] + gb_ref[0, gi + 1:gi + 2, :])
            b_col = jnp.sum(jnp.where(mask, jnp.broadcast_to(lf, (t, t)), 0.0), axis=1, keepdims=True)
            b_row = jnp.sum(jnp.where(eye, jnp.broadcast_to(b_col, (t, t)), 0.0), axis=0, keepdims=True)
            a_row = ig - b_row
            dm = jnp.where(mask, jnp.broadcast_to(a_row, (t, t)), NEG_BIG)
            amax = jnp.max(dm, axis=1, keepdims=True)
            s0 = s_raw * jnp.exp(dm - amax)
            h0_ref[d, pl.ds(g0, t), :] = _dot(s0.astype(BF16), v_aug)
            a_end = jnp.max(a_row, axis=1, keepdims=True)
            kts = (kt.astype(F32) * jnp.exp(a_row - a_end)).astype(BF16)
            u_ref[d, ci] = _dot(kts, v_aug)
            bcol_ref[d, pl.ds(g0, t), :] = b_col
            amax_ref[d, pl.ds(g0, t), :] = amax
            aend_ref[d, ci] = jnp.broadcast_to(a_end, (8, LANES))
            bend_ref[d, ci] = jnp.broadcast_to(jnp.sum(lf, axis=1, keepdims=True), (8, LANES))

    ua = _pick(nc, (4, 2, 1))

    def abody(i, c):
        for u in range(ua):
            ci = ua * i + u
            r0 = pl.multiple_of(ci * t, t)
            intra(ci, r0, r0)
        return c

    lax.fori_loop(0, nc // ua, abody, 0)
    intra(nc, meta_r0, lr)

    def step(d, ci, r0, g0):
        m_prev = m_ref[d, 0:1, 0:1]
        amax = amax_ref[d, pl.ds(g0, t), :]
        m_col = jnp.maximum(m_prev, amax)
        num = (jnp.exp(m_prev - m_col) * _dot(scaled_q(r0), c_ref[d].astype(BF16))
               + jnp.exp(amax - m_col) * h0_ref[d, pl.ds(g0, t), :])
        den = jnp.maximum(jnp.abs(num[:, dv:dv + 1]), jnp.exp(-(bcol_ref[d, pl.ds(g0, t), :] + m_col)))
        a_end = aend_ref[d, ci, 0:1, 0:1]
        m_end = jnp.maximum(m_prev, a_end)
        c_ref[d] = jnp.exp(m_prev - m_end) * c_ref[d] + jnp.exp(a_end - m_end) * u_ref[d, ci]
        m_ref[d] = jnp.broadcast_to(bend_ref[d, ci, 0:1, 0:1] + m_end, (8, LANES))
        return num[:, :dv] * (1.0 / den)

    hf_ref[lr:l, :] = step(0, nc, meta_r0, lr)[t - N_META:, :]

    ub = _pick(nc, (2, 1))

    def bbody(i, c):
        for u in range(ub):
            jf = ub * i + u
            rf = pl.multiple_of(jf * t, t)
            hf_ref[pl.ds(rf, t), :] = step(0, jf, rf, rf)
            jb = nc - 1 - jf
            rb = pl.multiple_of(jb * t, t)
            hb_ref[pl.ds(rb, t), :] = step(1, jb, rb, rb)
        return c

    lax.fori_loop(0, nc // ub, bbody, 0)
    hb_ref[lr:l, :] = step(1, nc, meta_r0, lr)[t - N_META:, :]

    def finish(r0, rows):
        hs = hf_ref[pl.ds(r0, rows), :] + hb_ref[pl.ds(r0, rows), :]
        y = hs * lax.rsqrt(jnp.mean(hs * hs, axis=-1, keepdims=True) + EPS) * nw_ref[0]
        gate = jax.nn.sigmoid(mo_ref[0, pl.ds(r0, rows), :].astype(F32))
        o_ref[0, pl.ds(r0, rows), :] = (y * gate).astype(BF16)

    def fbody(i, c):
        for u in range(ua):
            finish(pl.multiple_of((ua * i + u) * t, t), t)
        return c

    lax.fori_loop(0, nc // ua, fbody, 0)
    finish(lr, N_META)


def _mlstm(proj3, gates_row, gate_bias, norm_w, lr):
    b, l, _ = proj3.shape
    gl = gates_row.shape[-1]
    nw = lr // CHUNK + 1
    aug = M_V_DIM + LANES
    return pl.pallas_call(
        functools.partial(_mlstm_kernel, lr=lr),
        grid=(b, M_HEADS),
        in_specs=[
            pl.BlockSpec((1, l, M_QK_DIM), lambda i, h: (i, 0, COL_MQ + h)),
            pl.BlockSpec((1, l, M_QK_DIM), lambda i, h: (i, 0, COL_MK + h)),
            pl.BlockSpec((1, l, M_V_DIM), lambda i, h: (i, 0, COL_MV + h)),
            pl.BlockSpec((1, l, M_V_DIM), lambda i, h: (i, 0, COL_MO + h)),
            pl.BlockSpec((1, 1, 8, gl), lambda i, h: (i, h, 0, 0)),
            pl.BlockSpec((1, 8, LANES), lambda i, h: (h, 0, 0)),
            pl.BlockSpec((1, 1, M_V_DIM), lambda i, h: (h, 0, 0)),
        ],
        out_specs=pl.BlockSpec((1, l, M_V_DIM), lambda i, h: (i, 0, h)),
        out_shape=jax.ShapeDtypeStruct((b, l, M_HEADS * M_V_DIM), BF16),
        scratch_shapes=[
            pltpu.VMEM((2, M_QK_DIM, aug), F32),
            pltpu.VMEM((2, 8, LANES), F32),
            pltpu.VMEM((M_QK_DIM, gl), BF16),
            pltpu.VMEM((2, gl, aug), F32),
            pltpu.VMEM((2, nw, M_QK_DIM, aug), F32),
            pltpu.VMEM((2, gl, 1), F32),
            pltpu.VMEM((2, gl, 1), F32),
            pltpu.VMEM((2, nw, 8, LANES), F32),
            pltpu.VMEM((2, nw, 8, LANES), F32),
            pltpu.VMEM((l, M_V_DIM), F32), pltpu.VMEM((l, M_V_DIM), F32),
        ],
        compiler_params=pltpu.CompilerParams(
            dimension_semantics=("arbitrary", "arbitrary"), vmem_limit_bytes=VMEM_LIMIT),
        name="mlstm",
    )(proj3, proj3, proj3, proj3, gates_row, gate_bias, norm_w)


def _merge_kernel(oa_ref, hm_ref, ga_ref, gm_ref, h_ref, wa_ref, wm_ref, wo_ref, n2_ref,
                  rw_ref, rb_ref, tri_ref,
                  h1_ref, xn_ref, te_ref, tw_ref, rk_ref, cnt_ref, carry_ref, *, tm):
    @pl.when(pl.program_id(0) == 0)
    def _():
        carry_ref[...] = jnp.zeros(carry_ref.shape, F32)

    pa = _dot(oa_ref[...], wa_ref[...])
    pm = _dot(hm_ref[...], wm_ref[...])
    merged = (jax.nn.sigmoid(ga_ref[...].astype(F32)) * pa
              + jax.nn.sigmoid(gm_ref[...].astype(F32)) * pm)
    h1 = h_ref[...] + _dot(merged.astype(BF16), wo_ref[...])
    h1_ref[...] = h1
    xn32 = h1 * lax.rsqrt(jnp.mean(h1 * h1, axis=-1, keepdims=True) + EPS) * n2_ref[...]
    xn = xn32.astype(BF16)
    xn_ref[...] = _pack_rows(xn32)

    logits = _nt(rw_ref[...], xn) + rb_ref[:, 0:1]
    eidx = lax.broadcasted_iota(jnp.int32, (N_EXPERTS, tm), 0)
    work = logits
    vals, hots = [], []
    for _ in range(TOP_K):
        mx = jnp.max(work, axis=0, keepdims=True)
        sel = jnp.min(jnp.where(work == mx, eidx, N_EXPERTS), axis=0, keepdims=True)
        hot = eidx == sel
        vals.append(mx)
        hots.append(hot)
        work = jnp.where(hot, -jnp.inf, work)
    ex = [jnp.exp(v - vals[0]) for v in vals]
    inv = 1.0 / (ex[0] + ex[1] + ex[2] + ex[3])
    chosen = jnp.where(hots[0] | hots[1] | hots[2] | hots[3], 1.0, 0.0)

    carry = carry_ref[:, 0:1]
    ranks = []
    for s in range(tm // LANES):
        sub = chosen[:, s * LANES:(s + 1) * LANES]
        ranks.append(_dot(sub.astype(BF16), tri_ref[...]) + carry)
        carry = carry + jnp.sum(sub, axis=1, keepdims=True)
    before = jnp.concatenate(ranks, axis=1) if len(ranks) > 1 else ranks[0]
    carry_ref[...] = jnp.broadcast_to(carry, carry_ref.shape)
    cnt_ref[...] = jnp.broadcast_to(carry, cnt_ref.shape)

    for kk in range(TOP_K):
        te_ref[kk:kk + 1, :] = jnp.sum(jnp.where(hots[kk], eidx, 0), axis=0, keepdims=True)
        tw_ref[kk:kk + 1, :] = ex[kk] * inv
        rk_ref[kk:kk + 1, :] = jnp.sum(jnp.where(hots[kk], before, 0.0), axis=0,
                                       keepdims=True).astype(jnp.int32)


def _merge(o_a, h_m, proj, h0, w_a, w_m, w_o, norm2_w, rw_t, rb_col, tri):
    n = h0.shape[0]
    tm = _pick(n, (384, 256, 128))
    row = lambda i: (i, 0)
    const = lambda i: (0, 0)
    tok = lambda i: (0, i)
    wspec = pl.BlockSpec((D_MODEL, D_MODEL), const)
    return pl.pallas_call(
        functools.partial(_merge_kernel, tm=tm),
        grid=(n // tm,),
        in_specs=[
            pl.BlockSpec((tm, D_MODEL), row),
            pl.BlockSpec((tm, D_MODEL), row),
            pl.BlockSpec((tm, D_MODEL), lambda i: (i, COL_GA)),
            pl.BlockSpec((tm, D_MODEL), lambda i: (i, COL_GM)),
            pl.BlockSpec((tm, D_MODEL), row),
            wspec, wspec, wspec,
            pl.BlockSpec((1, D_MODEL), const),
            pl.BlockSpec((N_EXPERTS, D_MODEL), const),
            pl.BlockSpec((N_EXPERTS, LANES), const),
            pl.BlockSpec((LANES, LANES), const),
        ],
        out_specs=[
            pl.BlockSpec((tm, D_MODEL), row),
            pl.BlockSpec((tm, ROW_WORDS), row),
            pl.BlockSpec((TOP_K, tm), tok),
            pl.BlockSpec((TOP_K, tm), tok),
            pl.BlockSpec((TOP_K, tm), tok),
            pl.BlockSpec((N_EXPERTS, LANES), const),
        ],
        out_shape=[
            jax.ShapeDtypeStruct((n, D_MODEL), F32),
            jax.ShapeDtypeStruct((n, ROW_WORDS), jnp.int32),
            jax.ShapeDtypeStruct((TOP_K, n), jnp.int32),
            jax.ShapeDtypeStruct((TOP_K, n), F32),
            jax.ShapeDtypeStruct((TOP_K, n), jnp.int32),
            jax.ShapeDtypeStruct((N_EXPERTS, LANES), F32),
        ],
        scratch_shapes=[pltpu.VMEM((N_EXPERTS, LANES), F32)],
        compiler_params=pltpu.CompilerParams(
            dimension_semantics=("arbitrary",), vmem_limit_bytes=VMEM_LIMIT),
        name="merge_router",
    )(o_a, h_m, proj, proj, h0, w_a, w_m, w_o, norm2_w, rw_t, rb_col, tri)


def _expert_kernel(be_ref, nu_ref, x_ref, wg_ref, bg_ref, wu_ref, bu_ref, wd_ref, bd_ref, y_ref,
                   wg_s, wu_s, wd_s):
    i = pl.program_id(0)
    used = i < nu_ref[0]

    @pl.when(jnp.logical_or(i == 0, be_ref[i] != be_ref[jnp.maximum(i - 1, 0)]))
    def _():
        wg_s[...] = wg_ref[0].astype(BF16)
        wu_s[...] = wu_ref[0].astype(BF16)
        wd_s[...] = wd_ref[0].astype(BF16)

    @pl.when(used)
    def _():
        x = _unpack_rows(x_ref[...]).astype(BF16)
        gt = jnp.minimum(_dot(x, wg_s[...]) + bg_ref[0], SWIGLU_LIMIT)
        up = jnp.clip(_dot(x, wu_s[...]) + bu_ref[0], -SWIGLU_LIMIT, SWIGLU_LIMIT)
        glu = gt * jax.nn.sigmoid(SWIGLU_ALPHA * gt)
        act = ((up + 1.0) * glu).astype(BF16)
        y_ref[...] = _pack_rows(_dot(act, wd_s[...]) + bd_ref[0])

    @pl.when(jnp.logical_not(used))
    def _():
        y_ref[...] = jnp.zeros(y_ref.shape, jnp.int32)


def _experts(blk_e, n_used, xs, w_g, b_g, w_u, b_u, w_d, b_d):
    cap = xs.shape[0]
    n_blocks = cap // MOE_ROWS
    wspec = lambda d0, d1: pl.BlockSpec((1, d0, d1), lambda i, be, nu: (be[i], 0, 0))
    return pl.pallas_call(
        _expert_kernel,
        grid_spec=pltpu.PrefetchScalarGridSpec(
            num_scalar_prefetch=2,
            grid=(n_blocks,),
            in_specs=[
                pl.BlockSpec((MOE_ROWS, ROW_WORDS), lambda i, be, nu: (i, 0)),
                wspec(D_MODEL, D_FF), wspec(1, D_FF),
                wspec(D_MODEL, D_FF), wspec(1, D_FF),
                wspec(D_FF, D_MODEL), wspec(1, D_MODEL),
            ],
            out_specs=pl.BlockSpec((MOE_ROWS, ROW_WORDS), lambda i, be, nu: (i, 0)),
            scratch_shapes=[pltpu.VMEM((D_MODEL, D_FF), BF16), pltpu.VMEM((D_MODEL, D_FF), BF16),
                            pltpu.VMEM((D_FF, D_MODEL), BF16)],
        ),
        out_shape=jax.ShapeDtypeStruct((cap, ROW_WORDS), jnp.int32),
        compiler_params=pltpu.CompilerParams(
            dimension_semantics=("arbitrary",), vmem_limit_bytes=VMEM_LIMIT),
        name="experts",
    )(blk_e, n_used, xs, w_g, b_g, w_u, b_u, w_d, b_d)


def _sc_worker_windows(n_win):
    return -(-n_win // (SC_CORES * SC_SUBCORES))


def _sc_dispatch(x_words, dest3, cap):
    n = x_words.shape[0]
    n_win = n // SC_WINDOW
    per = _sc_worker_windows(n_win)

    def body(x_hbm, d_hbm, o_hbm, idx_v, rows_v):
        wid = lax.axis_index("s") * SC_CORES + lax.axis_index("c")

        @pl.loop(0, per)
        def _(i):
            win = jnp.minimum(wid * per + i, n_win - 1)
            pltpu.sync_copy(d_hbm.at[win], idx_v)
            pltpu.sync_copy(x_hbm.at[pl.ds(win * SC_WINDOW, SC_WINDOW)], rows_v)
            for kk in range(TOP_K):
                pltpu.sync_copy(rows_v, o_hbm.at[idx_v.at[kk]])

    return pl.kernel(
        body,
        out_type=jax.ShapeDtypeStruct((cap, ROW_WORDS), jnp.int32),
        mesh=plsc.VectorSubcoreMesh(core_axis_name="c", subcore_axis_name="s"),
        scratch_types=[pltpu.VMEM((TOP_K, SC_WINDOW), jnp.int32),
                       pltpu.VMEM((SC_WINDOW, ROW_WORDS), jnp.int32)],
        name="moe_dispatch",
    )(x_words, dest3)


def _sc_gather(y_words, idx2):
    n_win = idx2.shape[0]
    per = _sc_worker_windows(n_win)

    def body(y_hbm, i_hbm, o_hbm, idx_v, rows_v, sem):
        wid = lax.axis_index("s") * SC_CORES + lax.axis_index("c")

        @pl.loop(0, per)
        def _(i):
            win = jnp.minimum(wid * per + i, n_win - 1)
            pltpu.sync_copy(i_hbm.at[win], idx_v)
            pltpu.async_copy(y_hbm.at[idx_v], rows_v, sem).wait()
            pltpu.sync_copy(rows_v, o_hbm.at[pl.ds(win * SC_WINDOW, SC_WINDOW)])

    return pl.kernel(
        body,
        out_type=jax.ShapeDtypeStruct((n_win * SC_WINDOW, ROW_WORDS), jnp.int32),
        mesh=plsc.VectorSubcoreMesh(core_axis_name="c", subcore_axis_name="s"),
        scratch_types=[pltpu.VMEM((SC_WINDOW,), jnp.int32),
                       pltpu.VMEM((SC_WINDOW, ROW_WORDS), jnp.int32),
                       pltpu.SemaphoreType.DMA],
        name="moe_gather",
    )(y_words, idx2)


def _combine_kernel(h_ref, y_ref, w_ref, fw_ref, o_ref):
    acc = h_ref[0]
    for kk in range(TOP_K):
        acc = acc + w_ref[0, :, kk:kk + 1] * _unpack_rows(y_ref[kk, 0])
    o_ref[0] = acc * lax.rsqrt(jnp.mean(acc * acc, axis=-1, keepdims=True) + EPS) * fw_ref[...]


def _combine(h1, yg, wts, final_w, b0, nb, lr):
    tr = _pick(lr, (512, 256, 128))
    return pl.pallas_call(
        _combine_kernel,
        grid=(nb, lr // tr),
        in_specs=[
            pl.BlockSpec((1, tr, D_MODEL), lambda i, j: (b0 + i, j, 0)),
            pl.BlockSpec((TOP_K, 1, tr, ROW_WORDS), lambda i, j: (0, b0 + i, j, 0)),
            pl.BlockSpec((1, tr, TOP_K), lambda i, j: (b0 + i, j, 0)),
            pl.BlockSpec((1, D_MODEL), lambda i, j: (0, 0)),
        ],
        out_specs=pl.BlockSpec((1, tr, D_MODEL), lambda i, j: (i, j, 0)),
        out_shape=jax.ShapeDtypeStruct((nb, lr, D_MODEL), F32),
        compiler_params=pltpu.CompilerParams(
            dimension_semantics=("arbitrary", "arbitrary"), vmem_limit_bytes=VMEM_LIMIT),
        name="combine",
    )(h1, yg, wts, final_w)


def _rope_tables(lr):
    l = lr + N_META
    half = ROT_DIM // 2
    pos = jnp.concatenate([jnp.arange(N_META, l, dtype=F32), jnp.arange(N_META, dtype=F32)])
    inv_freq = ROPE_THETA ** (-jnp.arange(0, ROT_DIM, 2, dtype=F32) / ROT_DIM)
    ang = pos[:, None] * inv_freq[None, :]
    cos, sin = jnp.cos(ang), jnp.sin(ang)
    z = jnp.zeros((l, A_QK_DIM - ROT_DIM), F32)
    zh = jnp.zeros((l, half), F32)
    cos64 = jnp.concatenate([cos, cos, z + 1.0], axis=1)
    sa64 = jnp.concatenate([-sin, zh, z], axis=1)
    sb64 = jnp.concatenate([zh, sin, z], axis=1)
    dup = lambda a: jnp.concatenate([a, a], axis=1)
    return dup(cos64), dup(sa64), dup(sb64)


def kernel(x_prompt, x_sample, meta_tokens, norm1_w, w_in, lambda_q1, lambda_k1, lambda_q2, lambda_k2, attn_subln_w, mlstm_gate_b, mlstm_norm_w, w_br_attn, w_br_mlstm, w_out, norm2_w, router_w, router_b, w_gate, b_gate, w_up, b_up, w_down, b_down, final_norm_w):
    bp, lr, _ = x_prompt.shape
    bs = x_sample.shape[0]
    b = bp + bs
    l = lr + N_META
    n = b * l

    w_in0 = w_in[0]
    g0 = 3 * 1024 + 2 * 512 + 2 * 1024
    w_main = jnp.concatenate([w_in0[:, :g0], w_in0[:, g0 + 4 * M_HEADS:]], axis=1).astype(BF16)
    w_gates = jnp.pad(w_in0[:, g0:g0 + 4 * M_HEADS], ((0, 0), (0, LANES - 4 * M_HEADS))).astype(BF16)
    lam = (jnp.exp(jnp.sum(lambda_q1[0] * lambda_k1[0])) - jnp.exp(jnp.sum(lambda_q2[0] * lambda_k2[0]))
           + LAM_INIT)
    lam_row = jnp.full((1, LANES), lam, F32)
    sub_w = (attn_subln_w[0] * (1.0 - LAM_INIT)).reshape(1, A_V_DIM)
    neg_row = jnp.where(jnp.arange(LANES) < N_META, 0.0, NEG_BIG).astype(F32).reshape(1, LANES)
    cos_t, sa_t, sb_t = _rope_tables(lr)
    gate_bias = jnp.broadcast_to(
        jnp.pad(mlstm_gate_b[0].T, ((0, 0), (0, 4)))[:, :, None], (M_HEADS, 8, LANES)).astype(F32)
    norm_w_m = mlstm_norm_w[0].reshape(M_HEADS, 1, M_V_DIM)
    rw_t = router_w[0].T.astype(BF16)
    rb_col = jnp.broadcast_to(router_b[0][:, None], (N_EXPERTS, LANES)).astype(F32)
    tri = (jnp.arange(LANES)[:, None] < jnp.arange(LANES)[None, :]).astype(BF16)

    x_all = jnp.concatenate([x_prompt, x_sample], axis=0)
    meta_b = jnp.broadcast_to(meta_tokens[None].astype(x_all.dtype), (b, N_META, D_MODEL))
    h0 = jnp.concatenate([x_all, meta_b], axis=1).reshape(n, D_MODEL)

    proj, gates = _inproj(h0, norm1_w[0].reshape(1, D_MODEL), w_main, w_gates)
    proj3 = proj.reshape(b, l, N_MAIN)

    o_a = _attention(proj3, cos_t, sa_t, sb_t, neg_row, lam_row, sub_w, lr)

    g4 = gates.reshape(b, l, 4, M_HEADS).transpose(0, 3, 2, 1)
    off = jnp.array([NEG_BIG, 1e4, NEG_BIG, 1e4], F32)[None, None, :, None]
    fill = jnp.broadcast_to(off, (b, M_HEADS, 4, CHUNK - N_META))
    g_row = jnp.concatenate([g4[..., :lr], fill, g4[..., lr:]], axis=-1)
    g_row = jnp.pad(g_row, ((0, 0), (0, 0), (0, 4), (0, 0)))
    h_m = _mlstm(proj3, g_row, gate_bias, norm_w_m, lr)

    h1, xn2, top_e, top_w, rank, cnt = _merge(
        o_a.reshape(n, D_MODEL), h_m.reshape(n, D_MODEL), proj, h0,
        w_br_attn[0].astype(BF16), w_br_mlstm[0].astype(BF16), w_out[0].astype(BF16),
        norm2_w[0].reshape(1, D_MODEL), rw_t, rb_col, tri)

    counts = cnt[:, 0].astype(jnp.int32)
    padded = (counts + MOE_ROWS - 1) // MOE_ROWS * MOE_ROWS
    pends = jnp.cumsum(padded)
    pstarts = pends - padded
    e_ids = jnp.arange(N_EXPERTS, dtype=jnp.int32)
    pstart_of = jnp.sum(jnp.where(top_e[..., None] == e_ids, pstarts, 0), axis=-1)
    dest = pstart_of + rank
    n_blocks = (n * TOP_K + N_EXPERTS * (MOE_ROWS - 1) + MOE_ROWS - 1) // MOE_ROWS
    cap = n_blocks * MOE_ROWS
    blk_start = jnp.arange(n_blocks, dtype=jnp.int32) * MOE_ROWS
    blk_e = jnp.minimum(jnp.sum((pends[None, :] <= blk_start[:, None]).astype(jnp.int32), axis=1),
                        N_EXPERTS - 1)
    n_used = (pends[-1:] // MOE_ROWS).astype(jnp.int32)

    dest_win = dest.reshape(TOP_K, n // SC_WINDOW, SC_WINDOW).transpose(1, 0, 2)
    xs = _sc_dispatch(xn2, dest_win, cap)
    y = _experts(blk_e, n_used, xs,
                 w_gate[0], b_gate[0].reshape(N_EXPERTS, 1, D_FF),
                 w_up[0], b_up[0].reshape(N_EXPERTS, 1, D_FF),
                 w_down[0], b_down[0].reshape(N_EXPERTS, 1, D_MODEL))
    yg = _sc_gather(y, dest.reshape(TOP_K * n // SC_WINDOW, SC_WINDOW)).reshape(TOP_K, b, l, ROW_WORDS)
    wts = top_w.T.reshape(b, l, TOP_K)
    h1_3 = h1.reshape(b, l, D_MODEL)
    fw = final_norm_w.reshape(1, D_MODEL)
    y_prompt = _combine(h1_3, yg, wts, fw, 0, bp, lr)
    y_sample = _combine(h1_3, yg, wts, fw, bp, bs, lr)
    return (y_prompt, y_sample)
```

```python
import functools
import math

import jax
import jax.numpy as jnp
import numpy as np
from jax import lax
from jax.experimental import pallas as pl
from jax.experimental.pallas import tpu as pltpu
from jax.experimental.pallas import tpu_sc as plsc

F32 = jnp.float32
BF16 = jnp.bfloat16

D_MODEL = 1024
N_META = 16
EPS = 1e-6
A_HEADS = 8
A_QK_DIM = 64
A_V_DIM = 128
ROT_DIM = 16
ROPE_THETA = 500000.0
M_HEADS = 4
M_QK_DIM = 128
M_V_DIM = 256
CHUNK = 128
N_EXPERTS = 32
TOP_K = 4
D_FF = 1024
SWIGLU_LIMIT = 7.0
SWIGLU_ALPHA = 1.702
LAM_INIT = 0.8 - 0.6 * math.exp(-0.3 * 0)

LANES = 128
NEG_BIG = -1e30
MOE_ROWS = 512
SC_CORES = 2
SC_SUBCORES = 16
SC_WINDOW = 128
ROW_WORDS = D_MODEL // 2
COL_AQ, COL_AK, COL_AV = 0, 8, 16
COL_MQ, COL_MK = 24, 28
COL_MV, COL_MO = 16, 20
COL_GA, COL_GM = 6, 7
N_MAIN = 8192

VMEM_LIMIT = 56 * 1024 * 1024


def _pick(n, prefs):
    for p in prefs:
        if n % p == 0:
            return p
    raise ValueError(f"no tile for {n}")


def _nt(a, b):
    return lax.dot_general(a, b, (((1,), (1,)), ((), ())), preferred_element_type=F32)


def _dot(a, b):
    return jnp.dot(a, b, preferred_element_type=F32)


def _pack_rows(x):
    w = x.shape[1] // 2
    bits = lambda v: lax.bitcast_convert_type(v.astype(BF16).astype(F32), jnp.uint32)
    lo = lax.shift_right_logical(bits(x[:, :w]), jnp.uint32(16))
    hi = bits(x[:, w:]) & jnp.uint32(0xFFFF0000)
    return lax.bitcast_convert_type(lo | hi, jnp.int32)


def _unpack_rows(wds):
    u = lax.bitcast_convert_type(wds, jnp.uint32)
    lo = lax.bitcast_convert_type(lax.shift_left(u, jnp.uint32(16)), F32)
    hi = lax.bitcast_convert_type(u & jnp.uint32(0xFFFF0000), F32)
    return jnp.concatenate([lo, hi], axis=1)


def _inproj_kernel(x_ref, n1_ref, w_ref, wg_ref, o_ref, g_ref, xn_ref):
    @pl.when(pl.program_id(1) == 0)
    def _():
        x = x_ref[...]
        ms = jnp.mean(x * x, axis=-1, keepdims=True)
        xn = (x * lax.rsqrt(ms + EPS) * n1_ref[...]).astype(BF16)
        xn_ref[...] = xn
        g_ref[...] = _dot(xn, wg_ref[...])[:, :4 * M_HEADS]

    o_ref[...] = _dot(xn_ref[...], w_ref[...]).astype(BF16)


def _inproj(h0, norm1_w, w_main, w_gates):
    n = h0.shape[0]
    tm = _pick(n, (1152, 768, 512, 384, 256, 128))
    tn = 2048
    return pl.pallas_call(
        _inproj_kernel,
        grid=(n // tm, N_MAIN // tn),
        in_specs=[
            pl.BlockSpec((tm, D_MODEL), lambda i, j: (i, 0)),
            pl.BlockSpec((1, D_MODEL), lambda i, j: (0, 0)),
            pl.BlockSpec((D_MODEL, tn), lambda i, j: (0, j)),
            pl.BlockSpec((D_MODEL, LANES), lambda i, j: (0, 0)),
        ],
        out_specs=[
            pl.BlockSpec((tm, tn), lambda i, j: (i, j)),
            pl.BlockSpec((tm, 4 * M_HEADS), lambda i, j: (i, 0)),
        ],
        out_shape=[
            jax.ShapeDtypeStruct((n, N_MAIN), BF16),
            jax.ShapeDtypeStruct((n, 4 * M_HEADS), F32),
        ],
        scratch_shapes=[pltpu.VMEM((tm, D_MODEL), BF16)],
        compiler_params=pltpu.CompilerParams(
            dimension_semantics=("arbitrary", "arbitrary"), vmem_limit_bytes=VMEM_LIMIT),
        name="inproj",
    )(h0, norm1_w, w_main, w_gates)


def _attn_kernel(q_ref, k_ref, v_ref, cos_ref, sa_ref, sb_ref, neg_ref, lam_ref, sw_ref,
                 o_ref, q1_ref, q2_ref, vs_ref, kt_ref, sa1_ref, sa2_ref, sb1_ref, sb2_ref, *, lr):
    l = lr + N_META
    lp = lr + LANES
    nblk = lr // LANES
    scale = A_QK_DIM ** -0.5 * math.log2(math.e)

    def rope(t):
        return (t * cos_ref[...] + pltpu.roll(t, LANES - ROT_DIM // 2, 1) * sa_ref[...]
                + pltpu.roll(t, ROT_DIM // 2, 1) * sb_ref[...])

    lane = lax.broadcasted_iota(jnp.int32, (l, LANES), 1)
    qr = rope(q_ref[0].astype(F32)) * scale
    q1_ref[0:l, :] = jnp.where(lane < A_QK_DIM, qr, 0.0).astype(BF16)
    q2_ref[0:l, :] = jnp.where(lane >= A_QK_DIM, qr, 0.0).astype(BF16)
    vs_ref[0:l, :] = v_ref[0]
    pad = jnp.zeros((lp - l, LANES), BF16)
    q1_ref[l:lp, :] = pad
    q2_ref[l:lp, :] = pad
    vs_ref[l:lp, :] = pad
    kr = rope(k_ref[0].astype(F32))
    for c in range(nblk):
        kt_ref[:, c * LANES:(c + 1) * LANES] = kr[c * LANES:(c + 1) * LANES, :].T.astype(BF16)
    k_tail = jnp.concatenate([kr[lr:l, :], jnp.zeros((lp - l, LANES), F32)], axis=0)
    kt_ref[:, lr:lp] = k_tail.T.astype(BF16)

    lam = lam_ref[:, 0:1]

    def scores(r0, s1_ref, s2_ref):
        rows = pl.ds(r0, LANES)
        k_all = kt_ref[...]
        s1_ref[...] = _dot(q1_ref[rows, :], k_all)
        s2_ref[...] = _dot(q2_ref[rows, :], k_all)

    def softmax_parts(s_ref, nrows):
        sm = s_ref[0:nrows, :lr]
        sl = s_ref[0:nrows, lr:] + neg_ref[...]
        m = jnp.maximum(jnp.max(sm, axis=1, keepdims=True), jnp.max(sl, axis=1, keepdims=True))
        pm = jnp.exp2(sm - m)
        pl_ = jnp.exp2(sl - m)
        tot = jnp.sum(pm, axis=1, keepdims=True) + jnp.sum(pl_, axis=1, keepdims=True)
        return pm, pl_, tot

    def finish(r0, nrows, s1_ref, s2_ref):
        p1m, p1l, t1 = softmax_parts(s1_ref, nrows)
        p2m, p2l, t2 = softmax_parts(s2_ref, nrows)
        c = lam * t1 / t2
        am = (p1m - p2m * c).astype(BF16)
        al = (p1l - p2l * c).astype(BF16)
        o = (_dot(am, vs_ref[0:lr, :]) + _dot(al, vs_ref[lr:lp, :])) * (1.0 / t1)
        o = o * lax.rsqrt(jnp.mean(o * o, axis=-1, keepdims=True) + EPS) * sw_ref[...]
        o_ref[0, pl.ds(r0, nrows), :] = o.astype(BF16)

    scores(0, sa1_ref, sa2_ref)

    pairs = _pick(nblk // 2, (8, 4, 2, 1))

    def body(j, c):
        for u in range(pairs):
            ra = pl.multiple_of((j * pairs + u) * (2 * LANES), LANES)
            rb = ra + LANES
            scores(rb, sb1_ref, sb2_ref)
            finish(ra, LANES, sa1_ref, sa2_ref)
            scores(rb + LANES, sa1_ref, sa2_ref)
            finish(rb, LANES, sb1_ref, sb2_ref)
        return c

    lax.fori_loop(0, nblk // (2 * pairs), body, 0)
    finish(lr, N_META, sa1_ref, sa2_ref)


def _attention(proj3, cos_t, sa_t, sb_t, neg_row, lam_row, sub_w, lr):
    b, l, _ = proj3.shape
    lp = lr + LANES
    const2 = lambda i, h: (0, 0)
    return pl.pallas_call(
        functools.partial(_attn_kernel, lr=lr),
        grid=(b, A_HEADS),
        in_specs=[
            pl.BlockSpec((1, l, LANES), lambda i, h: (i, 0, COL_AQ + h)),
            pl.BlockSpec((1, l, LANES), lambda i, h: (i, 0, COL_AK + h)),
            pl.BlockSpec((1, l, LANES), lambda i, h: (i, 0, COL_AV + h)),
            pl.BlockSpec((l, LANES), const2),
            pl.BlockSpec((l, LANES), const2),
            pl.BlockSpec((l, LANES), const2),
            pl.BlockSpec((1, LANES), const2),
            pl.BlockSpec((1, LANES), const2),
            pl.BlockSpec((1, LANES), const2),
        ],
        out_specs=pl.BlockSpec((1, l, LANES), lambda i, h: (i, 0, h)),
        out_shape=jax.ShapeDtypeStruct((b, l, A_HEADS * A_V_DIM), BF16),
        scratch_shapes=([pltpu.VMEM((lp, LANES), BF16)] * 3 + [pltpu.VMEM((LANES, lp), BF16)]
                        + [pltpu.VMEM((LANES, lp), F32)] * 4),
        compiler_params=pltpu.CompilerParams(
            dimension_semantics=("arbitrary", "arbitrary"), vmem_limit_bytes=VMEM_LIMIT),
        name="diff_attn",
    )(proj3, proj3, proj3, cos_t, sa_t, sb_t, neg_row, lam_row, sub_w)


def _log_sigmoid(x):
    return jnp.minimum(x, 0.0) - jnp.log(1.0 + jnp.exp(-jnp.abs(x)))


def _mlstm_kernel(q_ref, k_ref, v_ref, mo_ref, g_ref, gb_ref, nw_ref, o_ref,
                  c_ref, m_ref, kt_ref, h0_ref, u_ref, bcol_ref, amax_ref, aend_ref, bend_ref,
                  hf_ref, hb_ref, *, lr):
    l = lr + N_META
    t = CHUNK
    nc = lr // t
    dv = M_V_DIM
    scale = M_QK_DIM ** -0.5
    row = lax.broadcasted_iota(jnp.int32, (t, t), 0)
    col = lax.broadcasted_iota(jnp.int32, (t, t), 1)
    eye = row == col
    ones_tile = jnp.where(col == 0, 1.0, 0.0).astype(BF16)
    meta_r0 = l - t

    c_ref[...] = jnp.zeros(c_ref.shape, F32)
    m_ref[...] = jnp.zeros(m_ref.shape, F32)

    for c in range(nc + 1):
        r0 = c * t if c < nc else meta_r0
        kt_ref[:, c * t:(c + 1) * t] = k_ref[0, r0:r0 + t, :].astype(F32).T.astype(BF16)

    def scaled_q(r0):
        return (q_ref[0, pl.ds(r0, t), :].astype(F32) * scale).astype(BF16)

    def intra(ci, r0, g0):
        v_aug = jnp.concatenate([v_ref[0, pl.ds(r0, t), :], ones_tile], axis=1)
        kt = kt_ref[:, pl.ds(g0, t)]
        s_raw = _dot(scaled_q(r0), kt)
        for d, mask in ((0, col <= row), (1, col >= row)):
            gi = 2 * d
            ig = g_ref[0, 0, gi:gi + 1, pl.ds(g0, t)] + gb_ref[0, gi:gi + 1, :]
            lf = _log_sigmoid(g_ref[0, 0, gi + 1:gi + 2, pl.ds(g0, t)] + gb_ref[0, gi + 1:gi + 2, :])
            b_col = jnp.sum(jnp.where(mask, jnp.broadcast_to(lf, (t, t)), 0.0), axis=1, keepdims=True)
            b_row = jnp.sum(jnp.where(eye, jnp.broadcast_to(b_col, (t, t)), 0.0), axis=0, keepdims=True)
            a_row = ig - b_row
            dm = jnp.where(mask, jnp.broadcast_to(a_row, (t, t)), NEG_BIG)
            amax = jnp.max(dm, axis=1, keepdims=True)
            s0 = s_raw * jnp.exp(dm - amax)
            h0_ref[d, pl.ds(g0, t), :] = _dot(s0.astype(BF16), v_aug)
            a_end = jnp.max(a_row, axis=1, keepdims=True)
            kts = (kt.astype(F32) * jnp.exp(a_row - a_end)).astype(BF16)
            u_ref[d, ci] = _dot(kts, v_aug)
            bcol_ref[d, pl.ds(g0, t), :] = b_col
            amax_ref[d, pl.ds(g0, t), :] = amax
            aend_ref[d, ci] = jnp.broadcast_to(a_end, (8, LANES))
            bend_ref[d, ci] = jnp.broadcast_to(jnp.sum(lf, axis=1, keepdims=True), (8, LANES))

    ua = _pick(nc, (8, 4, 2, 1))

    def abody(i, c):
        for u in range(ua):
            ci = ua * i + u
            r0 = pl.multiple_of(ci * t, t)
            intra(ci, r0, r0)
        return c

    lax.fori_loop(0, nc // ua, abody, 0)
    intra(nc, meta_r0, lr)

    def step(d, ci, r0, g0):
        m_prev = m_ref[d, 0:1, 0:1]
        amax = amax_ref[d, pl.ds(g0, t), :]
        m_col = jnp.maximum(m_prev, amax)
        num = (jnp.exp(m_prev - m_col) * _dot(scaled_q(r0), c_ref[d].astype(BF16))
               + jnp.exp(amax - m_col) * h0_ref[d, pl.ds(g0, t), :])
        den = jnp.maximum(jnp.abs(num[:, dv:dv + 1]), jnp.exp(-(bcol_ref[d, pl.ds(g0, t), :] + m_col)))
        a_end = aend_ref[d, ci, 0:1, 0:1]
        m_end = jnp.maximum(m_prev, a_end)
        c_ref[d] = jnp.exp(m_prev - m_end) * c_ref[d] + jnp.exp(a_end - m_end) * u_ref[d, ci]
        m_ref[d] = jnp.broadcast_to(bend_ref[d, ci, 0:1, 0:1] + m_end, (8, LANES))
        return num[:, :dv] * (1.0 / den)

    hf_ref[lr:l, :] = step(0, nc, meta_r0, lr)[t - N_META:, :]

    ub = _pick(nc, (2, 1))

    def bbody(i, c):
        for u in range(ub):
            jf = ub * i + u
            rf = pl.multiple_of(jf * t, t)
            hf_ref[pl.ds(rf, t), :] = step(0, jf, rf, rf)
            jb = nc - 1 - jf
            rb = pl.multiple_of(jb * t, t)
            hb_ref[pl.ds(rb, t), :] = step(1, jb, rb, rb)
        return c

    lax.fori_loop(0, nc // ub, bbody, 0)
    hb_ref[lr:l, :] = step(1, nc, meta_r0, lr)[t - N_META:, :]

    def finish(r0, rows):
        hs = hf_ref[pl.ds(r0, rows), :] + hb_ref[pl.ds(r0, rows), :]
        y = hs * lax.rsqrt(jnp.mean(hs * hs, axis=-1, keepdims=True) + EPS) * nw_ref[0]
        gate = jax.nn.sigmoid(mo_ref[0, pl.ds(r0, rows), :].astype(F32))
        o_ref[0, pl.ds(r0, rows), :] = (y * gate).astype(BF16)

    def fbody(i, c):
        for u in range(ua):
            finish(pl.multiple_of((ua * i + u) * t, t), t)
        return c

    lax.fori_loop(0, nc // ua, fbody, 0)
    finish(lr, N_META)


def _mlstm(proj3, gates_row, gate_bias, norm_w, lr):
    b, l, _ = proj3.shape
    gl = gates_row.shape[-1]
    nw = lr // CHUNK + 1
    aug = M_V_DIM + LANES
    return pl.pallas_call(
        functools.partial(_mlstm_kernel, lr=lr),
        grid=(b, M_HEADS),
        in_specs=[
            pl.BlockSpec((1, l, M_QK_DIM), lambda i, h: (i, 0, COL_MQ + h)),
            pl.BlockSpec((1, l, M_QK_DIM), lambda i, h: (i, 0, COL_MK + h)),
            pl.BlockSpec((1, l, M_V_DIM), lambda i, h: (i, 0, COL_MV + h)),
            pl.BlockSpec((1, l, M_V_DIM), lambda i, h: (i, 0, COL_MO + h)),
            pl.BlockSpec((1, 1, 8, gl), lambda i, h: (i, h, 0, 0)),
            pl.BlockSpec((1, 8, LANES), lambda i, h: (h, 0, 0)),
            pl.BlockSpec((1, 1, M_V_DIM), lambda i, h: (h, 0, 0)),
        ],
        out_specs=pl.BlockSpec((1, l, M_V_DIM), lambda i, h: (i, 0, h)),
        out_shape=jax.ShapeDtypeStruct((b, l, M_HEADS * M_V_DIM), BF16),
        scratch_shapes=[
            pltpu.VMEM((2, M_QK_DIM, aug), F32),
            pltpu.VMEM((2, 8, LANES), F32),
            pltpu.VMEM((M_QK_DIM, gl), BF16),
            pltpu.VMEM((2, gl, aug), F32),
            pltpu.VMEM((2, nw, M_QK_DIM, aug), F32),
            pltpu.VMEM((2, gl, 1), F32),
            pltpu.VMEM((2, gl, 1), F32),
            pltpu.VMEM((2, nw, 8, LANES), F32),
            pltpu.VMEM((2, nw, 8, LANES), F32),
            pltpu.VMEM((l, M_V_DIM), F32), pltpu.VMEM((l, M_V_DIM), F32),
        ],
        compiler_params=pltpu.CompilerParams(
            dimension_semantics=("arbitrary", "arbitrary"), vmem_limit_bytes=VMEM_LIMIT),
        name="mlstm",
    )(proj3, proj3, proj3, proj3, gates_row, gate_bias, norm_w)


def _merge_kernel(oa_ref, hm_ref, ga_ref, gm_ref, h_ref, wa_ref, wm_ref, wo_ref, n2_ref,
                  rw_ref, rb_ref, tri_ref,
                  h1_ref, xn_ref, te_ref, tw_ref, rk_ref, cnt_ref, carry_ref, *, tm):
    @pl.when(pl.program_id(0) == 0)
    def _():
        carry_ref[...] = jnp.zeros(carry_ref.shape, F32)

    pa = _dot(oa_ref[...], wa_ref[...])
    pm = _dot(hm_ref[...], wm_ref[...])
    merged = (jax.nn.sigmoid(ga_ref[...].astype(F32)) * pa
              + jax.nn.sigmoid(gm_ref[...].astype(F32)) * pm)
    h1 = h_ref[...] + _dot(merged.astype(BF16), wo_ref[...])
    h1_ref[...] = h1
    xn32 = h1 * lax.rsqrt(jnp.mean(h1 * h1, axis=-1, keepdims=True) + EPS) * n2_ref[...]
    xn = xn32.astype(BF16)
    xn_ref[...] = _pack_rows(xn32)

    logits = _nt(rw_ref[...], xn) + rb_ref[:, 0:1]
    eidx = lax.broadcasted_iota(jnp.int32, (N_EXPERTS, tm), 0)
    work = logits
    vals, hots = [], []
    for _ in range(TOP_K):
        mx = jnp.max(work, axis=0, keepdims=True)
        sel = jnp.min(jnp.where(work == mx, eidx, N_EXPERTS), axis=0, keepdims=True)
        hot = eidx == sel
        vals.append(mx)
        hots.append(hot)
        work = jnp.where(hot, -jnp.inf, work)
    ex = [jnp.exp(v - vals[0]) for v in vals]
    inv = 1.0 / (ex[0] + ex[1] + ex[2] + ex[3])
    chosen = jnp.where(hots[0] | hots[1] | hots[2] | hots[3], 1.0, 0.0)

    carry = carry_ref[:, 0:1]
    ranks = []
    for s in range(tm // LANES):
        sub = chosen[:, s * LANES:(s + 1) * LANES]
        ranks.append(_dot(sub.astype(BF16), tri_ref[...]) + carry)
        carry = carry + jnp.sum(sub, axis=1, keepdims=True)
    before = jnp.concatenate(ranks, axis=1) if len(ranks) > 1 else ranks[0]
    carry_ref[...] = jnp.broadcast_to(carry, carry_ref.shape)
    cnt_ref[...] = jnp.broadcast_to(carry, cnt_ref.shape)

    for kk in range(TOP_K):
        te_ref[kk:kk + 1, :] = jnp.sum(jnp.where(hots[kk], eidx, 0), axis=0, keepdims=True)
        tw_ref[kk:kk + 1, :] = ex[kk] * inv
        rk_ref[kk:kk + 1, :] = jnp.sum(jnp.where(hots[kk], before, 0.0), axis=0,
                                       keepdims=True).astype(jnp.int32)


def _merge(o_a, h_m, proj, h0, w_a, w_m, w_o, norm2_w, rw_t, rb_col, tri):
    n = h0.shape[0]
    tm = _pick(n, (384, 256, 128))
    row = lambda i: (i, 0)
    const = lambda i: (0, 0)
    tok = lambda i: (0, i)
    wspec = pl.BlockSpec((D_MODEL, D_MODEL), const)
    return pl.pallas_call(
        functools.partial(_merge_kernel, tm=tm),
        grid=(n // tm,),
        in_specs=[
            pl.BlockSpec((tm, D_MODEL), row),
            pl.BlockSpec((tm, D_MODEL), row),
            pl.BlockSpec((tm, D_MODEL), lambda i: (i, COL_GA)),
            pl.BlockSpec((tm, D_MODEL), lambda i: (i, COL_GM)),
            pl.BlockSpec((tm, D_MODEL), row),
            wspec, wspec, wspec,
            pl.BlockSpec((1, D_MODEL), const),
            pl.BlockSpec((N_EXPERTS, D_MODEL), const),
            pl.BlockSpec((N_EXPERTS, LANES), const),
            pl.BlockSpec((LANES, LANES), const),
        ],
        out_specs=[
            pl.BlockSpec((tm, D_MODEL), row),
            pl.BlockSpec((tm, ROW_WORDS), row),
            pl.BlockSpec((TOP_K, tm), tok),
            pl.BlockSpec((TOP_K, tm), tok),
            pl.BlockSpec((TOP_K, tm), tok),
            pl.BlockSpec((N_EXPERTS, LANES), const),
        ],
        out_shape=[
            jax.ShapeDtypeStruct((n, D_MODEL), F32),
            jax.ShapeDtypeStruct((n, ROW_WORDS), jnp.int32),
            jax.ShapeDtypeStruct((TOP_K, n), jnp.int32),
            jax.ShapeDtypeStruct((TOP_K, n), F32),
            jax.ShapeDtypeStruct((TOP_K, n), jnp.int32),
            jax.ShapeDtypeStruct((N_EXPERTS, LANES), F32),
        ],
        scratch_shapes=[pltpu.VMEM((N_EXPERTS, LANES), F32)],
        compiler_params=pltpu.CompilerParams(
            dimension_semantics=("arbitrary",), vmem_limit_bytes=VMEM_LIMIT),
        name="merge_router",
    )(o_a, h_m, proj, proj, h0, w_a, w_m, w_o, norm2_w, rw_t, rb_col, tri)


def _expert_kernel(be_ref, nu_ref, x_ref, wg_ref, bg_ref, wu_ref, bu_ref, wd_ref, bd_ref, y_ref,
                   wg_s, wu_s, wd_s):
    i = pl.program_id(0)
    used = i < nu_ref[0]

    @pl.when(jnp.logical_or(i == 0, be_ref[i] != be_ref[jnp.maximum(i - 1, 0)]))
    def _():
        wg_s[...] = wg_ref[0].astype(BF16)
        wu_s[...] = wu_ref[0].astype(BF16)
        wd_s[...] = wd_ref[0].astype(BF16)

    @pl.when(used)
    def _():
        x = _unpack_rows(x_ref[...]).astype(BF16)
        gt = jnp.minimum(_dot(x, wg_s[...]) + bg_ref[0], SWIGLU_LIMIT)
        up = jnp.clip(_dot(x, wu_s[...]) + bu_ref[0], -SWIGLU_LIMIT, SWIGLU_LIMIT)
        glu = gt * jax.nn.sigmoid(SWIGLU_ALPHA * gt)
        act = ((up + 1.0) * glu).astype(BF16)
        y_ref[...] = _pack_rows(_dot(act, wd_s[...]) + bd_ref[0])

    @pl.when(jnp.logical_not(used))
    def _():
        y_ref[...] = jnp.zeros(y_ref.shape, jnp.int32)


def _experts(blk_e, n_used, xs, w_g, b_g, w_u, b_u, w_d, b_d):
    cap = xs.shape[0]
    n_blocks = cap // MOE_ROWS
    wspec = lambda d0, d1: pl.BlockSpec((1, d0, d1), lambda i, be, nu: (be[i], 0, 0))
    return pl.pallas_call(
        _expert_kernel,
        grid_spec=pltpu.PrefetchScalarGridSpec(
            num_scalar_prefetch=2,
            grid=(n_blocks,),
            in_specs=[
                pl.BlockSpec((MOE_ROWS, ROW_WORDS), lambda i, be, nu: (i, 0)),
                wspec(D_MODEL, D_FF), wspec(1, D_FF),
                wspec(D_MODEL, D_FF), wspec(1, D_FF),
                wspec(D_FF, D_MODEL), wspec(1, D_MODEL),
            ],
            out_specs=pl.BlockSpec((MOE_ROWS, ROW_WORDS), lambda i, be, nu: (i, 0)),
            scratch_shapes=[pltpu.VMEM((D_MODEL, D_FF), BF16), pltpu.VMEM((D_MODEL, D_FF), BF16),
                            pltpu.VMEM((D_FF, D_MODEL), BF16)],
        ),
        out_shape=jax.ShapeDtypeStruct((cap, ROW_WORDS), jnp.int32),
        compiler_params=pltpu.CompilerParams(
            dimension_semantics=("arbitrary",), vmem_limit_bytes=VMEM_LIMIT),
        name="experts",
    )(blk_e, n_used, xs, w_g, b_g, w_u, b_u, w_d, b_d)


def _sc_worker_windows(n_win):
    return -(-n_win // (SC_CORES * SC_SUBCORES))


def _sc_dispatch(x_words, dest3, cap):
    n = x_words.shape[0]
    n_win = n // SC_WINDOW
    per = _sc_worker_windows(n_win)

    def body(x_hbm, d_hbm, o_hbm, idx_v, rows_v):
        wid = lax.axis_index("s") * SC_CORES + lax.axis_index("c")

        @pl.loop(0, per)
        def _(i):
            win = jnp.minimum(wid * per + i, n_win - 1)
            pltpu.sync_copy(d_hbm.at[win], idx_v)
            pltpu.sync_copy(x_hbm.at[pl.ds(win * SC_WINDOW, SC_WINDOW)], rows_v)
            for kk in range(TOP_K):
                pltpu.sync_copy(rows_v, o_hbm.at[idx_v.at[kk]])

    return pl.kernel(
        body,
        out_type=jax.ShapeDtypeStruct((cap, ROW_WORDS), jnp.int32),
        mesh=plsc.VectorSubcoreMesh(core_axis_name="c", subcore_axis_name="s"),
        scratch_types=[pltpu.VMEM((TOP_K, SC_WINDOW), jnp.int32),
                       pltpu.VMEM((SC_WINDOW, ROW_WORDS), jnp.int32)],
        name="moe_dispatch",
    )(x_words, dest3)


def _sc_gather(y_words, idx2):
    n_win = idx2.shape[0]
    per = _sc_worker_windows(n_win)

    def body(y_hbm, i_hbm, o_hbm, idx_v, rows_v, sem):
        wid = lax.axis_index("s") * SC_CORES + lax.axis_index("c")

        @pl.loop(0, per)
        def _(i):
            win = jnp.minimum(wid * per + i, n_win - 1)
            pltpu.sync_copy(i_hbm.at[win], idx_v)
            pltpu.async_copy(y_hbm.at[idx_v], rows_v, sem).wait()
            pltpu.sync_copy(rows_v, o_hbm.at[pl.ds(win * SC_WINDOW, SC_WINDOW)])

    return pl.kernel(
        body,
        out_type=jax.ShapeDtypeStruct((n_win * SC_WINDOW, ROW_WORDS), jnp.int32),
        mesh=plsc.VectorSubcoreMesh(core_axis_name="c", subcore_axis_name="s"),
        scratch_types=[pltpu.VMEM((SC_WINDOW,), jnp.int32),
                       pltpu.VMEM((SC_WINDOW, ROW_WORDS), jnp.int32),
                       pltpu.SemaphoreType.DMA],
        name="moe_gather",
    )(y_words, idx2)


def _combine_kernel(h_ref, y_ref, w_ref, fw_ref, o_ref):
    acc = h_ref[0]
    for kk in range(TOP_K):
        acc = acc + w_ref[0, :, kk:kk + 1] * _unpack_rows(y_ref[kk, 0])
    o_ref[0] = acc * lax.rsqrt(jnp.mean(acc * acc, axis=-1, keepdims=True) + EPS) * fw_ref[...]


def _combine(h1, yg, wts, final_w, b0, nb, lr):
    tr = _pick(lr, (1024, 512, 256, 128))
    return pl.pallas_call(
        _combine_kernel,
        grid=(nb, lr // tr),
        in_specs=[
            pl.BlockSpec((1, tr, D_MODEL), lambda i, j: (b0 + i, j, 0)),
            pl.BlockSpec((TOP_K, 1, tr, ROW_WORDS), lambda i, j: (0, b0 + i, j, 0)),
            pl.BlockSpec((1, tr, TOP_K), lambda i, j: (b0 + i, j, 0)),
            pl.BlockSpec((1, D_MODEL), lambda i, j: (0, 0)),
        ],
        out_specs=pl.BlockSpec((1, tr, D_MODEL), lambda i, j: (i, j, 0)),
        out_shape=jax.ShapeDtypeStruct((nb, lr, D_MODEL), F32),
        compiler_params=pltpu.CompilerParams(
            dimension_semantics=("arbitrary", "arbitrary"), vmem_limit_bytes=VMEM_LIMIT),
        name="combine",
    )(h1, yg, wts, final_w)


def _rope_tables(lr):
    l = lr + N_META
    half = ROT_DIM // 2
    pos = jnp.concatenate([jnp.arange(N_META, l, dtype=F32), jnp.arange(N_META, dtype=F32)])
    inv_freq = ROPE_THETA ** (-jnp.arange(0, ROT_DIM, 2, dtype=F32) / ROT_DIM)
    ang = pos[:, None] * inv_freq[None, :]
    cos, sin = jnp.cos(ang), jnp.sin(ang)
    z = jnp.zeros((l, A_QK_DIM - ROT_DIM), F32)
    zh = jnp.zeros((l, half), F32)
    cos64 = jnp.concatenate([cos, cos, z + 1.0], axis=1)
    sa64 = jnp.concatenate([-sin, zh, z], axis=1)
    sb64 = jnp.concatenate([zh, sin, z], axis=1)
    dup = lambda a: jnp.concatenate([a, a], axis=1)
    return dup(cos64), dup(sa64), dup(sb64)


def kernel(x_prompt, x_sample, meta_tokens, norm1_w, w_in, lambda_q1, lambda_k1, lambda_q2, lambda_k2, attn_subln_w, mlstm_gate_b, mlstm_norm_w, w_br_attn, w_br_mlstm, w_out, norm2_w, router_w, router_b, w_gate, b_gate, w_up, b_up, w_down, b_down, final_norm_w):
    bp, lr, _ = x_prompt.shape
    bs = x_sample.shape[0]
    b = bp + bs
    l = lr + N_META
    n = b * l

    w_in0 = w_in[0]
    g0 = 3 * 1024 + 2 * 512 + 2 * 1024
    w_main = jnp.concatenate([w_in0[:, :g0], w_in0[:, g0 + 4 * M_HEADS:]], axis=1).astype(BF16)
    w_gates = jnp.pad(w_in0[:, g0:g0 + 4 * M_HEADS], ((0, 0), (0, LANES - 4 * M_HEADS))).astype(BF16)
    lam = (jnp.exp(jnp.sum(lambda_q1[0] * lambda_k1[0])) - jnp.exp(jnp.sum(lambda_q2[0] * lambda_k2[0]))
           + LAM_INIT)
    lam_row = jnp.full((1, LANES), lam, F32)
    sub_w = (attn_subln_w[0] * (1.0 - LAM_INIT)).reshape(1, A_V_DIM)
    neg_row = jnp.where(jnp.arange(LANES) < N_META, 0.0, NEG_BIG).astype(F32).reshape(1, LANES)
    cos_t, sa_t, sb_t = _rope_tables(lr)
    gate_bias = jnp.broadcast_to(
        jnp.pad(mlstm_gate_b[0].T, ((0, 0), (0, 4)))[:, :, None], (M_HEADS, 8, LANES)).astype(F32)
    norm_w_m = mlstm_norm_w[0].reshape(M_HEADS, 1, M_V_DIM)
    rw_t = router_w[0].T.astype(BF16)
    rb_col = jnp.broadcast_to(router_b[0][:, None], (N_EXPERTS, LANES)).astype(F32)
    tri = (jnp.arange(LANES)[:, None] < jnp.arange(LANES)[None, :]).astype(BF16)

    x_all = jnp.concatenate([x_prompt, x_sample], axis=0)
    meta_b = jnp.broadcast_to(meta_tokens[None].astype(x_all.dtype), (b, N_META, D_MODEL))
    h0 = jnp.concatenate([x_all, meta_b], axis=1).reshape(n, D_MODEL)

    proj, gates = _inproj(h0, norm1_w[0].reshape(1, D_MODEL), w_main, w_gates)
    proj3 = proj.reshape(b, l, N_MAIN)

    o_a = _attention(proj3, cos_t, sa_t, sb_t, neg_row, lam_row, sub_w, lr)

    g4 = gates.reshape(b, l, 4, M_HEADS).transpose(0, 3, 2, 1)
    off = jnp.array([NEG_BIG, 1e4, NEG_BIG, 1e4], F32)[None, None, :, None]
    fill = jnp.broadcast_to(off, (b, M_HEADS, 4, CHUNK - N_META))
    g_row = jnp.concatenate([g4[..., :lr], fill, g4[..., lr:]], axis=-1)
    g_row = jnp.pad(g_row, ((0, 0), (0, 0), (0, 4), (0, 0)))
    h_m = _mlstm(proj3, g_row, gate_bias, norm_w_m, lr)

    h1, xn2, top_e, top_w, rank, cnt = _merge(
        o_a.reshape(n, D_MODEL), h_m.reshape(n, D_MODEL), proj, h0,
        w_br_attn[0].astype(BF16), w_br_mlstm[0].astype(BF16), w_out[0].astype(BF16),
        norm2_w[0].reshape(1, D_MODEL), rw_t, rb_col, tri)

    counts = cnt[:, 0].astype(jnp.int32)
    padded = (counts + MOE_ROWS - 1) // MOE_ROWS * MOE_ROWS
    pends = jnp.cumsum(padded)
    pstarts = pends - padded
    e_ids = jnp.arange(N_EXPERTS, dtype=jnp.int32)
    pstart_of = jnp.sum(jnp.where(top_e[..., None] == e_ids, pstarts, 0), axis=-1)
    dest = pstart_of + rank
    n_blocks = (n * TOP_K + N_EXPERTS * (MOE_ROWS - 1) + MOE_ROWS - 1) // MOE_ROWS
    cap = n_blocks * MOE_ROWS
    blk_start = jnp.arange(n_blocks, dtype=jnp.int32) * MOE_ROWS
    blk_e = jnp.minimum(jnp.sum((pends[None, :] <= blk_start[:, None]).astype(jnp.int32), axis=1),
                        N_EXPERTS - 1)
    n_used = (pends[-1:] // MOE_ROWS).astype(jnp.int32)

    dest_win = dest.reshape(TOP_K, n // SC_WINDOW, SC_WINDOW).transpose(1, 0, 2)
    xs = _sc_dispatch(xn2, dest_win, cap)
    y = _experts(blk_e, n_used, xs,
                 w_gate[0], b_gate[0].reshape(N_EXPERTS, 1, D_FF),
                 w_up[0], b_up[0].reshape(N_EXPERTS, 1, D_FF),
                 w_down[0], b_down[0].reshape(N_EXPERTS, 1, D_MODEL))
    yg = _sc_gather(y, dest.reshape(TOP_K * n // SC_WINDOW, SC_WINDOW)).reshape(TOP_K, b, l, ROW_WORDS)
    wts = top_w.T.reshape(b, l, TOP_K)
    h1_3 = h1.reshape(b, l, D_MODEL)
    fw = final_norm_w.reshape(1, D_MODEL)
    y_prompt = _combine(h1_3, yg, wts, fw, 0, bp, lr)
    y_sample = _combine(h1_3, yg, wts, fw, bp, bs, lr)
    return (y_prompt, y_sample)
```

```python
import functools
import math

import jax
import jax.numpy as jnp
from jax import lax
from jax.experimental import pallas as pl
from jax.experimental.pallas import tpu as pltpu
from jax.experimental.pallas import tpu_sc as plsc

F32 = jnp.float32
BF16 = jnp.bfloat16

D_MODEL = 1024
N_META = 16
EPS = 1e-6
A_HEADS = 8
A_QK_DIM = 64
A_V_DIM = 128
ROT_DIM = 16
ROPE_THETA = 500000.0
M_HEADS = 4
M_QK_DIM = 128
M_V_DIM = 256
CHUNK = 128
N_EXPERTS = 32
TOP_K = 4
D_FF = 1024
SWIGLU_LIMIT = 7.0
SWIGLU_ALPHA = 1.702
LAM_INIT = 0.8 - 0.6 * math.exp(-0.3 * 0)

LANES = 128
NEG_BIG = -1e30
MOE_ROWS = 512
SC_CORES = 2
SC_SUBCORES = 16
SC_WINDOW = 128
ROW_WORDS = D_MODEL // 2
W_AQK = A_HEADS * 2 * A_QK_DIM
W_AV = A_HEADS * A_V_DIM
W_MQK = M_HEADS * M_QK_DIM
W_MV = M_HEADS * M_V_DIM
OFF_GATES = 2 * W_AQK + W_AV + 2 * W_MQK + 2 * W_MV
N_MAIN = OFF_GATES + 2 * D_MODEL
COL_AQ, COL_AK, COL_AV = 0, W_AQK // LANES, 2 * W_AQK // LANES
COL_MQ = (2 * W_AQK + W_AV) // M_QK_DIM
COL_MK = COL_MQ + M_HEADS
COL_MV = (2 * W_AQK + W_AV + 2 * W_MQK) // M_V_DIM
COL_MO = COL_MV + M_HEADS
COL_GA = OFF_GATES // D_MODEL
COL_GM = COL_GA + 1

VMEM_LIMIT = 56 * 1024 * 1024


def _pick(n, prefs):
    for p in prefs:
        if n % p == 0:
            return p
    raise ValueError(f"no tile for {n}")


def _nt(a, b):
    return lax.dot_general(a, b, (((1,), (1,)), ((), ())), preferred_element_type=F32)


def _dot(a, b):
    return jnp.dot(a, b, preferred_element_type=F32)


def _pack_rows(x):
    w = x.shape[1] // 2
    bits = lambda v: lax.bitcast_convert_type(v.astype(BF16).astype(F32), jnp.uint32)
    lo = lax.shift_right_logical(bits(x[:, :w]), jnp.uint32(16))
    hi = bits(x[:, w:]) & jnp.uint32(0xFFFF0000)
    return lax.bitcast_convert_type(lo | hi, jnp.int32)


def _unpack_rows(wds):
    u = lax.bitcast_convert_type(wds, jnp.uint32)
    lo = lax.bitcast_convert_type(lax.shift_left(u, jnp.uint32(16)), F32)
    hi = lax.bitcast_convert_type(u & jnp.uint32(0xFFFF0000), F32)
    return jnp.concatenate([lo, hi], axis=1)


def _inproj_kernel(x_ref, n1_ref, w_ref, wg_ref, o_ref, g_ref, xn_ref):
    @pl.when(pl.program_id(1) == 0)
    def _():
        x = x_ref[...]
        ms = jnp.mean(x * x, axis=-1, keepdims=True)
        xn = (x * lax.rsqrt(ms + EPS) * n1_ref[...]).astype(BF16)
        xn_ref[...] = xn
        g_ref[...] = _dot(xn, wg_ref[...])[:, :4 * M_HEADS]

    o_ref[...] = _dot(xn_ref[...], w_ref[...]).astype(BF16)


def _inproj(h0, norm1_w, w_main, w_gates):
    n = h0.shape[0]
    tm = _pick(n, (1152, 768, 512, 384, 256, 128))
    tn = 2048
    return pl.pallas_call(
        _inproj_kernel,
        grid=(n // tm, N_MAIN // tn),
        in_specs=[
            pl.BlockSpec((tm, D_MODEL), lambda i, j: (i, 0)),
            pl.BlockSpec((1, D_MODEL), lambda i, j: (0, 0)),
            pl.BlockSpec((D_MODEL, tn), lambda i, j: (0, j)),
            pl.BlockSpec((D_MODEL, LANES), lambda i, j: (0, 0)),
        ],
        out_specs=[
            pl.BlockSpec((tm, tn), lambda i, j: (i, j)),
            pl.BlockSpec((tm, 4 * M_HEADS), lambda i, j: (i, 0)),
        ],
        out_shape=[
            jax.ShapeDtypeStruct((n, N_MAIN), BF16),
            jax.ShapeDtypeStruct((n, 4 * M_HEADS), F32),
        ],
        scratch_shapes=[pltpu.VMEM((tm, D_MODEL), BF16)],
        compiler_params=pltpu.CompilerParams(
            dimension_semantics=("arbitrary", "arbitrary"), vmem_limit_bytes=VMEM_LIMIT),
        name="inproj",
    )(h0, norm1_w, w_main, w_gates)


def _attn_kernel(q_ref, k_ref, v_ref, cos_ref, sa_ref, sb_ref, neg_ref, lam_ref, sw_ref,
                 o_ref, q1_ref, q2_ref, vs_ref, kt_ref, sa1_ref, sa2_ref, sb1_ref, sb2_ref, *, lr):
    l = lr + N_META
    lp = lr + LANES
    nblk = lr // LANES
    scale = A_QK_DIM ** -0.5 * math.log2(math.e)

    def rope(t):
        return (t * cos_ref[...] + pltpu.roll(t, LANES - ROT_DIM // 2, 1) * sa_ref[...]
                + pltpu.roll(t, ROT_DIM // 2, 1) * sb_ref[...])

    lane = lax.broadcasted_iota(jnp.int32, (l, LANES), 1)
    qr = rope(q_ref[0].astype(F32)) * scale
    q1_ref[0:l, :] = jnp.where(lane < A_QK_DIM, qr, 0.0).astype(BF16)
    q2_ref[0:l, :] = jnp.where(lane >= A_QK_DIM, qr, 0.0).astype(BF16)
    vs_ref[0:l, :] = v_ref[0]
    pad = jnp.zeros((lp - l, LANES), BF16)
    q1_ref[l:lp, :] = pad
    q2_ref[l:lp, :] = pad
    vs_ref[l:lp, :] = pad
    kr = rope(k_ref[0].astype(F32))
    for c in range(nblk):
        kt_ref[:, c * LANES:(c + 1) * LANES] = kr[c * LANES:(c + 1) * LANES, :].T.astype(BF16)
    k_tail = jnp.concatenate([kr[lr:l, :], jnp.zeros((lp - l, LANES), F32)], axis=0)
    kt_ref[:, lr:lp] = k_tail.T.astype(BF16)

    lam = lam_ref[:, 0:1]

    def scores(r0, s1_ref, s2_ref):
        rows = pl.ds(r0, LANES)
        k_all = kt_ref[...]
        s1_ref[...] = _dot(q1_ref[rows, :], k_all)
        s2_ref[...] = _dot(q2_ref[rows, :], k_all)

    def softmax_parts(s_ref, nrows):
        sm = s_ref[0:nrows, :lr]
        sl = s_ref[0:nrows, lr:] + neg_ref[...]
        m = jnp.maximum(jnp.max(sm, axis=1, keepdims=True), jnp.max(sl, axis=1, keepdims=True))
        pm = jnp.exp2(sm - m)
        pl_ = jnp.exp2(sl - m)
        tot = jnp.sum(pm, axis=1, keepdims=True) + jnp.sum(pl_, axis=1, keepdims=True)
        return pm, pl_, tot

    def finish(r0, nrows, s1_ref, s2_ref):
        p1m, p1l, t1 = softmax_parts(s1_ref, nrows)
        p2m, p2l, t2 = softmax_parts(s2_ref, nrows)
        c = lam * t1 / t2
        am = (p1m - p2m * c).astype(BF16)
        al = (p1l - p2l * c).astype(BF16)
        o = (_dot(am, vs_ref[0:lr, :]) + _dot(al, vs_ref[lr:lp, :])) * (1.0 / t1)
        o = o * lax.rsqrt(jnp.mean(o * o, axis=-1, keepdims=True) + EPS) * sw_ref[...]
        o_ref[0, pl.ds(r0, nrows), :] = o.astype(BF16)

    scores(0, sa1_ref, sa2_ref)

    pairs = _pick(nblk // 2, (8, 4, 2, 1))

    def body(j, c):
        for u in range(pairs):
            ra = pl.multiple_of((j * pairs + u) * (2 * LANES), LANES)
            rb = ra + LANES
            scores(rb, sb1_ref, sb2_ref)
            finish(ra, LANES, sa1_ref, sa2_ref)
            scores(rb + LANES, sa1_ref, sa2_ref)
            finish(rb, LANES, sb1_ref, sb2_ref)
        return c

    lax.fori_loop(0, nblk // (2 * pairs), body, 0)
    finish(lr, N_META, sa1_ref, sa2_ref)


def _attention(proj3, cos_t, sa_t, sb_t, neg_row, lam_row, sub_w, lr):
    b, l, _ = proj3.shape
    lp = lr + LANES
    const2 = lambda i, h: (0, 0)
    return pl.pallas_call(
        functools.partial(_attn_kernel, lr=lr),
        grid=(b, A_HEADS),
        in_specs=[
            pl.BlockSpec((1, l, LANES), lambda i, h: (i, 0, COL_AQ + h)),
            pl.BlockSpec((1, l, LANES), lambda i, h: (i, 0, COL_AK + h)),
            pl.BlockSpec((1, l, LANES), lambda i, h: (i, 0, COL_AV + h)),
            pl.BlockSpec((l, LANES), const2),
            pl.BlockSpec((l, LANES), const2),
            pl.BlockSpec((l, LANES), const2),
            pl.BlockSpec((1, LANES), const2),
            pl.BlockSpec((1, LANES), const2),
            pl.BlockSpec((1, LANES), const2),
        ],
        out_specs=pl.BlockSpec((1, l, LANES), lambda i, h: (i, 0, h)),
        out_shape=jax.ShapeDtypeStruct((b, l, A_HEADS * A_V_DIM), BF16),
        scratch_shapes=([pltpu.VMEM((lp, LANES), BF16)] * 3 + [pltpu.VMEM((LANES, lp), BF16)]
                        + [pltpu.VMEM((LANES, lp), F32)] * 4),
        compiler_params=pltpu.CompilerParams(
            dimension_semantics=("arbitrary", "arbitrary"), vmem_limit_bytes=VMEM_LIMIT),
        name="diff_attn",
    )(proj3, proj3, proj3, cos_t, sa_t, sb_t, neg_row, lam_row, sub_w)


def _log_sigmoid(x):
    return jnp.minimum(x, 0.0) - jnp.log(1.0 + jnp.exp(-jnp.abs(x)))


def _mlstm_kernel(q_ref, k_ref, v_ref, mo_ref, g_ref, gb_ref, nw_ref, o_ref,
                  c_ref, m_ref, kt_ref, h0_ref, u_ref, bcol_ref, amax_ref, aend_ref, bend_ref,
                  hf_ref, hb_ref, *, lr):
    l = lr + N_META
    t = CHUNK
    nc = lr // t
    dv = M_V_DIM
    scale = M_QK_DIM ** -0.5
    row = lax.broadcasted_iota(jnp.int32, (t, t), 0)
    col = lax.broadcasted_iota(jnp.int32, (t, t), 1)
    eye = row == col
    ones_tile = jnp.where(col == 0, 1.0, 0.0).astype(BF16)
    meta_r0 = l - t

    c_ref[...] = jnp.zeros(c_ref.shape, F32)
    m_ref[...] = jnp.zeros(m_ref.shape, F32)

    for c in range(nc + 1):
        r0 = c * t if c < nc else meta_r0
        kt_ref[:, c * t:(c + 1) * t] = k_ref[0, r0:r0 + t, :].astype(F32).T.astype(BF16)

    def scaled_q(r0):
        return (q_ref[0, pl.ds(r0, t), :].astype(F32) * scale).astype(BF16)

    def intra(ci, r0, g0):
        v_aug = jnp.concatenate([v_ref[0, pl.ds(r0, t), :], ones_tile], axis=1)
        kt = kt_ref[:, pl.ds(g0, t)]
        s_raw = _dot(scaled_q(r0), kt)
        for d, mask in ((0, col <= row), (1, col >= row)):
            gi = 2 * d
            ig = g_ref[0, 0, gi:gi + 1, pl.ds(g0, t)] + gb_ref[0, gi:gi + 1, :]
            lf = _log_sigmoid(g_ref[0, 0, gi + 1:gi + 2, pl.ds(g0, t)] + gb_ref[0, gi + 1:gi + 2, :])
            b_col = jnp.sum(jnp.where(mask, jnp.broadcast_to(lf, (t, t)), 0.0), axis=1, keepdims=True)
            b_row = jnp.sum(jnp.where(eye, jnp.broadcast_to(b_col, (t, t)), 0.0), axis=0, keepdims=True)
            a_row = ig - b_row
            dm = jnp.where(mask, jnp.broadcast_to(a_row, (t, t)), NEG_BIG)
            amax = jnp.max(dm, axis=1, keepdims=True)
            s0 = s_raw * jnp.exp(dm - amax)
            h0_ref[d, pl.ds(g0, t), :] = _dot(s0.astype(BF16), v_aug)
            a_end = jnp.max(a_row, axis=1, keepdims=True)
            kts = (kt.astype(F32) * jnp.exp(a_row - a_end)).astype(BF16)
            u_ref[d, ci] = _dot(kts, v_aug)
            bcol_ref[d, pl.ds(g0, t), :] = b_col
            amax_ref[d, pl.ds(g0, t), :] = amax
            aend_ref[d, ci] = jnp.broadcast_to(a_end, (8, LANES))
            bend_ref[d, ci] = jnp.broadcast_to(jnp.sum(lf, axis=1, keepdims=True), (8, LANES))

    ua = _pick(nc, (8, 4, 2, 1))

    def abody(i, c):
        for u in range(ua):
            ci = ua * i + u
            r0 = pl.multiple_of(ci * t, t)
            intra(ci, r0, r0)
        return c

    lax.fori_loop(0, nc // ua, abody, 0)
    intra(nc, meta_r0, lr)

    def step(d, ci, r0, g0):
        m_prev = m_ref[d, 0:1, 0:1]
        amax = amax_ref[d, pl.ds(g0, t), :]
        m_col = jnp.maximum(m_prev, amax)
        num = (jnp.exp(m_prev - m_col) * _dot(scaled_q(r0), c_ref[d].astype(BF16))
               + jnp.exp(amax - m_col) * h0_ref[d, pl.ds(g0, t), :])
        den = jnp.maximum(jnp.abs(num[:, dv:dv + 1]), jnp.exp(-(bcol_ref[d, pl.ds(g0, t), :] + m_col)))
        a_end = aend_ref[d, ci, 0:1, 0:1]
        m_end = jnp.maximum(m_prev, a_end)
        c_ref[d] = jnp.exp(m_prev - m_end) * c_ref[d] + jnp.exp(a_end - m_end) * u_ref[d, ci]
        m_ref[d] = jnp.broadcast_to(bend_ref[d, ci, 0:1, 0:1] + m_end, (8, LANES))
        return num[:, :dv] * (1.0 / den)

    hf_ref[lr:l, :] = step(0, nc, meta_r0, lr)[t - N_META:, :]

    ub = _pick(nc, (2, 1))

    def bbody(i, c):
        for u in range(ub):
            jf = ub * i + u
            rf = pl.multiple_of(jf * t, t)
            hf_ref[pl.ds(rf, t), :] = step(0, jf, rf, rf)
            jb = nc - 1 - jf
            rb = pl.multiple_of(jb * t, t)
            hb_ref[pl.ds(rb, t), :] = step(1, jb, rb, rb)
        return c

    lax.fori_loop(0, nc // ub, bbody, 0)
    hb_ref[lr:l, :] = step(1, nc, meta_r0, lr)[t - N_META:, :]

    def finish(r0, rows):
        hs = hf_ref[pl.ds(r0, rows), :] + hb_ref[pl.ds(r0, rows), :]
        y = hs * lax.rsqrt(jnp.mean(hs * hs, axis=-1, keepdims=True) + EPS) * nw_ref[0]
        gate = jax.nn.sigmoid(mo_ref[0, pl.ds(r0, rows), :].astype(F32))
        o_ref[0, pl.ds(r0, rows), :] = (y * gate).astype(BF16)

    def fbody(i, c):
        for u in range(ua):
            finish(pl.multiple_of((ua * i + u) * t, t), t)
        return c

    lax.fori_loop(0, nc // ua, fbody, 0)
    finish(lr, N_META)


def _mlstm(proj3, gates_row, gate_bias, norm_w, lr):
    b, l, _ = proj3.shape
    gl = gates_row.shape[-1]
    nw = lr // CHUNK + 1
    aug = M_V_DIM + LANES
    return pl.pallas_call(
        functools.partial(_mlstm_kernel, lr=lr),
        grid=(b, M_HEADS),
        in_specs=[
            pl.BlockSpec((1, l, M_QK_DIM), lambda i, h: (i, 0, COL_MQ + h)),
            pl.BlockSpec((1, l, M_QK_DIM), lambda i, h: (i, 0, COL_MK + h)),
            pl.BlockSpec((1, l, M_V_DIM), lambda i, h: (i, 0, COL_MV + h)),
            pl.BlockSpec((1, l, M_V_DIM), lambda i, h: (i, 0, COL_MO + h)),
            pl.BlockSpec((1, 1, 8, gl), lambda i, h: (i, h, 0, 0)),
            pl.BlockSpec((1, 8, LANES), lambda i, h: (h, 0, 0)),
            pl.BlockSpec((1, 1, M_V_DIM), lambda i, h: (h, 0, 0)),
        ],
        out_specs=pl.BlockSpec((1, l, M_V_DIM), lambda i, h: (i, 0, h)),
        out_shape=jax.ShapeDtypeStruct((b, l, M_HEADS * M_V_DIM), BF16),
        scratch_shapes=[
            pltpu.VMEM((2, M_QK_DIM, aug), F32),
            pltpu.VMEM((2, 8, LANES), F32),
            pltpu.VMEM((M_QK_DIM, gl), BF16),
            pltpu.VMEM((2, gl, aug), F32),
            pltpu.VMEM((2, nw, M_QK_DIM, aug), F32),
            pltpu.VMEM((2, gl, 1), F32),
            pltpu.VMEM((2, gl, 1), F32),
            pltpu.VMEM((2, nw, 8, LANES), F32),
            pltpu.VMEM((2, nw, 8, LANES), F32),
            pltpu.VMEM((l, M_V_DIM), F32), pltpu.VMEM((l, M_V_DIM), F32),
        ],
        compiler_params=pltpu.CompilerParams(
            dimension_semantics=("arbitrary", "arbitrary"), vmem_limit_bytes=VMEM_LIMIT),
        name="mlstm",
    )(proj3, proj3, proj3, proj3, gates_row, gate_bias, norm_w)


def _merge_kernel(oa_ref, hm_ref, ga_ref, gm_ref, h_ref, wa_ref, wm_ref, wo_ref, n2_ref,
                  rw_ref, rb_ref, tri_ref,
                  h1_ref, xn_ref, te_ref, tw_ref, rk_ref, cnt_ref, carry_ref, *, tm):
    @pl.when(pl.program_id(0) == 0)
    def _():
        carry_ref[...] = jnp.zeros(carry_ref.shape, F32)

    pa = _dot(oa_ref[...], wa_ref[...])
    pm = _dot(hm_ref[...], wm_ref[...])
    merged = (jax.nn.sigmoid(ga_ref[...].astype(F32)) * pa
              + jax.nn.sigmoid(gm_ref[...].astype(F32)) * pm)
    h1 = h_ref[...] + _dot(merged.astype(BF16), wo_ref[...])
    h1_ref[...] = h1
    xn32 = h1 * lax.rsqrt(jnp.mean(h1 * h1, axis=-1, keepdims=True) + EPS) * n2_ref[...]
    xn = xn32.astype(BF16)
    xn_ref[...] = _pack_rows(xn32)

    logits = _nt(rw_ref[...], xn) + rb_ref[:, 0:1]
    eidx = lax.broadcasted_iota(jnp.int32, (N_EXPERTS, tm), 0)
    work = logits
    vals, hots = [], []
    for _ in range(TOP_K):
        mx = jnp.max(work, axis=0, keepdims=True)
        sel = jnp.min(jnp.where(work == mx, eidx, N_EXPERTS), axis=0, keepdims=True)
        hot = eidx == sel
        vals.append(mx)
        hots.append(hot)
        work = jnp.where(hot, -jnp.inf, work)
    ex = [jnp.exp(v - vals[0]) for v in vals]
    inv = 1.0 / (ex[0] + ex[1] + ex[2] + ex[3])
    chosen = jnp.where(hots[0] | hots[1] | hots[2] | hots[3], 1.0, 0.0)

    carry = carry_ref[:, 0:1]
    ranks = []
    for s in range(tm // LANES):
        sub = chosen[:, s * LANES:(s + 1) * LANES]
        ranks.append(_dot(sub.astype(BF16), tri_ref[...]) + carry)
        carry = carry + jnp.sum(sub, axis=1, keepdims=True)
    before = jnp.concatenate(ranks, axis=1) if len(ranks) > 1 else ranks[0]
    carry_ref[...] = jnp.broadcast_to(carry, carry_ref.shape)
    cnt_ref[...] = jnp.broadcast_to(carry, cnt_ref.shape)

    for kk in range(TOP_K):
        te_ref[kk:kk + 1, :] = jnp.sum(jnp.where(hots[kk], eidx, 0), axis=0, keepdims=True)
        tw_ref[kk:kk + 1, :] = ex[kk] * inv
        rk_ref[kk:kk + 1, :] = jnp.sum(jnp.where(hots[kk], before, 0.0), axis=0,
                                       keepdims=True).astype(jnp.int32)


def _merge(o_a, h_m, proj, h0, w_a, w_m, w_o, norm2_w, rw_t, rb_col, tri):
    n = h0.shape[0]
    tm = _pick(n, (384, 256, 128))
    row = lambda i: (i, 0)
    const = lambda i: (0, 0)
    tok = lambda i: (0, i)
    wspec = pl.BlockSpec((D_MODEL, D_MODEL), const)
    return pl.pallas_call(
        functools.partial(_merge_kernel, tm=tm),
        grid=(n // tm,),
        in_specs=[
            pl.BlockSpec((tm, D_MODEL), row),
            pl.BlockSpec((tm, D_MODEL), row),
            pl.BlockSpec((tm, D_MODEL), lambda i: (i, COL_GA)),
            pl.BlockSpec((tm, D_MODEL), lambda i: (i, COL_GM)),
            pl.BlockSpec((tm, D_MODEL), row),
            wspec, wspec, wspec,
            pl.BlockSpec((1, D_MODEL), const),
            pl.BlockSpec((N_EXPERTS, D_MODEL), const),
            pl.BlockSpec((N_EXPERTS, LANES), const),
            pl.BlockSpec((LANES, LANES), const),
        ],
        out_specs=[
            pl.BlockSpec((tm, D_MODEL), row),
            pl.BlockSpec((tm, ROW_WORDS), row),
            pl.BlockSpec((TOP_K, tm), tok),
            pl.BlockSpec((TOP_K, tm), tok),
            pl.BlockSpec((TOP_K, tm), tok),
            pl.BlockSpec((N_EXPERTS, LANES), const),
        ],
        out_shape=[
            jax.ShapeDtypeStruct((n, D_MODEL), F32),
            jax.ShapeDtypeStruct((n, ROW_WORDS), jnp.int32),
            jax.ShapeDtypeStruct((TOP_K, n), jnp.int32),
            jax.ShapeDtypeStruct((TOP_K, n), F32),
            jax.ShapeDtypeStruct((TOP_K, n), jnp.int32),
            jax.ShapeDtypeStruct((N_EXPERTS, LANES), F32),
        ],
        scratch_shapes=[pltpu.VMEM((N_EXPERTS, LANES), F32)],
        compiler_params=pltpu.CompilerParams(
            dimension_semantics=("arbitrary",), vmem_limit_bytes=VMEM_LIMIT),
        name="merge_router",
    )(o_a, h_m, proj, proj, h0, w_a, w_m, w_o, norm2_w, rw_t, rb_col, tri)


def _expert_kernel(be_ref, nu_ref, x_ref, wg_ref, bg_ref, wu_ref, bu_ref, wd_ref, bd_ref, y_ref,
                   wg_s, wu_s, wd_s):
    i = pl.program_id(0)
    used = i < nu_ref[0]

    @pl.when(jnp.logical_or(i == 0, be_ref[i] != be_ref[jnp.maximum(i - 1, 0)]))
    def _():
        wg_s[...] = wg_ref[0].astype(BF16)
        wu_s[...] = wu_ref[0].astype(BF16)
        wd_s[...] = wd_ref[0].astype(BF16)

    @pl.when(used)
    def _():
        x = _unpack_rows(x_ref[...]).astype(BF16)
        gt = jnp.minimum(_dot(x, wg_s[...]) + bg_ref[0], SWIGLU_LIMIT)
        up = jnp.clip(_dot(x, wu_s[...]) + bu_ref[0], -SWIGLU_LIMIT, SWIGLU_LIMIT)
        glu = gt * jax.nn.sigmoid(SWIGLU_ALPHA * gt)
        act = ((up + 1.0) * glu).astype(BF16)
        y_ref[...] = _pack_rows(_dot(act, wd_s[...]) + bd_ref[0])

    @pl.when(jnp.logical_not(used))
    def _():
        y_ref[...] = jnp.zeros(y_ref.shape, jnp.int32)


def _experts(blk_e, n_used, xs, w_g, b_g, w_u, b_u, w_d, b_d):
    cap = xs.shape[0]
    n_blocks = cap // MOE_ROWS
    wspec = lambda d0, d1: pl.BlockSpec((1, d0, d1), lambda i, be, nu: (be[i], 0, 0))
    return pl.pallas_call(
        _expert_kernel,
        grid_spec=pltpu.PrefetchScalarGridSpec(
            num_scalar_prefetch=2,
            grid=(n_blocks,),
            in_specs=[
                pl.BlockSpec((MOE_ROWS, ROW_WORDS), lambda i, be, nu: (i, 0)),
                wspec(D_MODEL, D_FF), wspec(1, D_FF),
                wspec(D_MODEL, D_FF), wspec(1, D_FF),
                wspec(D_FF, D_MODEL), wspec(1, D_MODEL),
            ],
            out_specs=pl.BlockSpec((MOE_ROWS, ROW_WORDS), lambda i, be, nu: (i, 0)),
            scratch_shapes=[pltpu.VMEM((D_MODEL, D_FF), BF16), pltpu.VMEM((D_MODEL, D_FF), BF16),
                            pltpu.VMEM((D_FF, D_MODEL), BF16)],
        ),
        out_shape=jax.ShapeDtypeStruct((cap, ROW_WORDS), jnp.int32),
        compiler_params=pltpu.CompilerParams(
            dimension_semantics=("arbitrary",), vmem_limit_bytes=VMEM_LIMIT),
        name="experts",
    )(blk_e, n_used, xs, w_g, b_g, w_u, b_u, w_d, b_d)


def _sc_worker_windows(n_win):
    return -(-n_win // (SC_CORES * SC_SUBCORES))


def _sc_dispatch(x_words, dest3, cap):
    n = x_words.shape[0]
    n_win = n // SC_WINDOW
    per = _sc_worker_windows(n_win)

    def body(x_hbm, d_hbm, o_hbm, idx_v, rows_v):
        wid = lax.axis_index("s") * SC_CORES + lax.axis_index("c")

        @pl.loop(0, per)
        def _(i):
            win = jnp.minimum(wid * per + i, n_win - 1)
            pltpu.sync_copy(d_hbm.at[win], idx_v)
            pltpu.sync_copy(x_hbm.at[pl.ds(win * SC_WINDOW, SC_WINDOW)], rows_v)
            for kk in range(TOP_K):
                pltpu.sync_copy(rows_v, o_hbm.at[idx_v.at[kk]])

    return pl.kernel(
        body,
        out_type=jax.ShapeDtypeStruct((cap, ROW_WORDS), jnp.int32),
        mesh=plsc.VectorSubcoreMesh(core_axis_name="c", subcore_axis_name="s"),
        scratch_types=[pltpu.VMEM((TOP_K, SC_WINDOW), jnp.int32),
                       pltpu.VMEM((SC_WINDOW, ROW_WORDS), jnp.int32)],
        name="moe_dispatch",
    )(x_words, dest3)


def _sc_gather(y_words, idx2):
    n_win = idx2.shape[0]
    per = _sc_worker_windows(n_win)

    def body(y_hbm, i_hbm, o_hbm, idx_v, rows_v, sem):
        wid = lax.axis_index("s") * SC_CORES + lax.axis_index("c")

        @pl.loop(0, per)
        def _(i):
            win = jnp.minimum(wid * per + i, n_win - 1)
            pltpu.sync_copy(i_hbm.at[win], idx_v)
            pltpu.async_copy(y_hbm.at[idx_v], rows_v, sem).wait()
            pltpu.sync_copy(rows_v, o_hbm.at[pl.ds(win * SC_WINDOW, SC_WINDOW)])

    return pl.kernel(
        body,
        out_type=jax.ShapeDtypeStruct((n_win * SC_WINDOW, ROW_WORDS), jnp.int32),
        mesh=plsc.VectorSubcoreMesh(core_axis_name="c", subcore_axis_name="s"),
        scratch_types=[pltpu.VMEM((SC_WINDOW,), jnp.int32),
                       pltpu.VMEM((SC_WINDOW, ROW_WORDS), jnp.int32),
                       pltpu.SemaphoreType.DMA],
        name="moe_gather",
    )(y_words, idx2)


def _combine_kernel(h_ref, y_ref, w_ref, fw_ref, o_ref):
    acc = h_ref[0]
    for kk in range(TOP_K):
        acc = acc + w_ref[0, :, kk:kk + 1] * _unpack_rows(y_ref[kk, 0])
    o_ref[0] = acc * lax.rsqrt(jnp.mean(acc * acc, axis=-1, keepdims=True) + EPS) * fw_ref[...]


def _combine(h1, yg, wts, final_w, b0, nb, lr):
    tr = _pick(lr, (1024, 512, 256, 128))
    return pl.pallas_call(
        _combine_kernel,
        grid=(nb, lr // tr),
        in_specs=[
            pl.BlockSpec((1, tr, D_MODEL), lambda i, j: (b0 + i, j, 0)),
            pl.BlockSpec((TOP_K, 1, tr, ROW_WORDS), lambda i, j: (0, b0 + i, j, 0)),
            pl.BlockSpec((1, tr, TOP_K), lambda i, j: (b0 + i, j, 0)),
            pl.BlockSpec((1, D_MODEL), lambda i, j: (0, 0)),
        ],
        out_specs=pl.BlockSpec((1, tr, D_MODEL), lambda i, j: (i, j, 0)),
        out_shape=jax.ShapeDtypeStruct((nb, lr, D_MODEL), F32),
        compiler_params=pltpu.CompilerParams(
            dimension_semantics=("arbitrary", "arbitrary"), vmem_limit_bytes=VMEM_LIMIT),
        name="combine",
    )(h1, yg, wts, final_w)


def _rope_tables(lr):
    l = lr + N_META
    half = ROT_DIM // 2
    pos = jnp.concatenate([jnp.arange(N_META, l, dtype=F32), jnp.arange(N_META, dtype=F32)])
    inv_freq = ROPE_THETA ** (-jnp.arange(0, ROT_DIM, 2, dtype=F32) / ROT_DIM)
    ang = pos[:, None] * inv_freq[None, :]
    cos, sin = jnp.cos(ang), jnp.sin(ang)
    z = jnp.zeros((l, A_QK_DIM - ROT_DIM), F32)
    zh = jnp.zeros((l, half), F32)
    cos64 = jnp.concatenate([cos, cos, z + 1.0], axis=1)
    sa64 = jnp.concatenate([-sin, zh, z], axis=1)
    sb64 = jnp.concatenate([zh, sin, z], axis=1)
    dup = lambda a: jnp.concatenate([a, a], axis=1)
    return dup(cos64), dup(sa64), dup(sb64)


def kernel(x_prompt, x_sample, meta_tokens, norm1_w, w_in, lambda_q1, lambda_k1, lambda_q2, lambda_k2, attn_subln_w, mlstm_gate_b, mlstm_norm_w, w_br_attn, w_br_mlstm, w_out, norm2_w, router_w, router_b, w_gate, b_gate, w_up, b_up, w_down, b_down, final_norm_w):
    bp, lr, _ = x_prompt.shape
    bs = x_sample.shape[0]
    b = bp + bs
    l = lr + N_META
    n = b * l
    assert x_sample.shape[1:] == x_prompt.shape[1:] == (lr, D_MODEL)
    assert lr % (2 * LANES) == 0, "attention pipelines query blocks in pairs; mLSTM chunks are 128 rows"
    assert n % SC_WINDOW == 0, "SparseCore dispatch works on whole 128-token windows"

    w_in0 = w_in[0]
    g0, g1 = OFF_GATES, OFF_GATES + 4 * M_HEADS
    w_main = jnp.concatenate([w_in0[:, :g0], w_in0[:, g1:]], axis=1).astype(BF16)
    w_gates = jnp.pad(w_in0[:, g0:g1], ((0, 0), (0, LANES - 4 * M_HEADS))).astype(BF16)
    lam = (jnp.exp(jnp.sum(lambda_q1[0] * lambda_k1[0])) - jnp.exp(jnp.sum(lambda_q2[0] * lambda_k2[0]))
           + LAM_INIT)
    lam_row = jnp.full((1, LANES), lam, F32)
    sub_w = (attn_subln_w[0] * (1.0 - LAM_INIT)).reshape(1, A_V_DIM)
    neg_row = jnp.where(jnp.arange(LANES) < N_META, 0.0, NEG_BIG).astype(F32).reshape(1, LANES)
    cos_t, sa_t, sb_t = _rope_tables(lr)
    gate_bias = jnp.broadcast_to(
        jnp.pad(mlstm_gate_b[0].T, ((0, 0), (0, 4)))[:, :, None], (M_HEADS, 8, LANES)).astype(F32)
    norm_w_m = mlstm_norm_w[0].reshape(M_HEADS, 1, M_V_DIM)
    rw_t = router_w[0].T.astype(BF16)
    rb_col = jnp.broadcast_to(router_b[0][:, None], (N_EXPERTS, LANES)).astype(F32)
    tri = (jnp.arange(LANES)[:, None] < jnp.arange(LANES)[None, :]).astype(BF16)

    x_all = jnp.concatenate([x_prompt, x_sample], axis=0)
    meta_b = jnp.broadcast_to(meta_tokens[None].astype(x_all.dtype), (b, N_META, D_MODEL))
    h0 = jnp.concatenate([x_all, meta_b], axis=1).reshape(n, D_MODEL)

    proj, gates = _inproj(h0, norm1_w[0].reshape(1, D_MODEL), w_main, w_gates)
    proj3 = proj.reshape(b, l, N_MAIN)

    o_a = _attention(proj3, cos_t, sa_t, sb_t, neg_row, lam_row, sub_w, lr)

    g4 = gates.reshape(b, l, 4, M_HEADS).transpose(0, 3, 2, 1)
    off = jnp.array([NEG_BIG, 1e4, NEG_BIG, 1e4], F32)[None, None, :, None]
    fill = jnp.broadcast_to(off, (b, M_HEADS, 4, CHUNK - N_META))
    g_row = jnp.concatenate([g4[..., :lr], fill, g4[..., lr:]], axis=-1)
    g_row = jnp.pad(g_row, ((0, 0), (0, 0), (0, 4), (0, 0)))
    h_m = _mlstm(proj3, g_row, gate_bias, norm_w_m, lr)

    h1, xn2, top_e, top_w, rank, cnt = _merge(
        o_a.reshape(n, D_MODEL), h_m.reshape(n, D_MODEL), proj, h0,
        w_br_attn[0].astype(BF16), w_br_mlstm[0].astype(BF16), w_out[0].astype(BF16),
        norm2_w[0].reshape(1, D_MODEL), rw_t, rb_col, tri)

    counts = cnt[:, 0].astype(jnp.int32)
    padded = (counts + MOE_ROWS - 1) // MOE_ROWS * MOE_ROWS
    pends = jnp.cumsum(padded)
    pstarts = pends - padded
    e_ids = jnp.arange(N_EXPERTS, dtype=jnp.int32)
    pstart_of = jnp.sum(jnp.where(top_e[..., None] == e_ids, pstarts, 0), axis=-1)
    dest = pstart_of + rank
    n_blocks = (n * TOP_K + N_EXPERTS * (MOE_ROWS - 1) + MOE_ROWS - 1) // MOE_ROWS
    cap = n_blocks * MOE_ROWS
    blk_start = jnp.arange(n_blocks, dtype=jnp.int32) * MOE_ROWS
    blk_e = jnp.minimum(jnp.sum((pends[None, :] <= blk_start[:, None]).astype(jnp.int32), axis=1),
                        N_EXPERTS - 1)
    n_used = (pends[-1:] // MOE_ROWS).astype(jnp.int32)

    dest_win = dest.reshape(TOP_K, n // SC_WINDOW, SC_WINDOW).transpose(1, 0, 2)
    xs = _sc_dispatch(xn2, dest_win, cap)
    y = _experts(blk_e, n_used, xs,
                 w_gate[0], b_gate[0].reshape(N_EXPERTS, 1, D_FF),
                 w_up[0], b_up[0].reshape(N_EXPERTS, 1, D_FF),
                 w_down[0], b_down[0].reshape(N_EXPERTS, 1, D_MODEL))
    yg = _sc_gather(y, dest.reshape(TOP_K * n // SC_WINDOW, SC_WINDOW)).reshape(TOP_K, b, l, ROW_WORDS)
    wts = top_w.T.reshape(b, l, TOP_K)
    h1_3 = h1.reshape(b, l, D_MODEL)
    fw = final_norm_w.reshape(1, D_MODEL)
    y_prompt = _combine(h1_3, yg, wts, fw, 0, bp, lr)
    y_sample = _combine(h1_3, yg, wts, fw, bp, bs, lr)
    return (y_prompt, y_sample)
```

```python
import functools
import math

import jax
import jax.numpy as jnp
from jax import lax
from jax.experimental import pallas as pl
from jax.experimental.pallas import tpu as pltpu
from jax.experimental.pallas import tpu_sc as plsc

F32 = jnp.float32
BF16 = jnp.bfloat16

D_MODEL = 1024
N_META = 16
EPS = 1e-6
A_HEADS = 8
A_QK_DIM = 64
A_V_DIM = 128
ROT_DIM = 16
ROPE_THETA = 500000.0
M_HEADS = 4
M_QK_DIM = 128
M_V_DIM = 256
CHUNK = 128
N_EXPERTS = 32
TOP_K = 4
D_FF = 1024
SWIGLU_LIMIT = 7.0
SWIGLU_ALPHA = 1.702
LAM_INIT = 0.8 - 0.6 * math.exp(-0.3 * 0)

LANES = 128
NEG_BIG = -1e30
MOE_ROWS = 512
SC_CORES = 2
SC_SUBCORES = 16
SC_WINDOW = 128
ROW_WORDS = D_MODEL // 2
W_AQK = A_HEADS * 2 * A_QK_DIM
W_AV = A_HEADS * A_V_DIM
W_MQK = M_HEADS * M_QK_DIM
W_MV = M_HEADS * M_V_DIM
OFF_GATES = 2 * W_AQK + W_AV + 2 * W_MQK + 2 * W_MV
N_MAIN = OFF_GATES + 2 * D_MODEL
COL_AQ, COL_AK, COL_AV = 0, W_AQK // LANES, 2 * W_AQK // LANES
COL_MQ = (2 * W_AQK + W_AV) // M_QK_DIM
COL_MK = COL_MQ + M_HEADS
COL_MV = (2 * W_AQK + W_AV + 2 * W_MQK) // M_V_DIM
COL_MO = COL_MV + M_HEADS
COL_GA = OFF_GATES // D_MODEL
COL_GM = COL_GA + 1

VMEM_LIMIT = 56 * 1024 * 1024


def _pick(n, prefs):
    for p in prefs:
        if n % p == 0:
            return p
    raise ValueError(f"no tile for {n}")


def _nt(a, b):
    return lax.dot_general(a, b, (((1,), (1,)), ((), ())), preferred_element_type=F32)


def _dot(a, b):
    return jnp.dot(a, b, preferred_element_type=F32)


def _pack_rows(x):
    w = x.shape[1] // 2
    bits = lambda v: lax.bitcast_convert_type(v.astype(BF16).astype(F32), jnp.uint32)
    lo = lax.shift_right_logical(bits(x[:, :w]), jnp.uint32(16))
    hi = bits(x[:, w:]) & jnp.uint32(0xFFFF0000)
    return lax.bitcast_convert_type(lo | hi, jnp.int32)


def _unpack_rows(wds):
    u = lax.bitcast_convert_type(wds, jnp.uint32)
    lo = lax.bitcast_convert_type(lax.shift_left(u, jnp.uint32(16)), F32)
    hi = lax.bitcast_convert_type(u & jnp.uint32(0xFFFF0000), F32)
    return jnp.concatenate([lo, hi], axis=1)


def _inproj_kernel(x_ref, n1_ref, w_ref, wg_ref, o_ref, g_ref, xn_ref):
    @pl.when(pl.program_id(1) == 0)
    def _():
        x = x_ref[...]
        ms = jnp.mean(x * x, axis=-1, keepdims=True)
        xn = (x * lax.rsqrt(ms + EPS) * n1_ref[...]).astype(BF16)
        xn_ref[...] = xn
        g_ref[...] = _dot(xn, wg_ref[...])[:, :4 * M_HEADS]

    o_ref[...] = _dot(xn_ref[...], w_ref[...]).astype(BF16)


def _inproj(h0, norm1_w, w_main, w_gates):
    n = h0.shape[0]
    tm = _pick(n, (1152, 768, 512, 384, 256, 128))
    tn = 2048
    return pl.pallas_call(
        _inproj_kernel,
        grid=(n // tm, N_MAIN // tn),
        in_specs=[
            pl.BlockSpec((tm, D_MODEL), lambda i, j: (i, 0)),
            pl.BlockSpec((1, D_MODEL), lambda i, j: (0, 0)),
            pl.BlockSpec((D_MODEL, tn), lambda i, j: (0, j)),
            pl.BlockSpec((D_MODEL, LANES), lambda i, j: (0, 0)),
        ],
        out_specs=[
            pl.BlockSpec((tm, tn), lambda i, j: (i, j)),
            pl.BlockSpec((tm, 4 * M_HEADS), lambda i, j: (i, 0)),
        ],
        out_shape=[
            jax.ShapeDtypeStruct((n, N_MAIN), BF16),
            jax.ShapeDtypeStruct((n, 4 * M_HEADS), F32),
        ],
        scratch_shapes=[pltpu.VMEM((tm, D_MODEL), BF16)],
        compiler_params=pltpu.CompilerParams(
            dimension_semantics=("arbitrary", "arbitrary"), vmem_limit_bytes=VMEM_LIMIT),
        name="inproj",
    )(h0, norm1_w, w_main, w_gates)


def _attn_kernel(q_ref, k_ref, v_ref, cos_ref, sa_ref, sb_ref, neg_ref, lam_ref, sw_ref,
                 o_ref, q1_ref, q2_ref, vs_ref, kt_ref, sa1_ref, sa2_ref, sb1_ref, sb2_ref, *, lr):
    l = lr + N_META
    lp = lr + LANES
    nblk = lr // LANES
    scale = A_QK_DIM ** -0.5 * math.log2(math.e)

    def rope(t):
        return (t * cos_ref[...] + pltpu.roll(t, LANES - ROT_DIM // 2, 1) * sa_ref[...]
                + pltpu.roll(t, ROT_DIM // 2, 1) * sb_ref[...])

    lane = lax.broadcasted_iota(jnp.int32, (l, LANES), 1)
    qr = rope(q_ref[0].astype(F32)) * scale
    q1_ref[0:l, :] = jnp.where(lane < A_QK_DIM, qr, 0.0).astype(BF16)
    q2_ref[0:l, :] = jnp.where(lane >= A_QK_DIM, qr, 0.0).astype(BF16)
    vs_ref[0:l, :] = v_ref[0]
    pad = jnp.zeros((lp - l, LANES), BF16)
    q1_ref[l:lp, :] = pad
    q2_ref[l:lp, :] = pad
    vs_ref[l:lp, :] = pad
    kr = rope(k_ref[0].astype(F32))
    for c in range(nblk):
        kt_ref[:, c * LANES:(c + 1) * LANES] = kr[c * LANES:(c + 1) * LANES, :].T.astype(BF16)
    k_tail = jnp.concatenate([kr[lr:l, :], jnp.zeros((lp - l, LANES), F32)], axis=0)
    kt_ref[:, lr:lp] = k_tail.T.astype(BF16)

    lam = lam_ref[:, 0:1]

    def scores(r0, s1_ref, s2_ref):
        rows = pl.ds(r0, LANES)
        k_all = kt_ref[...]
        s1_ref[...] = _dot(q1_ref[rows, :], k_all)
        s2_ref[...] = _dot(q2_ref[rows, :], k_all)

    def softmax_parts(s_ref, nrows):
        sm = s_ref[0:nrows, :lr]
        sl = s_ref[0:nrows, lr:] + neg_ref[...]
        m = jnp.maximum(jnp.max(sm, axis=1, keepdims=True), jnp.max(sl, axis=1, keepdims=True))
        pm = jnp.exp2(sm - m)
        pl_ = jnp.exp2(sl - m)
        tot = jnp.sum(pm, axis=1, keepdims=True) + jnp.sum(pl_, axis=1, keepdims=True)
        return pm, pl_, tot

    def finish(r0, nrows, s1_ref, s2_ref):
        p1m, p1l, t1 = softmax_parts(s1_ref, nrows)
        p2m, p2l, t2 = softmax_parts(s2_ref, nrows)
        c = lam * t1 / t2
        am = (p1m - p2m * c).astype(BF16)
        al = (p1l - p2l * c).astype(BF16)
        o = (_dot(am, vs_ref[0:lr, :]) + _dot(al, vs_ref[lr:lp, :])) * (1.0 / t1)
        o = o * lax.rsqrt(jnp.mean(o * o, axis=-1, keepdims=True) + EPS) * sw_ref[...]
        o_ref[0, pl.ds(r0, nrows), :] = o.astype(BF16)

    scores(0, sa1_ref, sa2_ref)

    pairs = _pick(nblk // 2, (8, 4, 2, 1))

    def body(j, c):
        for u in range(pairs):
            ra = pl.multiple_of((j * pairs + u) * (2 * LANES), LANES)
            rb = ra + LANES
            scores(rb, sb1_ref, sb2_ref)
            finish(ra, LANES, sa1_ref, sa2_ref)
            scores(rb + LANES, sa1_ref, sa2_ref)
            finish(rb, LANES, sb1_ref, sb2_ref)
        return c

    lax.fori_loop(0, nblk // (2 * pairs), body, 0)
    finish(lr, N_META, sa1_ref, sa2_ref)


def _attention(proj3, cos_t, sa_t, sb_t, neg_row, lam_row, sub_w, lr):
    b, l, _ = proj3.shape
    lp = lr + LANES
    const2 = lambda i, h: (0, 0)
    return pl.pallas_call(
        functools.partial(_attn_kernel, lr=lr),
        grid=(b, A_HEADS),
        in_specs=[
            pl.BlockSpec((1, l, LANES), lambda i, h: (i, 0, COL_AQ + h)),
            pl.BlockSpec((1, l, LANES), lambda i, h: (i, 0, COL_AK + h)),
            pl.BlockSpec((1, l, LANES), lambda i, h: (i, 0, COL_AV + h)),
            pl.BlockSpec((l, LANES), const2),
            pl.BlockSpec((l, LANES), const2),
            pl.BlockSpec((l, LANES), const2),
            pl.BlockSpec((1, LANES), const2),
            pl.BlockSpec((1, LANES), const2),
            pl.BlockSpec((1, LANES), const2),
        ],
        out_specs=pl.BlockSpec((1, l, LANES), lambda i, h: (i, 0, h)),
        out_shape=jax.ShapeDtypeStruct((b, l, A_HEADS * A_V_DIM), BF16),
        scratch_shapes=([pltpu.VMEM((lp, LANES), BF16)] * 3 + [pltpu.VMEM((LANES, lp), BF16)]
                        + [pltpu.VMEM((LANES, lp), F32)] * 4),
        compiler_params=pltpu.CompilerParams(
            dimension_semantics=("arbitrary", "arbitrary"), vmem_limit_bytes=VMEM_LIMIT),
        name="diff_attn",
    )(proj3, proj3, proj3, cos_t, sa_t, sb_t, neg_row, lam_row, sub_w)


def _log_sigmoid(x):
    return jnp.minimum(x, 0.0) - jnp.log(1.0 + jnp.exp(-jnp.abs(x)))


def _mlstm_kernel(q_ref, k_ref, v_ref, mo_ref, g_ref, gb_ref, nw_ref, o_ref,
                  c_ref, m_ref, kt_ref, h0_ref, u_ref, bcol_ref, amax_ref, aend_ref, bend_ref,
                  hf_ref, hb_ref, *, lr):
    l = lr + N_META
    t = CHUNK
    nc = lr // t
    dv = M_V_DIM
    scale = M_QK_DIM ** -0.5
    row = lax.broadcasted_iota(jnp.int32, (t, t), 0)
    col = lax.broadcasted_iota(jnp.int32, (t, t), 1)
    eye = row == col
    ones_tile = jnp.where(col == 0, 1.0, 0.0).astype(BF16)
    meta_r0 = l - t

    c_ref[...] = jnp.zeros(c_ref.shape, F32)
    m_ref[...] = jnp.zeros(m_ref.shape, F32)

    for c in range(nc + 1):
        r0 = c * t if c < nc else meta_r0
        kt_ref[:, c * t:(c + 1) * t] = k_ref[0, r0:r0 + t, :].astype(F32).T.astype(BF16)

    def scaled_q(r0):
        return (q_ref[0, pl.ds(r0, t), :].astype(F32) * scale).astype(BF16)

    def intra(ci, r0, g0):
        v_aug = jnp.concatenate([v_ref[0, pl.ds(r0, t), :], ones_tile], axis=1)
        kt = kt_ref[:, pl.ds(g0, t)]
        s_raw = _dot(scaled_q(r0), kt)
        for d, mask in ((0, col <= row), (1, col >= row)):
            gi = 2 * d
            ig = g_ref[0, 0, gi:gi + 1, pl.ds(g0, t)] + gb_ref[0, gi:gi + 1, :]
            lf = _log_sigmoid(g_ref[0, 0, gi + 1:gi + 2, pl.ds(g0, t)] + gb_ref[0, gi + 1:gi + 2, :])
            b_col = jnp.sum(jnp.where(mask, jnp.broadcast_to(lf, (t, t)), 0.0), axis=1, keepdims=True)
            b_row = jnp.sum(jnp.where(eye, jnp.broadcast_to(b_col, (t, t)), 0.0), axis=0, keepdims=True)
            a_row = ig - b_row
            dm = jnp.where(mask, jnp.broadcast_to(a_row, (t, t)), NEG_BIG)
            amax = jnp.max(dm, axis=1, keepdims=True)
            s0 = s_raw * jnp.exp(dm - amax)
            h0_ref[d, pl.ds(g0, t), :] = _dot(s0.astype(BF16), v_aug)
            a_end = jnp.max(a_row, axis=1, keepdims=True)
            kts = (kt.astype(F32) * jnp.exp(a_row - a_end)).astype(BF16)
            u_ref[d, ci] = _dot(kts, v_aug)
            bcol_ref[d, pl.ds(g0, t), :] = b_col
            amax_ref[d, pl.ds(g0, t), :] = amax
            aend_ref[d, ci] = jnp.broadcast_to(a_end, (8, LANES))
            bend_ref[d, ci] = jnp.broadcast_to(jnp.sum(lf, axis=1, keepdims=True), (8, LANES))

    ua = _pick(nc, (8, 4, 2, 1))

    def abody(i, c):
        for u in range(ua):
            ci = ua * i + u
            r0 = pl.multiple_of(ci * t, t)
            intra(ci, r0, r0)
        return c

    lax.fori_loop(0, nc // ua, abody, 0)
    intra(nc, meta_r0, lr)

    def step(d, ci, r0, g0):
        m_prev = m_ref[d, 0:1, 0:1]
        amax = amax_ref[d, pl.ds(g0, t), :]
        m_col = jnp.maximum(m_prev, amax)
        num = (jnp.exp(m_prev - m_col) * _dot(scaled_q(r0), c_ref[d].astype(BF16))
               + jnp.exp(amax - m_col) * h0_ref[d, pl.ds(g0, t), :])
        den = jnp.maximum(jnp.abs(num[:, dv:dv + 1]), jnp.exp(-(bcol_ref[d, pl.ds(g0, t), :] + m_col)))
        a_end = aend_ref[d, ci, 0:1, 0:1]
        m_end = jnp.maximum(m_prev, a_end)
        c_ref[d] = jnp.exp(m_prev - m_end) * c_ref[d] + jnp.exp(a_end - m_end) * u_ref[d, ci]
        m_ref[d] = jnp.broadcast_to(bend_ref[d, ci, 0:1, 0:1] + m_end, (8, LANES))
        return num[:, :dv] * (1.0 / den)

    hf_ref[lr:l, :] = step(0, nc, meta_r0, lr)[t - N_META:, :]

    ub = _pick(nc, (2, 1))

    def bbody(i, c):
        for u in range(ub):
            jf = ub * i + u
            rf = pl.multiple_of(jf * t, t)
            hf_ref[pl.ds(rf, t), :] = step(0, jf, rf, rf)
            jb = nc - 1 - jf
            rb = pl.multiple_of(jb * t, t)
            hb_ref[pl.ds(rb, t), :] = step(1, jb, rb, rb)
        return c

    lax.fori_loop(0, nc // ub, bbody, 0)
    hb_ref[lr:l, :] = step(1, nc, meta_r0, lr)[t - N_META:, :]

    def finish(r0, rows):
        hs = hf_ref[pl.ds(r0, rows), :] + hb_ref[pl.ds(r0, rows), :]
        y = hs * lax.rsqrt(jnp.mean(hs * hs, axis=-1, keepdims=True) + EPS) * nw_ref[0]
        gate = jax.nn.sigmoid(mo_ref[0, pl.ds(r0, rows), :].astype(F32))
        o_ref[0, pl.ds(r0, rows), :] = (y * gate).astype(BF16)

    def fbody(i, c):
        for u in range(ua):
            finish(pl.multiple_of((ua * i + u) * t, t), t)
        return c

    lax.fori_loop(0, nc // ua, fbody, 0)
    finish(lr, N_META)


def _mlstm(proj3, gates_row, gate_bias, norm_w, lr):
    b, l, _ = proj3.shape
    gl = gates_row.shape[-1]
    nw = lr // CHUNK + 1
    aug = M_V_DIM + LANES
    return pl.pallas_call(
        functools.partial(_mlstm_kernel, lr=lr),
        grid=(b, M_HEADS),
        in_specs=[
            pl.BlockSpec((1, l, M_QK_DIM), lambda i, h: (i, 0, COL_MQ + h)),
            pl.BlockSpec((1, l, M_QK_DIM), lambda i, h: (i, 0, COL_MK + h)),
            pl.BlockSpec((1, l, M_V_DIM), lambda i, h: (i, 0, COL_MV + h)),
            pl.BlockSpec((1, l, M_V_DIM), lambda i, h: (i, 0, COL_MO + h)),
            pl.BlockSpec((1, 1, 8, gl), lambda i, h: (i, h, 0, 0)),
            pl.BlockSpec((1, 8, LANES), lambda i, h: (h, 0, 0)),
            pl.BlockSpec((1, 1, M_V_DIM), lambda i, h: (h, 0, 0)),
        ],
        out_specs=pl.BlockSpec((1, l, M_V_DIM), lambda i, h: (i, 0, h)),
        out_shape=jax.ShapeDtypeStruct((b, l, M_HEADS * M_V_DIM), BF16),
        scratch_shapes=[
            pltpu.VMEM((2, M_QK_DIM, aug), F32),
            pltpu.VMEM((2, 8, LANES), F32),
            pltpu.VMEM((M_QK_DIM, gl), BF16),
            pltpu.VMEM((2, gl, aug), F32),
            pltpu.VMEM((2, nw, M_QK_DIM, aug), F32),
            pltpu.VMEM((2, gl, 1), F32),
            pltpu.VMEM((2, gl, 1), F32),
            pltpu.VMEM((2, nw, 8, LANES), F32),
            pltpu.VMEM((2, nw, 8, LANES), F32),
            pltpu.VMEM((l, M_V_DIM), F32), pltpu.VMEM((l, M_V_DIM), F32),
        ],
        compiler_params=pltpu.CompilerParams(
            dimension_semantics=("arbitrary", "arbitrary"), vmem_limit_bytes=VMEM_LIMIT),
        name="mlstm",
    )(proj3, proj3, proj3, proj3, gates_row, gate_bias, norm_w)


def _merge_kernel(oa_ref, hm_ref, ga_ref, gm_ref, h_ref, wa_ref, wm_ref, wo_ref, n2_ref,
                  rw_ref, rb_ref, tri_ref,
                  h1_ref, xn_ref, te_ref, tw_ref, rk_ref, cnt_ref, carry_ref, *, tm):
    @pl.when(pl.program_id(0) == 0)
    def _():
        carry_ref[...] = jnp.zeros(carry_ref.shape, F32)

    pa = _dot(oa_ref[...], wa_ref[...])
    pm = _dot(hm_ref[...], wm_ref[...])
    merged = (jax.nn.sigmoid(ga_ref[...].astype(F32)) * pa
              + jax.nn.sigmoid(gm_ref[...].astype(F32)) * pm)
    h1 = h_ref[...] + _dot(merged.astype(BF16), wo_ref[...])
    h1_ref[...] = h1
    xn32 = h1 * lax.rsqrt(jnp.mean(h1 * h1, axis=-1, keepdims=True) + EPS) * n2_ref[...]
    xn = xn32.astype(BF16)
    xn_ref[...] = _pack_rows(xn32)

    logits = _nt(rw_ref[...], xn) + rb_ref[:, 0:1]
    eidx = lax.broadcasted_iota(jnp.int32, (N_EXPERTS, tm), 0)
    work = logits
    vals, hots = [], []
    for _ in range(TOP_K):
        mx = jnp.max(work, axis=0, keepdims=True)
        sel = jnp.min(jnp.where(work == mx, eidx, N_EXPERTS), axis=0, keepdims=True)
        hot = eidx == sel
        vals.append(mx)
        hots.append(hot)
        work = jnp.where(hot, -jnp.inf, work)
    ex = [jnp.exp(v - vals[0]) for v in vals]
    inv = 1.0 / (ex[0] + ex[1] + ex[2] + ex[3])
    chosen = jnp.where(hots[0] | hots[1] | hots[2] | hots[3], 1.0, 0.0)

    carry = carry_ref[:, 0:1]
    ranks = []
    for s in range(tm // LANES):
        sub = chosen[:, s * LANES:(s + 1) * LANES]
        ranks.append(_dot(sub.astype(BF16), tri_ref[...]) + carry)
        carry = carry + jnp.sum(sub, axis=1, keepdims=True)
    before = jnp.concatenate(ranks, axis=1) if len(ranks) > 1 else ranks[0]
    carry_ref[...] = jnp.broadcast_to(carry, carry_ref.shape)
    cnt_ref[...] = jnp.broadcast_to(carry, cnt_ref.shape)

    for kk in range(TOP_K):
        te_ref[kk:kk + 1, :] = jnp.sum(jnp.where(hots[kk], eidx, 0), axis=0, keepdims=True)
        tw_ref[kk:kk + 1, :] = ex[kk] * inv
        rk_ref[kk:kk + 1, :] = jnp.sum(jnp.where(hots[kk], before, 0.0), axis=0,
                                       keepdims=True).astype(jnp.int32)


def _merge(o_a, h_m, proj, h0, w_a, w_m, w_o, norm2_w, rw_t, rb_col, tri):
    n = h0.shape[0]
    tm = _pick(n, (384, 256, 128))
    row = lambda i: (i, 0)
    const = lambda i: (0, 0)
    tok = lambda i: (0, i)
    wspec = pl.BlockSpec((D_MODEL, D_MODEL), const)
    return pl.pallas_call(
        functools.partial(_merge_kernel, tm=tm),
        grid=(n // tm,),
        in_specs=[
            pl.BlockSpec((tm, D_MODEL), row),
            pl.BlockSpec((tm, D_MODEL), row),
            pl.BlockSpec((tm, D_MODEL), lambda i: (i, COL_GA)),
            pl.BlockSpec((tm, D_MODEL), lambda i: (i, COL_GM)),
            pl.BlockSpec((tm, D_MODEL), row),
            wspec, wspec, wspec,
            pl.BlockSpec((1, D_MODEL), const),
            pl.BlockSpec((N_EXPERTS, D_MODEL), const),
            pl.BlockSpec((N_EXPERTS, LANES), const),
            pl.BlockSpec((LANES, LANES), const),
        ],
        out_specs=[
            pl.BlockSpec((tm, D_MODEL), row),
            pl.BlockSpec((tm, ROW_WORDS), row),
            pl.BlockSpec((TOP_K, tm), tok),
            pl.BlockSpec((TOP_K, tm), tok),
            pl.BlockSpec((TOP_K, tm), tok),
            pl.BlockSpec((N_EXPERTS, LANES), const),
        ],
        out_shape=[
            jax.ShapeDtypeStruct((n, D_MODEL), F32),
            jax.ShapeDtypeStruct((n, ROW_WORDS), jnp.int32),
            jax.ShapeDtypeStruct((TOP_K, n), jnp.int32),
            jax.ShapeDtypeStruct((TOP_K, n), F32),
            jax.ShapeDtypeStruct((TOP_K, n), jnp.int32),
            jax.ShapeDtypeStruct((N_EXPERTS, LANES), F32),
        ],
        scratch_shapes=[pltpu.VMEM((N_EXPERTS, LANES), F32)],
        compiler_params=pltpu.CompilerParams(
            dimension_semantics=("arbitrary",), vmem_limit_bytes=VMEM_LIMIT),
        name="merge_router",
    )(o_a, h_m, proj, proj, h0, w_a, w_m, w_o, norm2_w, rw_t, rb_col, tri)


def _expert_kernel(be_ref, nu_ref, x_ref, wg_ref, bg_ref, wu_ref, bu_ref, wd_ref, bd_ref, y_ref,
                   wg_s, wu_s, wd_s):
    i = pl.program_id(0)
    used = i < nu_ref[0]

    @pl.when(jnp.logical_or(i == 0, be_ref[i] != be_ref[jnp.maximum(i - 1, 0)]))
    def _():
        wg_s[...] = wg_ref[0].astype(BF16)
        wu_s[...] = wu_ref[0].astype(BF16)
        wd_s[...] = wd_ref[0].astype(BF16)

    @pl.when(used)
    def _():
        x = _unpack_rows(x_ref[...]).astype(BF16)
        gt = jnp.minimum(_dot(x, wg_s[...]) + bg_ref[0], SWIGLU_LIMIT)
        up = jnp.clip(_dot(x, wu_s[...]) + bu_ref[0], -SWIGLU_LIMIT, SWIGLU_LIMIT)
        glu = gt * jax.nn.sigmoid(SWIGLU_ALPHA * gt)
        act = ((up + 1.0) * glu).astype(BF16)
        y_ref[...] = _pack_rows(_dot(act, wd_s[...]) + bd_ref[0])

    @pl.when(jnp.logical_not(used))
    def _():
        y_ref[...] = jnp.zeros(y_ref.shape, jnp.int32)


def _experts(blk_e, n_used, xs, w_g, b_g, w_u, b_u, w_d, b_d):
    cap = xs.shape[0]
    n_blocks = cap // MOE_ROWS
    wspec = lambda d0, d1: pl.BlockSpec((1, d0, d1), lambda i, be, nu: (be[i], 0, 0))
    return pl.pallas_call(
        _expert_kernel,
        grid_spec=pltpu.PrefetchScalarGridSpec(
            num_scalar_prefetch=2,
            grid=(n_blocks,),
            in_specs=[
                pl.BlockSpec((MOE_ROWS, ROW_WORDS), lambda i, be, nu: (i, 0)),
                wspec(D_MODEL, D_FF), wspec(1, D_FF),
                wspec(D_MODEL, D_FF), wspec(1, D_FF),
                wspec(D_FF, D_MODEL), wspec(1, D_MODEL),
            ],
            out_specs=pl.BlockSpec((MOE_ROWS, ROW_WORDS), lambda i, be, nu: (i, 0)),
            scratch_shapes=[pltpu.VMEM((D_MODEL, D_FF), BF16), pltpu.VMEM((D_MODEL, D_FF), BF16),
                            pltpu.VMEM((D_FF, D_MODEL), BF16)],
        ),
        out_shape=jax.ShapeDtypeStruct((cap, ROW_WORDS), jnp.int32),
        compiler_params=pltpu.CompilerParams(
            dimension_semantics=("arbitrary",), vmem_limit_bytes=VMEM_LIMIT),
        name="experts",
    )(blk_e, n_used, xs, w_g, b_g, w_u, b_u, w_d, b_d)


def _sc_worker_windows(n_win):
    return -(-n_win // (SC_CORES * SC_SUBCORES))


def _sc_dispatch(x_words, dest3, cap):
    n = x_words.shape[0]
    n_win = n // SC_WINDOW
    per = _sc_worker_windows(n_win)

    def body(x_hbm, d_hbm, o_hbm, idx_v, rows_v):
        wid = lax.axis_index("s") * SC_CORES + lax.axis_index("c")

        @pl.loop(0, per)
        def _(i):
            win = jnp.minimum(wid * per + i, n_win - 1)
            pltpu.sync_copy(d_hbm.at[win], idx_v)
            pltpu.sync_copy(x_hbm.at[pl.ds(win * SC_WINDOW, SC_WINDOW)], rows_v)
            for kk in range(TOP_K):
                pltpu.sync_copy(rows_v, o_hbm.at[idx_v.at[kk]])

    return pl.kernel(
        body,
        out_type=jax.ShapeDtypeStruct((cap, ROW_WORDS), jnp.int32),
        mesh=plsc.VectorSubcoreMesh(core_axis_name="c", subcore_axis_name="s"),
        scratch_types=[pltpu.VMEM((TOP_K, SC_WINDOW), jnp.int32),
                       pltpu.VMEM((SC_WINDOW, ROW_WORDS), jnp.int32)],
        name="moe_dispatch",
    )(x_words, dest3)


def _sc_gather(y_words, idx2):
    n_win = idx2.shape[0]
    per = _sc_worker_windows(n_win)

    def body(y_hbm, i_hbm, o_hbm, idx_v, rows_v, sem):
        wid = lax.axis_index("s") * SC_CORES + lax.axis_index("c")

        @pl.loop(0, per)
        def _(i):
            win = jnp.minimum(wid * per + i, n_win - 1)
            pltpu.sync_copy(i_hbm.at[win], idx_v)
            pltpu.async_copy(y_hbm.at[idx_v], rows_v, sem).wait()
            pltpu.sync_copy(rows_v, o_hbm.at[pl.ds(win * SC_WINDOW, SC_WINDOW)])

    return pl.kernel(
        body,
        out_type=jax.ShapeDtypeStruct((n_win * SC_WINDOW, ROW_WORDS), jnp.int32),
        mesh=plsc.VectorSubcoreMesh(core_axis_name="c", subcore_axis_name="s"),
        scratch_types=[pltpu.VMEM((SC_WINDOW,), jnp.int32),
                       pltpu.VMEM((SC_WINDOW, ROW_WORDS), jnp.int32),
                       pltpu.SemaphoreType.DMA],
        name="moe_gather",
    )(y_words, idx2)


def _combine_kernel(h_ref, y_ref, w_ref, fw_ref, o_ref):
    acc = h_ref[0]
    for kk in range(TOP_K):
        acc = acc + w_ref[0, :, kk:kk + 1] * _unpack_rows(y_ref[kk, 0])
    o_ref[0] = acc * lax.rsqrt(jnp.mean(acc * acc, axis=-1, keepdims=True) + EPS) * fw_ref[...]


def _combine(h1, yg, wts, final_w, b0, nb, lr):
    tr = _pick(lr, (1024, 512, 256, 128))
    return pl.pallas_call(
        _combine_kernel,
        grid=(nb, lr // tr),
        in_specs=[
            pl.BlockSpec((1, tr, D_MODEL), lambda i, j: (b0 + i, j, 0)),
            pl.BlockSpec((TOP_K, 1, tr, ROW_WORDS), lambda i, j: (0, b0 + i, j, 0)),
            pl.BlockSpec((1, tr, TOP_K), lambda i, j: (b0 + i, j, 0)),
            pl.BlockSpec((1, D_MODEL), lambda i, j: (0, 0)),
        ],
        out_specs=pl.BlockSpec((1, tr, D_MODEL), lambda i, j: (i, j, 0)),
        out_shape=jax.ShapeDtypeStruct((nb, lr, D_MODEL), F32),
        compiler_params=pltpu.CompilerParams(
            dimension_semantics=("arbitrary", "arbitrary"), vmem_limit_bytes=VMEM_LIMIT),
        name="combine",
    )(h1, yg, wts, final_w)


def _rope_tables(lr):
    l = lr + N_META
    half = ROT_DIM // 2
    pos = jnp.concatenate([jnp.arange(N_META, l, dtype=F32), jnp.arange(N_META, dtype=F32)])
    inv_freq = ROPE_THETA ** (-jnp.arange(0, ROT_DIM, 2, dtype=F32) / ROT_DIM)
    ang = pos[:, None] * inv_freq[None, :]
    cos, sin = jnp.cos(ang), jnp.sin(ang)
    z = jnp.zeros((l, A_QK_DIM - ROT_DIM), F32)
    zh = jnp.zeros((l, half), F32)
    cos64 = jnp.concatenate([cos, cos, z + 1.0], axis=1)
    sa64 = jnp.concatenate([-sin, zh, z], axis=1)
    sb64 = jnp.concatenate([zh, sin, z], axis=1)
    dup = lambda a: jnp.concatenate([a, a], axis=1)
    return dup(cos64), dup(sa64), dup(sb64)


def kernel(x_prompt, x_sample, meta_tokens, norm1_w, w_in, lambda_q1, lambda_k1, lambda_q2, lambda_k2, attn_subln_w, mlstm_gate_b, mlstm_norm_w, w_br_attn, w_br_mlstm, w_out, norm2_w, router_w, router_b, w_gate, b_gate, w_up, b_up, w_down, b_down, final_norm_w):
    lr = x_prompt.shape[1]
    l = lr + N_META
    assert x_sample.shape[1:] == x_prompt.shape[1:] == (lr, D_MODEL)
    assert lr % (2 * LANES) == 0, "attention pipelines query blocks in pairs; mLSTM chunks are 128 rows"

    w_in0 = w_in[0]
    g0, g1 = OFF_GATES, OFF_GATES + 4 * M_HEADS
    w_main = jnp.concatenate([w_in0[:, :g0], w_in0[:, g1:]], axis=1).astype(BF16)
    w_gates = jnp.pad(w_in0[:, g0:g1], ((0, 0), (0, LANES - 4 * M_HEADS))).astype(BF16)
    lam = (jnp.exp(jnp.sum(lambda_q1[0] * lambda_k1[0])) - jnp.exp(jnp.sum(lambda_q2[0] * lambda_k2[0]))
           + LAM_INIT)
    lam_row = jnp.full((1, LANES), lam, F32)
    sub_w = (attn_subln_w[0] * (1.0 - LAM_INIT)).reshape(1, A_V_DIM)
    neg_row = jnp.where(jnp.arange(LANES) < N_META, 0.0, NEG_BIG).astype(F32).reshape(1, LANES)
    cos_t, sa_t, sb_t = _rope_tables(lr)
    gate_bias = jnp.broadcast_to(
        jnp.pad(mlstm_gate_b[0].T, ((0, 0), (0, 4)))[:, :, None], (M_HEADS, 8, LANES)).astype(F32)
    norm_w_m = mlstm_norm_w[0].reshape(M_HEADS, 1, M_V_DIM)
    rw_t = router_w[0].T.astype(BF16)
    rb_col = jnp.broadcast_to(router_b[0][:, None], (N_EXPERTS, LANES)).astype(F32)
    tri = (jnp.arange(LANES)[:, None] < jnp.arange(LANES)[None, :]).astype(BF16)

    w_a, w_m, w_o = w_br_attn[0].astype(BF16), w_br_mlstm[0].astype(BF16), w_out[0].astype(BF16)
    n1, n2, fw = norm1_w[0].reshape(1, D_MODEL), norm2_w[0].reshape(1, D_MODEL), final_norm_w.reshape(1, D_MODEL)
    bg3, bu3, bd3 = (b_gate[0].reshape(N_EXPERTS, 1, D_FF), b_up[0].reshape(N_EXPERTS, 1, D_FF),
                     b_down[0].reshape(N_EXPERTS, 1, D_MODEL))

    def trunk(x):
        b = x.shape[0]
        n = b * l
        assert n % SC_WINDOW == 0, "SparseCore dispatch works on whole 128-token windows"
        meta_b = jnp.broadcast_to(meta_tokens[None].astype(x.dtype), (b, N_META, D_MODEL))
        h0 = jnp.concatenate([x, meta_b], axis=1).reshape(n, D_MODEL)

        proj, gates = _inproj(h0, n1, w_main, w_gates)
        proj3 = proj.reshape(b, l, N_MAIN)
        o_a = _attention(proj3, cos_t, sa_t, sb_t, neg_row, lam_row, sub_w, lr)

        g4 = gates.reshape(b, l, 4, M_HEADS).transpose(0, 3, 2, 1)
        off = jnp.array([NEG_BIG, 1e4, NEG_BIG, 1e4], F32)[None, None, :, None]
        fill = jnp.broadcast_to(off, (b, M_HEADS, 4, CHUNK - N_META))
        g_row = jnp.concatenate([g4[..., :lr], fill, g4[..., lr:]], axis=-1)
        g_row = jnp.pad(g_row, ((0, 0), (0, 0), (0, 4), (0, 0)))
        h_m = _mlstm(proj3, g_row, gate_bias, norm_w_m, lr)

        h1, xn2, top_e, top_w, rank, cnt = _merge(
            o_a.reshape(n, D_MODEL), h_m.reshape(n, D_MODEL), proj, h0,
            w_a, w_m, w_o, n2, rw_t, rb_col, tri)

        counts = cnt[:, 0].astype(jnp.int32)
        padded = (counts + MOE_ROWS - 1) // MOE_ROWS * MOE_ROWS
        pends = jnp.cumsum(padded)
        pstarts = pends - padded
        e_ids = jnp.arange(N_EXPERTS, dtype=jnp.int32)
        pstart_of = jnp.sum(jnp.where(top_e[..., None] == e_ids, pstarts, 0), axis=-1)
        dest = pstart_of + rank
        n_blocks = (n * TOP_K + N_EXPERTS * (MOE_ROWS - 1) + MOE_ROWS - 1) // MOE_ROWS
        cap = n_blocks * MOE_ROWS
        blk_start = jnp.arange(n_blocks, dtype=jnp.int32) * MOE_ROWS
        blk_e = jnp.minimum(jnp.sum((pends[None, :] <= blk_start[:, None]).astype(jnp.int32), axis=1),
                            N_EXPERTS - 1)
        n_used = (pends[-1:] // MOE_ROWS).astype(jnp.int32)

        dest_win = dest.reshape(TOP_K, n // SC_WINDOW, SC_WINDOW).transpose(1, 0, 2)
        xs = _sc_dispatch(xn2, dest_win, cap)
        y = _experts(blk_e, n_used, xs, w_gate[0], bg3, w_up[0], bu3, w_down[0], bd3)
        yg = _sc_gather(y, dest.reshape(TOP_K * n // SC_WINDOW, SC_WINDOW)).reshape(TOP_K, b, l, ROW_WORDS)
        return _combine(h1.reshape(b, l, D_MODEL), yg, top_w.T.reshape(b, l, TOP_K), fw, 0, b, lr)

    return (trunk(x_prompt), trunk(x_sample))
```

```python
import functools
import math

import jax
import jax.numpy as jnp
from jax import lax
from jax.experimental import pallas as pl
from jax.experimental.pallas import tpu as pltpu
from jax.experimental.pallas import tpu_sc as plsc

F32 = jnp.float32
BF16 = jnp.bfloat16

D_MODEL = 1024
N_META = 16
EPS = 1e-6
A_HEADS = 8
A_QK_DIM = 64
A_V_DIM = 128
ROT_DIM = 16
ROPE_THETA = 500000.0
M_HEADS = 4
M_QK_DIM = 128
M_V_DIM = 256
CHUNK = 128
N_EXPERTS = 32
TOP_K = 4
D_FF = 1024
SWIGLU_LIMIT = 7.0
SWIGLU_ALPHA = 1.702
LAM_INIT = 0.8 - 0.6 * math.exp(-0.3 * 0)

LANES = 128
NEG_BIG = -1e30
MOE_ROWS = 512
SC_CORES = 2
SC_SUBCORES = 16
SC_WINDOW = 128
ROW_WORDS = D_MODEL // 2
W_AQK = A_HEADS * 2 * A_QK_DIM
W_AV = A_HEADS * A_V_DIM
W_MQK = M_HEADS * M_QK_DIM
W_MV = M_HEADS * M_V_DIM
OFF_GATES = 2 * W_AQK + W_AV + 2 * W_MQK + 2 * W_MV
N_MAIN = OFF_GATES + 2 * D_MODEL
COL_AQ, COL_AK, COL_AV = 0, W_AQK // LANES, 2 * W_AQK // LANES
COL_MQ = (2 * W_AQK + W_AV) // M_QK_DIM
COL_MK = COL_MQ + M_HEADS
COL_MV = (2 * W_AQK + W_AV + 2 * W_MQK) // M_V_DIM
COL_MO = COL_MV + M_HEADS
COL_GA = OFF_GATES // D_MODEL
COL_GM = COL_GA + 1

VMEM_LIMIT = 56 * 1024 * 1024


def _pick(n, prefs):
    for p in prefs:
        if n % p == 0:
            return p
    raise ValueError(f"no tile for {n}")


def _nt(a, b):
    return lax.dot_general(a, b, (((1,), (1,)), ((), ())), preferred_element_type=F32)


def _dot(a, b):
    return jnp.dot(a, b, preferred_element_type=F32)


def _pack_rows(x):
    w = x.shape[1] // 2
    bits = lambda v: lax.bitcast_convert_type(v.astype(BF16).astype(F32), jnp.uint32)
    lo = lax.shift_right_logical(bits(x[:, :w]), jnp.uint32(16))
    hi = bits(x[:, w:]) & jnp.uint32(0xFFFF0000)
    return lax.bitcast_convert_type(lo | hi, jnp.int32)


def _unpack_rows(wds):
    u = lax.bitcast_convert_type(wds, jnp.uint32)
    lo = lax.bitcast_convert_type(lax.shift_left(u, jnp.uint32(16)), F32)
    hi = lax.bitcast_convert_type(u & jnp.uint32(0xFFFF0000), F32)
    return jnp.concatenate([lo, hi], axis=1)


def _inproj_kernel(x_ref, n1_ref, w_ref, wg_ref, o_ref, g_ref, xn_ref):
    @pl.when(pl.program_id(1) == 0)
    def _():
        x = x_ref[...]
        ms = jnp.mean(x * x, axis=-1, keepdims=True)
        xn = (x * lax.rsqrt(ms + EPS) * n1_ref[...]).astype(BF16)
        xn_ref[...] = xn
        g_ref[...] = _dot(xn, wg_ref[...])[:, :4 * M_HEADS]

    o_ref[...] = _dot(xn_ref[...], w_ref[...]).astype(BF16)


def _inproj(h0, norm1_w, w_main, w_gates):
    n = h0.shape[0]
    tm = _pick(n, (1152, 768, 512, 384, 256, 128))
    tn = 2048
    return pl.pallas_call(
        _inproj_kernel,
        grid=(n // tm, N_MAIN // tn),
        in_specs=[
            pl.BlockSpec((tm, D_MODEL), lambda i, j: (i, 0)),
            pl.BlockSpec((1, D_MODEL), lambda i, j: (0, 0)),
            pl.BlockSpec((D_MODEL, tn), lambda i, j: (0, j)),
            pl.BlockSpec((D_MODEL, LANES), lambda i, j: (0, 0)),
        ],
        out_specs=[
            pl.BlockSpec((tm, tn), lambda i, j: (i, j)),
            pl.BlockSpec((tm, 4 * M_HEADS), lambda i, j: (i, 0)),
        ],
        out_shape=[
            jax.ShapeDtypeStruct((n, N_MAIN), BF16),
            jax.ShapeDtypeStruct((n, 4 * M_HEADS), F32),
        ],
        scratch_shapes=[pltpu.VMEM((tm, D_MODEL), BF16)],
        compiler_params=pltpu.CompilerParams(
            dimension_semantics=("arbitrary", "arbitrary"), vmem_limit_bytes=VMEM_LIMIT),
        name="inproj",
    )(h0, norm1_w, w_main, w_gates)


def _attn_kernel(q_ref, k_ref, v_ref, cos_ref, sa_ref, sb_ref, neg_ref, lam_ref, sw_ref,
                 o_ref, q1_ref, q2_ref, vs_ref, kt_ref, sa1_ref, sa2_ref, sb1_ref, sb2_ref, *, lr):
    l = lr + N_META
    lp = lr + LANES
    nblk = lr // LANES
    scale = A_QK_DIM ** -0.5 * math.log2(math.e)

    def rope(t):
        return (t * cos_ref[...] + pltpu.roll(t, LANES - ROT_DIM // 2, 1) * sa_ref[...]
                + pltpu.roll(t, ROT_DIM // 2, 1) * sb_ref[...])

    lane = lax.broadcasted_iota(jnp.int32, (l, LANES), 1)
    qr = rope(q_ref[0].astype(F32)) * scale
    q1_ref[0:l, :] = jnp.where(lane < A_QK_DIM, qr, 0.0).astype(BF16)
    q2_ref[0:l, :] = jnp.where(lane >= A_QK_DIM, qr, 0.0).astype(BF16)
    vs_ref[0:l, :] = v_ref[0]
    pad = jnp.zeros((lp - l, LANES), BF16)
    q1_ref[l:lp, :] = pad
    q2_ref[l:lp, :] = pad
    vs_ref[l:lp, :] = pad
    kr = rope(k_ref[0].astype(F32))
    for c in range(nblk):
        kt_ref[:, c * LANES:(c + 1) * LANES] = kr[c * LANES:(c + 1) * LANES, :].T.astype(BF16)
    k_tail = jnp.concatenate([kr[lr:l, :], jnp.zeros((lp - l, LANES), F32)], axis=0)
    kt_ref[:, lr:lp] = k_tail.T.astype(BF16)

    lam = lam_ref[:, 0:1]

    def scores(r0, s1_ref, s2_ref):
        rows = pl.ds(r0, LANES)
        k_all = kt_ref[...]
        s1_ref[...] = _dot(q1_ref[rows, :], k_all)
        s2_ref[...] = _dot(q2_ref[rows, :], k_all)

    def softmax_parts(s_ref, nrows):
        sm = s_ref[0:nrows, :lr]
        sl = s_ref[0:nrows, lr:] + neg_ref[...]
        m = jnp.maximum(jnp.max(sm, axis=1, keepdims=True), jnp.max(sl, axis=1, keepdims=True))
        pm = jnp.exp2(sm - m)
        pl_ = jnp.exp2(sl - m)
        tot = jnp.sum(pm, axis=1, keepdims=True) + jnp.sum(pl_, axis=1, keepdims=True)
        return pm, pl_, tot

    def finish(r0, nrows, s1_ref, s2_ref):
        p1m, p1l, t1 = softmax_parts(s1_ref, nrows)
        p2m, p2l, t2 = softmax_parts(s2_ref, nrows)
        c = lam * t1 / t2
        am = (p1m - p2m * c).astype(BF16)
        al = (p1l - p2l * c).astype(BF16)
        o = (_dot(am, vs_ref[0:lr, :]) + _dot(al, vs_ref[lr:lp, :])) * (1.0 / t1)
        o = o * lax.rsqrt(jnp.mean(o * o, axis=-1, keepdims=True) + EPS) * sw_ref[...]
        o_ref[0, pl.ds(r0, nrows), :] = o.astype(BF16)

    scores(0, sa1_ref, sa2_ref)

    pairs = _pick(nblk // 2, (8, 4, 2, 1))

    def body(j, c):
        for u in range(pairs):
            ra = pl.multiple_of((j * pairs + u) * (2 * LANES), LANES)
            rb = ra + LANES
            scores(rb, sb1_ref, sb2_ref)
            finish(ra, LANES, sa1_ref, sa2_ref)
            scores(rb + LANES, sa1_ref, sa2_ref)
            finish(rb, LANES, sb1_ref, sb2_ref)
        return c

    lax.fori_loop(0, nblk // (2 * pairs), body, 0)
    finish(lr, N_META, sa1_ref, sa2_ref)


def _attention(proj3, cos_t, sa_t, sb_t, neg_row, lam_row, sub_w, lr):
    b, l, _ = proj3.shape
    lp = lr + LANES
    const2 = lambda i, h: (0, 0)
    return pl.pallas_call(
        functools.partial(_attn_kernel, lr=lr),
        grid=(b, A_HEADS),
        in_specs=[
            pl.BlockSpec((1, l, LANES), lambda i, h: (i, 0, COL_AQ + h)),
            pl.BlockSpec((1, l, LANES), lambda i, h: (i, 0, COL_AK + h)),
            pl.BlockSpec((1, l, LANES), lambda i, h: (i, 0, COL_AV + h)),
            pl.BlockSpec((l, LANES), const2),
            pl.BlockSpec((l, LANES), const2),
            pl.BlockSpec((l, LANES), const2),
            pl.BlockSpec((1, LANES), const2),
            pl.BlockSpec((1, LANES), const2),
            pl.BlockSpec((1, LANES), const2),
        ],
        out_specs=pl.BlockSpec((1, l, LANES), lambda i, h: (i, 0, h)),
        out_shape=jax.ShapeDtypeStruct((b, l, A_HEADS * A_V_DIM), BF16),
        scratch_shapes=([pltpu.VMEM((lp, LANES), BF16)] * 3 + [pltpu.VMEM((LANES, lp), BF16)]
                        + [pltpu.VMEM((LANES, lp), F32)] * 4),
        compiler_params=pltpu.CompilerParams(
            dimension_semantics=("arbitrary", "arbitrary"), vmem_limit_bytes=VMEM_LIMIT),
        name="diff_attn",
    )(proj3, proj3, proj3, cos_t, sa_t, sb_t, neg_row, lam_row, sub_w)


def _log_sigmoid(x):
    return jnp.minimum(x, 0.0) - jnp.log(1.0 + jnp.exp(-jnp.abs(x)))


def _mlstm_kernel(q_ref, k_ref, v_ref, mo_ref, g_ref, gb_ref, nw_ref, o_ref,
                  c_ref, m_ref, kt_ref, h0_ref, u_ref, bcol_ref, amax_ref, aend_ref, bend_ref,
                  hf_ref, hb_ref, *, lr):
    l = lr + N_META
    t = CHUNK
    nc = lr // t
    dv = M_V_DIM
    scale = M_QK_DIM ** -0.5
    row = lax.broadcasted_iota(jnp.int32, (t, t), 0)
    col = lax.broadcasted_iota(jnp.int32, (t, t), 1)
    eye = row == col
    ones_tile = jnp.where(col == 0, 1.0, 0.0).astype(BF16)
    meta_r0 = l - t

    c_ref[...] = jnp.zeros(c_ref.shape, F32)
    m_ref[...] = jnp.zeros(m_ref.shape, F32)

    for c in range(nc + 1):
        r0 = c * t if c < nc else meta_r0
        kt_ref[:, c * t:(c + 1) * t] = k_ref[0, r0:r0 + t, :].astype(F32).T.astype(BF16)

    def scaled_q(r0):
        return (q_ref[0, pl.ds(r0, t), :].astype(F32) * scale).astype(BF16)

    def intra(ci, r0, g0):
        v_aug = jnp.concatenate([v_ref[0, pl.ds(r0, t), :], ones_tile], axis=1)
        kt = kt_ref[:, pl.ds(g0, t)]
        s_raw = _dot(scaled_q(r0), kt)
        for d, mask in ((0, col <= row), (1, col >= row)):
            gi = 2 * d
            ig = g_ref[0, 0, gi:gi + 1, pl.ds(g0, t)] + gb_ref[0, gi:gi + 1, :]
            lf = _log_sigmoid(g_ref[0, 0, gi + 1:gi + 2, pl.ds(g0, t)] + gb_ref[0, gi + 1:gi + 2, :])
            b_col = jnp.sum(jnp.where(mask, jnp.broadcast_to(lf, (t, t)), 0.0), axis=1, keepdims=True)
            b_row = jnp.sum(jnp.where(eye, jnp.broadcast_to(b_col, (t, t)), 0.0), axis=0, keepdims=True)
            a_row = ig - b_row
            dm = jnp.where(mask, jnp.broadcast_to(a_row, (t, t)), NEG_BIG)
            amax = jnp.max(dm, axis=1, keepdims=True)
            s0 = s_raw * jnp.exp(dm - amax)
            h0_ref[d, pl.ds(g0, t), :] = _dot(s0.astype(BF16), v_aug)
            a_end = jnp.max(a_row, axis=1, keepdims=True)
            kts = (kt.astype(F32) * jnp.exp(a_row - a_end)).astype(BF16)
            u_ref[d, ci] = _dot(kts, v_aug)
            bcol_ref[d, pl.ds(g0, t), :] = b_col
            amax_ref[d, pl.ds(g0, t), :] = amax
            aend_ref[d, ci] = jnp.broadcast_to(a_end, (8, LANES))
            bend_ref[d, ci] = jnp.broadcast_to(jnp.sum(lf, axis=1, keepdims=True), (8, LANES))

    ua = _pick(nc, (8, 4, 2, 1))

    def abody(i, c):
        for u in range(ua):
            ci = ua * i + u
            r0 = pl.multiple_of(ci * t, t)
            intra(ci, r0, r0)
        return c

    lax.fori_loop(0, nc // ua, abody, 0)
    intra(nc, meta_r0, lr)

    def step(d, ci, r0, g0):
        m_prev = m_ref[d, 0:1, 0:1]
        amax = amax_ref[d, pl.ds(g0, t), :]
        m_col = jnp.maximum(m_prev, amax)
        num = (jnp.exp(m_prev - m_col) * _dot(scaled_q(r0), c_ref[d].astype(BF16))
               + jnp.exp(amax - m_col) * h0_ref[d, pl.ds(g0, t), :])
        den = jnp.maximum(jnp.abs(num[:, dv:dv + 1]), jnp.exp(-(bcol_ref[d, pl.ds(g0, t), :] + m_col)))
        a_end = aend_ref[d, ci, 0:1, 0:1]
        m_end = jnp.maximum(m_prev, a_end)
        c_ref[d] = jnp.exp(m_prev - m_end) * c_ref[d] + jnp.exp(a_end - m_end) * u_ref[d, ci]
        m_ref[d] = jnp.broadcast_to(bend_ref[d, ci, 0:1, 0:1] + m_end, (8, LANES))
        return num[:, :dv] * (1.0 / den)

    hf_ref[lr:l, :] = step(0, nc, meta_r0, lr)[t - N_META:, :]

    ub = _pick(nc, (2, 1))

    def bbody(i, c):
        for u in range(ub):
            jf = ub * i + u
            rf = pl.multiple_of(jf * t, t)
            hf_ref[pl.ds(rf, t), :] = step(0, jf, rf, rf)
            jb = nc - 1 - jf
            rb = pl.multiple_of(jb * t, t)
            hb_ref[pl.ds(rb, t), :] = step(1, jb, rb, rb)
        return c

    lax.fori_loop(0, nc // ub, bbody, 0)
    hb_ref[lr:l, :] = step(1, nc, meta_r0, lr)[t - N_META:, :]

    def finish(r0, rows):
        hs = hf_ref[pl.ds(r0, rows), :] + hb_ref[pl.ds(r0, rows), :]
        y = hs * lax.rsqrt(jnp.mean(hs * hs, axis=-1, keepdims=True) + EPS) * nw_ref[0]
        gate = jax.nn.sigmoid(mo_ref[0, pl.ds(r0, rows), :].astype(F32))
        o_ref[0, pl.ds(r0, rows), :] = (y * gate).astype(BF16)

    def fbody(i, c):
        for u in range(ua):
            finish(pl.multiple_of((ua * i + u) * t, t), t)
        return c

    lax.fori_loop(0, nc // ua, fbody, 0)
    finish(lr, N_META)


def _mlstm(proj3, gates_row, gate_bias, norm_w, lr):
    b, l, _ = proj3.shape
    gl = gates_row.shape[-1]
    nw = lr // CHUNK + 1
    aug = M_V_DIM + LANES
    return pl.pallas_call(
        functools.partial(_mlstm_kernel, lr=lr),
        grid=(b, M_HEADS),
        in_specs=[
            pl.BlockSpec((1, l, M_QK_DIM), lambda i, h: (i, 0, COL_MQ + h)),
            pl.BlockSpec((1, l, M_QK_DIM), lambda i, h: (i, 0, COL_MK + h)),
            pl.BlockSpec((1, l, M_V_DIM), lambda i, h: (i, 0, COL_MV + h)),
            pl.BlockSpec((1, l, M_V_DIM), lambda i, h: (i, 0, COL_MO + h)),
            pl.BlockSpec((1, 1, 8, gl), lambda i, h: (i, h, 0, 0)),
            pl.BlockSpec((1, 8, LANES), lambda i, h: (h, 0, 0)),
            pl.BlockSpec((1, 1, M_V_DIM), lambda i, h: (h, 0, 0)),
        ],
        out_specs=pl.BlockSpec((1, l, M_V_DIM), lambda i, h: (i, 0, h)),
        out_shape=jax.ShapeDtypeStruct((b, l, M_HEADS * M_V_DIM), BF16),
        scratch_shapes=[
            pltpu.VMEM((2, M_QK_DIM, aug), F32),
            pltpu.VMEM((2, 8, LANES), F32),
            pltpu.VMEM((M_QK_DIM, gl), BF16),
            pltpu.VMEM((2, gl, aug), F32),
            pltpu.VMEM((2, nw, M_QK_DIM, aug), F32),
            pltpu.VMEM((2, gl, 1), F32),
            pltpu.VMEM((2, gl, 1), F32),
            pltpu.VMEM((2, nw, 8, LANES), F32),
            pltpu.VMEM((2, nw, 8, LANES), F32),
            pltpu.VMEM((l, M_V_DIM), F32), pltpu.VMEM((l, M_V_DIM), F32),
        ],
        compiler_params=pltpu.CompilerParams(
            dimension_semantics=("arbitrary", "arbitrary"), vmem_limit_bytes=VMEM_LIMIT),
        name="mlstm",
    )(proj3, proj3, proj3, proj3, gates_row, gate_bias, norm_w)


def _merge_kernel(oa_ref, hm_ref, ga_ref, gm_ref, h_ref, wa_ref, wm_ref, wo_ref, n2_ref,
                  rw_ref, rb_ref, tri_ref,
                  h1_ref, xn_ref, te_ref, tw_ref, rk_ref, cnt_ref, carry_ref, *, tm):
    @pl.when(pl.program_id(0) == 0)
    def _():
        carry_ref[...] = jnp.zeros(carry_ref.shape, F32)

    pa = _dot(oa_ref[...], wa_ref[...])
    pm = _dot(hm_ref[...], wm_ref[...])
    merged = (jax.nn.sigmoid(ga_ref[...].astype(F32)) * pa
              + jax.nn.sigmoid(gm_ref[...].astype(F32)) * pm)
    h1 = h_ref[...] + _dot(merged.astype(BF16), wo_ref[...])
    h1_ref[...] = h1
    xn32 = h1 * lax.rsqrt(jnp.mean(h1 * h1, axis=-1, keepdims=True) + EPS) * n2_ref[...]
    xn = xn32.astype(BF16)
    xn_ref[...] = _pack_rows(xn32)

    logits = _nt(rw_ref[...], xn) + rb_ref[:, 0:1]
    eidx = lax.broadcasted_iota(jnp.int32, (N_EXPERTS, tm), 0)
    work = logits
    vals, hots = [], []
    for _ in range(TOP_K):
        mx = jnp.max(work, axis=0, keepdims=True)
        sel = jnp.min(jnp.where(work == mx, eidx, N_EXPERTS), axis=0, keepdims=True)
        hot = eidx == sel
        vals.append(mx)
        hots.append(hot)
        work = jnp.where(hot, -jnp.inf, work)
    ex = [jnp.exp(v - vals[0]) for v in vals]
    inv = 1.0 / (ex[0] + ex[1] + ex[2] + ex[3])
    chosen = jnp.where(hots[0] | hots[1] | hots[2] | hots[3], 1.0, 0.0)

    carry = carry_ref[:, 0:1]
    ranks = []
    for s in range(tm // LANES):
        sub = chosen[:, s * LANES:(s + 1) * LANES]
        ranks.append(_dot(sub.astype(BF16), tri_ref[...]) + carry)
        carry = carry + jnp.sum(sub, axis=1, keepdims=True)
    before = jnp.concatenate(ranks, axis=1) if len(ranks) > 1 else ranks[0]
    carry_ref[...] = jnp.broadcast_to(carry, carry_ref.shape)
    cnt_ref[...] = jnp.broadcast_to(carry, cnt_ref.shape)

    for kk in range(TOP_K):
        te_ref[kk:kk + 1, :] = jnp.sum(jnp.where(hots[kk], eidx, 0), axis=0, keepdims=True)
        tw_ref[kk:kk + 1, :] = ex[kk] * inv
        rk_ref[kk:kk + 1, :] = jnp.sum(jnp.where(hots[kk], before, 0.0), axis=0,
                                       keepdims=True).astype(jnp.int32)


def _merge(o_a, h_m, proj, h0, w_a, w_m, w_o, norm2_w, rw_t, rb_col, tri, row0, n):
    tm = _pick(math.gcd(n, row0) if row0 else n, (384, 256, 128))
    t0 = row0 // tm
    row = lambda i: (i, 0)
    src = lambda i: (t0 + i, 0)
    const = lambda i: (0, 0)
    tok = lambda i: (0, i)
    wspec = pl.BlockSpec((D_MODEL, D_MODEL), const)
    return pl.pallas_call(
        functools.partial(_merge_kernel, tm=tm),
        grid=(n // tm,),
        in_specs=[
            pl.BlockSpec((tm, D_MODEL), src),
            pl.BlockSpec((tm, D_MODEL), src),
            pl.BlockSpec((tm, D_MODEL), lambda i: (t0 + i, COL_GA)),
            pl.BlockSpec((tm, D_MODEL), lambda i: (t0 + i, COL_GM)),
            pl.BlockSpec((tm, D_MODEL), src),
            wspec, wspec, wspec,
            pl.BlockSpec((1, D_MODEL), const),
            pl.BlockSpec((N_EXPERTS, D_MODEL), const),
            pl.BlockSpec((N_EXPERTS, LANES), const),
            pl.BlockSpec((LANES, LANES), const),
        ],
        out_specs=[
            pl.BlockSpec((tm, D_MODEL), row),
            pl.BlockSpec((tm, ROW_WORDS), row),
            pl.BlockSpec((TOP_K, tm), tok),
            pl.BlockSpec((TOP_K, tm), tok),
            pl.BlockSpec((TOP_K, tm), tok),
            pl.BlockSpec((N_EXPERTS, LANES), const),
        ],
        out_shape=[
            jax.ShapeDtypeStruct((n, D_MODEL), F32),
            jax.ShapeDtypeStruct((n, ROW_WORDS), jnp.int32),
            jax.ShapeDtypeStruct((TOP_K, n), jnp.int32),
            jax.ShapeDtypeStruct((TOP_K, n), F32),
            jax.ShapeDtypeStruct((TOP_K, n), jnp.int32),
            jax.ShapeDtypeStruct((N_EXPERTS, LANES), F32),
        ],
        scratch_shapes=[pltpu.VMEM((N_EXPERTS, LANES), F32)],
        compiler_params=pltpu.CompilerParams(
            dimension_semantics=("arbitrary",), vmem_limit_bytes=VMEM_LIMIT),
        name="merge_router",
    )(o_a, h_m, proj, proj, h0, w_a, w_m, w_o, norm2_w, rw_t, rb_col, tri)


def _expert_kernel(be_ref, nu_ref, x_ref, wg_ref, bg_ref, wu_ref, bu_ref, wd_ref, bd_ref, y_ref,
                   wg_s, wu_s, wd_s):
    i = pl.program_id(0)
    used = i < nu_ref[0]

    @pl.when(jnp.logical_or(i == 0, be_ref[i] != be_ref[jnp.maximum(i - 1, 0)]))
    def _():
        wg_s[...] = wg_ref[0].astype(BF16)
        wu_s[...] = wu_ref[0].astype(BF16)
        wd_s[...] = wd_ref[0].astype(BF16)

    @pl.when(used)
    def _():
        x = _unpack_rows(x_ref[...]).astype(BF16)
        gt = jnp.minimum(_dot(x, wg_s[...]) + bg_ref[0], SWIGLU_LIMIT)
        up = jnp.clip(_dot(x, wu_s[...]) + bu_ref[0], -SWIGLU_LIMIT, SWIGLU_LIMIT)
        glu = gt * jax.nn.sigmoid(SWIGLU_ALPHA * gt)
        act = ((up + 1.0) * glu).astype(BF16)
        y_ref[...] = _pack_rows(_dot(act, wd_s[...]) + bd_ref[0])

    @pl.when(jnp.logical_not(used))
    def _():
        y_ref[...] = jnp.zeros(y_ref.shape, jnp.int32)


def _experts(blk_e, n_used, xs, w_g, b_g, w_u, b_u, w_d, b_d):
    cap = xs.shape[0]
    n_blocks = cap // MOE_ROWS
    wspec = lambda d0, d1: pl.BlockSpec((1, d0, d1), lambda i, be, nu: (be[i], 0, 0))
    return pl.pallas_call(
        _expert_kernel,
        grid_spec=pltpu.PrefetchScalarGridSpec(
            num_scalar_prefetch=2,
            grid=(n_blocks,),
            in_specs=[
                pl.BlockSpec((MOE_ROWS, ROW_WORDS), lambda i, be, nu: (i, 0)),
                wspec(D_MODEL, D_FF), wspec(1, D_FF),
                wspec(D_MODEL, D_FF), wspec(1, D_FF),
                wspec(D_FF, D_MODEL), wspec(1, D_MODEL),
            ],
            out_specs=pl.BlockSpec((MOE_ROWS, ROW_WORDS), lambda i, be, nu: (i, 0)),
            scratch_shapes=[pltpu.VMEM((D_MODEL, D_FF), BF16), pltpu.VMEM((D_MODEL, D_FF), BF16),
                            pltpu.VMEM((D_FF, D_MODEL), BF16)],
        ),
        out_shape=jax.ShapeDtypeStruct((cap, ROW_WORDS), jnp.int32),
        compiler_params=pltpu.CompilerParams(
            dimension_semantics=("arbitrary",), vmem_limit_bytes=VMEM_LIMIT),
        name="experts",
    )(blk_e, n_used, xs, w_g, b_g, w_u, b_u, w_d, b_d)


def _sc_worker_windows(n_win):
    return -(-n_win // (SC_CORES * SC_SUBCORES))


def _sc_dispatch(x_words, dest3, cap):
    n = x_words.shape[0]
    n_win = n // SC_WINDOW
    per = _sc_worker_windows(n_win)

    def body(x_hbm, d_hbm, o_hbm, idx_v, rows_v):
        wid = lax.axis_index("s") * SC_CORES + lax.axis_index("c")

        @pl.loop(0, per)
        def _(i):
            win = jnp.minimum(wid * per + i, n_win - 1)
            pltpu.sync_copy(d_hbm.at[win], idx_v)
            pltpu.sync_copy(x_hbm.at[pl.ds(win * SC_WINDOW, SC_WINDOW)], rows_v)
            for kk in range(TOP_K):
                pltpu.sync_copy(rows_v, o_hbm.at[idx_v.at[kk]])

    return pl.kernel(
        body,
        out_type=jax.ShapeDtypeStruct((cap, ROW_WORDS), jnp.int32),
        mesh=plsc.VectorSubcoreMesh(core_axis_name="c", subcore_axis_name="s"),
        scratch_types=[pltpu.VMEM((TOP_K, SC_WINDOW), jnp.int32),
                       pltpu.VMEM((SC_WINDOW, ROW_WORDS), jnp.int32)],
        name="moe_dispatch",
    )(x_words, dest3)


def _sc_gather(y_words, idx2):
    n_win = idx2.shape[0]
    per = _sc_worker_windows(n_win)

    def body(y_hbm, i_hbm, o_hbm, idx_v, rows_v, sem):
        wid = lax.axis_index("s") * SC_CORES + lax.axis_index("c")

        @pl.loop(0, per)
        def _(i):
            win = jnp.minimum(wid * per + i, n_win - 1)
            pltpu.sync_copy(i_hbm.at[win], idx_v)
            pltpu.async_copy(y_hbm.at[idx_v], rows_v, sem).wait()
            pltpu.sync_copy(rows_v, o_hbm.at[pl.ds(win * SC_WINDOW, SC_WINDOW)])

    return pl.kernel(
        body,
        out_type=jax.ShapeDtypeStruct((n_win * SC_WINDOW, ROW_WORDS), jnp.int32),
        mesh=plsc.VectorSubcoreMesh(core_axis_name="c", subcore_axis_name="s"),
        scratch_types=[pltpu.VMEM((SC_WINDOW,), jnp.int32),
                       pltpu.VMEM((SC_WINDOW, ROW_WORDS), jnp.int32),
                       pltpu.SemaphoreType.DMA],
        name="moe_gather",
    )(y_words, idx2)


def _combine_kernel(h_ref, y_ref, w_ref, fw_ref, o_ref):
    acc = h_ref[0]
    for kk in range(TOP_K):
        acc = acc + w_ref[0, :, kk:kk + 1] * _unpack_rows(y_ref[kk, 0])
    o_ref[0] = acc * lax.rsqrt(jnp.mean(acc * acc, axis=-1, keepdims=True) + EPS) * fw_ref[...]


def _combine(h1, yg, wts, final_w, b0, nb, lr):
    tr = _pick(lr, (1024, 512, 256, 128))
    return pl.pallas_call(
        _combine_kernel,
        grid=(nb, lr // tr),
        in_specs=[
            pl.BlockSpec((1, tr, D_MODEL), lambda i, j: (b0 + i, j, 0)),
            pl.BlockSpec((TOP_K, 1, tr, ROW_WORDS), lambda i, j: (0, b0 + i, j, 0)),
            pl.BlockSpec((1, tr, TOP_K), lambda i, j: (b0 + i, j, 0)),
            pl.BlockSpec((1, D_MODEL), lambda i, j: (0, 0)),
        ],
        out_specs=pl.BlockSpec((1, tr, D_MODEL), lambda i, j: (i, j, 0)),
        out_shape=jax.ShapeDtypeStruct((nb, lr, D_MODEL), F32),
        compiler_params=pltpu.CompilerParams(
            dimension_semantics=("arbitrary", "arbitrary"), vmem_limit_bytes=VMEM_LIMIT),
        name="combine",
    )(h1, yg, wts, final_w)


def _rope_tables(lr):
    l = lr + N_META
    half = ROT_DIM // 2
    pos = jnp.concatenate([jnp.arange(N_META, l, dtype=F32), jnp.arange(N_META, dtype=F32)])
    inv_freq = ROPE_THETA ** (-jnp.arange(0, ROT_DIM, 2, dtype=F32) / ROT_DIM)
    ang = pos[:, None] * inv_freq[None, :]
    cos, sin = jnp.cos(ang), jnp.sin(ang)
    z = jnp.zeros((l, A_QK_DIM - ROT_DIM), F32)
    zh = jnp.zeros((l, half), F32)
    cos64 = jnp.concatenate([cos, cos, z + 1.0], axis=1)
    sa64 = jnp.concatenate([-sin, zh, z], axis=1)
    sb64 = jnp.concatenate([zh, sin, z], axis=1)
    dup = lambda a: jnp.concatenate([a, a], axis=1)
    return dup(cos64), dup(sa64), dup(sb64)


def kernel(x_prompt, x_sample, meta_tokens, norm1_w, w_in, lambda_q1, lambda_k1, lambda_q2, lambda_k2, attn_subln_w, mlstm_gate_b, mlstm_norm_w, w_br_attn, w_br_mlstm, w_out, norm2_w, router_w, router_b, w_gate, b_gate, w_up, b_up, w_down, b_down, final_norm_w):
    lr = x_prompt.shape[1]
    l = lr + N_META
    assert x_sample.shape[1:] == x_prompt.shape[1:] == (lr, D_MODEL)
    assert lr % (2 * LANES) == 0, "attention pipelines query blocks in pairs; mLSTM chunks are 128 rows"

    w_in0 = w_in[0]
    g0, g1 = OFF_GATES, OFF_GATES + 4 * M_HEADS
    w_main = jnp.concatenate([w_in0[:, :g0], w_in0[:, g1:]], axis=1).astype(BF16)
    w_gates = jnp.pad(w_in0[:, g0:g1], ((0, 0), (0, LANES - 4 * M_HEADS))).astype(BF16)
    lam = (jnp.exp(jnp.sum(lambda_q1[0] * lambda_k1[0])) - jnp.exp(jnp.sum(lambda_q2[0] * lambda_k2[0]))
           + LAM_INIT)
    lam_row = jnp.full((1, LANES), lam, F32)
    sub_w = (attn_subln_w[0] * (1.0 - LAM_INIT)).reshape(1, A_V_DIM)
    neg_row = jnp.where(jnp.arange(LANES) < N_META, 0.0, NEG_BIG).astype(F32).reshape(1, LANES)
    cos_t, sa_t, sb_t = _rope_tables(lr)
    gate_bias = jnp.broadcast_to(
        jnp.pad(mlstm_gate_b[0].T, ((0, 0), (0, 4)))[:, :, None], (M_HEADS, 8, LANES)).astype(F32)
    norm_w_m = mlstm_norm_w[0].reshape(M_HEADS, 1, M_V_DIM)
    rw_t = router_w[0].T.astype(BF16)
    rb_col = jnp.broadcast_to(router_b[0][:, None], (N_EXPERTS, LANES)).astype(F32)
    tri = (jnp.arange(LANES)[:, None] < jnp.arange(LANES)[None, :]).astype(BF16)

    w_a, w_m, w_o = w_br_attn[0].astype(BF16), w_br_mlstm[0].astype(BF16), w_out[0].astype(BF16)
    n1, n2, fw = norm1_w[0].reshape(1, D_MODEL), norm2_w[0].reshape(1, D_MODEL), final_norm_w.reshape(1, D_MODEL)
    bg3, bu3, bd3 = (b_gate[0].reshape(N_EXPERTS, 1, D_FF), b_up[0].reshape(N_EXPERTS, 1, D_FF),
                     b_down[0].reshape(N_EXPERTS, 1, D_MODEL))

    b_all = x_prompt.shape[0] + x_sample.shape[0]
    n_all = b_all * l
    x_all = jnp.concatenate([x_prompt, x_sample], axis=0)
    meta_b = jnp.broadcast_to(meta_tokens[None].astype(x_all.dtype), (b_all, N_META, D_MODEL))
    h0 = jnp.concatenate([x_all, meta_b], axis=1).reshape(n_all, D_MODEL)

    proj, gates = _inproj(h0, n1, w_main, w_gates)
    proj3 = proj.reshape(b_all, l, N_MAIN)
    o_a = _attention(proj3, cos_t, sa_t, sb_t, neg_row, lam_row, sub_w, lr).reshape(n_all, D_MODEL)

    g4 = gates.reshape(b_all, l, 4, M_HEADS).transpose(0, 3, 2, 1)
    off = jnp.array([NEG_BIG, 1e4, NEG_BIG, 1e4], F32)[None, None, :, None]
    fill = jnp.broadcast_to(off, (b_all, M_HEADS, 4, CHUNK - N_META))
    g_row = jnp.concatenate([g4[..., :lr], fill, g4[..., lr:]], axis=-1)
    g_row = jnp.pad(g_row, ((0, 0), (0, 0), (0, 4), (0, 0)))
    h_m = _mlstm(proj3, g_row, gate_bias, norm_w_m, lr).reshape(n_all, D_MODEL)

    def channel_mix(b0, b):
        n = b * l
        assert n % SC_WINDOW == 0, "SparseCore dispatch works on whole 128-token windows"
        h1, xn2, top_e, top_w, rank, cnt = _merge(
            o_a, h_m, proj, h0, w_a, w_m, w_o, n2, rw_t, rb_col, tri, b0 * l, n)

        counts = cnt[:, 0].astype(jnp.int32)
        padded = (counts + MOE_ROWS - 1) // MOE_ROWS * MOE_ROWS
        pends = jnp.cumsum(padded)
        pstarts = pends - padded
        e_ids = jnp.arange(N_EXPERTS, dtype=jnp.int32)
        pstart_of = jnp.sum(jnp.where(top_e[..., None] == e_ids, pstarts, 0), axis=-1)
        dest = pstart_of + rank
        n_blocks = (n * TOP_K + N_EXPERTS * (MOE_ROWS - 1) + MOE_ROWS - 1) // MOE_ROWS
        cap = n_blocks * MOE_ROWS
        blk_start = jnp.arange(n_blocks, dtype=jnp.int32) * MOE_ROWS
        blk_e = jnp.minimum(jnp.sum((pends[None, :] <= blk_start[:, None]).astype(jnp.int32), axis=1),
                            N_EXPERTS - 1)
        n_used = (pends[-1:] // MOE_ROWS).astype(jnp.int32)

        dest_win = dest.reshape(TOP_K, n // SC_WINDOW, SC_WINDOW).transpose(1, 0, 2)
        xs = _sc_dispatch(xn2, dest_win, cap)
        y = _experts(blk_e, n_used, xs, w_gate[0], bg3, w_up[0], bu3, w_down[0], bd3)
        yg = _sc_gather(y, dest.reshape(TOP_K * n // SC_WINDOW, SC_WINDOW)).reshape(TOP_K, b, l, ROW_WORDS)
        return _combine(h1.reshape(b, l, D_MODEL), yg, top_w.T.reshape(b, l, TOP_K), fw, 0, b, lr)

    return (channel_mix(0, x_prompt.shape[0]), channel_mix(x_prompt.shape[0], x_sample.shape[0]))
```

```python
import functools
import math

import jax
import jax.numpy as jnp
from jax import lax
from jax.experimental import pallas as pl
from jax.experimental.pallas import tpu as pltpu
from jax.experimental.pallas import tpu_sc as plsc

F32 = jnp.float32
BF16 = jnp.bfloat16

D_MODEL = 1024
N_META = 16
EPS = 1e-6
A_HEADS = 8
A_QK_DIM = 64
A_V_DIM = 128
ROT_DIM = 16
ROPE_THETA = 500000.0
M_HEADS = 4
M_QK_DIM = 128
M_V_DIM = 256
CHUNK = 128
N_EXPERTS = 32
TOP_K = 4
D_FF = 1024
SWIGLU_LIMIT = 7.0
SWIGLU_ALPHA = 1.702
LAM_INIT = 0.8 - 0.6 * math.exp(-0.3 * 0)

LANES = 128
NEG_BIG = -1e30
MOE_ROWS = 512
SC_CORES = 2
SC_SUBCORES = 16
SC_WINDOW = 128
ROW_WORDS = D_MODEL // 2
W_AQK = A_HEADS * 2 * A_QK_DIM
W_AV = A_HEADS * A_V_DIM
W_MQK = M_HEADS * M_QK_DIM
W_MV = M_HEADS * M_V_DIM
OFF_GATES = 2 * W_AQK + W_AV + 2 * W_MQK + 2 * W_MV
N_MAIN = OFF_GATES + 2 * D_MODEL
COL_AQ, COL_AK, COL_AV = 0, W_AQK // LANES, 2 * W_AQK // LANES
COL_MQ = (2 * W_AQK + W_AV) // M_QK_DIM
COL_MK = COL_MQ + M_HEADS
COL_MV = (2 * W_AQK + W_AV + 2 * W_MQK) // M_V_DIM
COL_MO = COL_MV + M_HEADS
COL_GA = OFF_GATES // D_MODEL
COL_GM = COL_GA + 1

VMEM_LIMIT = 56 * 1024 * 1024


def _pick(n, prefs):
    for p in prefs:
        if n % p == 0:
            return p
    raise ValueError(f"no tile for {n}")


def _nt(a, b):
    return lax.dot_general(a, b, (((1,), (1,)), ((), ())), preferred_element_type=F32)


def _dot(a, b):
    return jnp.dot(a, b, preferred_element_type=F32)


def _pack_rows(x):
    w = x.shape[1] // 2
    bits = lambda v: lax.bitcast_convert_type(v.astype(BF16).astype(F32), jnp.uint32)
    lo = lax.shift_right_logical(bits(x[:, :w]), jnp.uint32(16))
    hi = bits(x[:, w:]) & jnp.uint32(0xFFFF0000)
    return lax.bitcast_convert_type(lo | hi, jnp.int32)


def _unpack_rows(wds):
    u = lax.bitcast_convert_type(wds, jnp.uint32)
    lo = lax.bitcast_convert_type(lax.shift_left(u, jnp.uint32(16)), F32)
    hi = lax.bitcast_convert_type(u & jnp.uint32(0xFFFF0000), F32)
    return jnp.concatenate([lo, hi], axis=1)


def _inproj_kernel(x_ref, n1_ref, w_ref, wg_ref, o_ref, g_ref, xn_ref):
    @pl.when(pl.program_id(1) == 0)
    def _():
        x = x_ref[...]
        ms = jnp.mean(x * x, axis=-1, keepdims=True)
        xn = (x * lax.rsqrt(ms + EPS) * n1_ref[...]).astype(BF16)
        xn_ref[...] = xn
        g_ref[...] = _dot(xn, wg_ref[...])[:, :4 * M_HEADS]

    o_ref[...] = _dot(xn_ref[...], w_ref[...]).astype(BF16)


def _inproj(h0, norm1_w, w_main, w_gates):
    n = h0.shape[0]
    tm = _pick(n, (1152, 768, 512, 384, 256, 128))
    tn = 2048
    return pl.pallas_call(
        _inproj_kernel,
        grid=(n // tm, N_MAIN // tn),
        in_specs=[
            pl.BlockSpec((tm, D_MODEL), lambda i, j: (i, 0)),
            pl.BlockSpec((1, D_MODEL), lambda i, j: (0, 0)),
            pl.BlockSpec((D_MODEL, tn), lambda i, j: (0, j)),
            pl.BlockSpec((D_MODEL, LANES), lambda i, j: (0, 0)),
        ],
        out_specs=[
            pl.BlockSpec((tm, tn), lambda i, j: (i, j)),
            pl.BlockSpec((tm, 4 * M_HEADS), lambda i, j: (i, 0)),
        ],
        out_shape=[
            jax.ShapeDtypeStruct((n, N_MAIN), BF16),
            jax.ShapeDtypeStruct((n, 4 * M_HEADS), F32),
        ],
        scratch_shapes=[pltpu.VMEM((tm, D_MODEL), BF16)],
        compiler_params=pltpu.CompilerParams(
            dimension_semantics=("arbitrary", "arbitrary"), vmem_limit_bytes=VMEM_LIMIT),
        name="inproj",
    )(h0, norm1_w, w_main, w_gates)


def _attn_kernel(q_ref, k_ref, v_ref, cos_ref, sa_ref, sb_ref, neg_ref, lam_ref, sw_ref,
                 o_ref, q1_ref, q2_ref, vs_ref, kt_ref, sa1_ref, sa2_ref, sb1_ref, sb2_ref, *, lr):
    l = lr + N_META
    lp = lr + LANES
    nblk = lr // LANES
    scale = A_QK_DIM ** -0.5 * math.log2(math.e)

    def rope(t):
        return (t * cos_ref[...] + pltpu.roll(t, LANES - ROT_DIM // 2, 1) * sa_ref[...]
                + pltpu.roll(t, ROT_DIM // 2, 1) * sb_ref[...])

    lane = lax.broadcasted_iota(jnp.int32, (l, LANES), 1)
    qr = rope(q_ref[0].astype(F32)) * scale
    q1_ref[0:l, :] = jnp.where(lane < A_QK_DIM, qr, 0.0).astype(BF16)
    q2_ref[0:l, :] = jnp.where(lane >= A_QK_DIM, qr, 0.0).astype(BF16)
    vs_ref[0:l, :] = v_ref[0]
    pad = jnp.zeros((lp - l, LANES), BF16)
    q1_ref[l:lp, :] = pad
    q2_ref[l:lp, :] = pad
    vs_ref[l:lp, :] = pad
    kr = rope(k_ref[0].astype(F32))
    for c in range(nblk):
        kt_ref[:, c * LANES:(c + 1) * LANES] = kr[c * LANES:(c + 1) * LANES, :].T.astype(BF16)
    k_tail = jnp.concatenate([kr[lr:l, :], jnp.zeros((lp - l, LANES), F32)], axis=0)
    kt_ref[:, lr:lp] = k_tail.T.astype(BF16)

    lam = lam_ref[:, 0:1]

    def scores(r0, s1_ref, s2_ref):
        rows = pl.ds(r0, LANES)
        k_all = kt_ref[...]
        s1_ref[...] = _dot(q1_ref[rows, :], k_all)
        s2_ref[...] = _dot(q2_ref[rows, :], k_all)

    def softmax_parts(s_ref, nrows):
        sm = s_ref[0:nrows, :lr]
        sl = s_ref[0:nrows, lr:] + neg_ref[...]
        m = jnp.maximum(jnp.max(sm, axis=1, keepdims=True), jnp.max(sl, axis=1, keepdims=True))
        pm = jnp.exp2(sm - m)
        pl_ = jnp.exp2(sl - m)
        tot = jnp.sum(pm, axis=1, keepdims=True) + jnp.sum(pl_, axis=1, keepdims=True)
        return pm, pl_, tot

    def finish(r0, nrows, s1_ref, s2_ref):
        p1m, p1l, t1 = softmax_parts(s1_ref, nrows)
        p2m, p2l, t2 = softmax_parts(s2_ref, nrows)
        c = lam * t1 / t2
        am = (p1m - p2m * c).astype(BF16)
        al = (p1l - p2l * c).astype(BF16)
        o = (_dot(am, vs_ref[0:lr, :]) + _dot(al, vs_ref[lr:lp, :])) * (1.0 / t1)
        o = o * lax.rsqrt(jnp.mean(o * o, axis=-1, keepdims=True) + EPS) * sw_ref[...]
        o_ref[0, pl.ds(r0, nrows), :] = o.astype(BF16)

    scores(0, sa1_ref, sa2_ref)

    pairs = _pick(nblk // 2, (8, 4, 2, 1))

    def body(j, c):
        for u in range(pairs):
            ra = pl.multiple_of((j * pairs + u) * (2 * LANES), LANES)
            rb = ra + LANES
            scores(rb, sb1_ref, sb2_ref)
            finish(ra, LANES, sa1_ref, sa2_ref)
            scores(rb + LANES, sa1_ref, sa2_ref)
            finish(rb, LANES, sb1_ref, sb2_ref)
        return c

    lax.fori_loop(0, nblk // (2 * pairs), body, 0)
    finish(lr, N_META, sa1_ref, sa2_ref)


def _attention(proj3, cos_t, sa_t, sb_t, neg_row, lam_row, sub_w, lr):
    b, l, _ = proj3.shape
    lp = lr + LANES
    const2 = lambda i, h: (0, 0)
    return pl.pallas_call(
        functools.partial(_attn_kernel, lr=lr),
        grid=(b, A_HEADS),
        in_specs=[
            pl.BlockSpec((1, l, LANES), lambda i, h: (i, 0, COL_AQ + h)),
            pl.BlockSpec((1, l, LANES), lambda i, h: (i, 0, COL_AK + h)),
            pl.BlockSpec((1, l, LANES), lambda i, h: (i, 0, COL_AV + h)),
            pl.BlockSpec((l, LANES), const2),
            pl.BlockSpec((l, LANES), const2),
            pl.BlockSpec((l, LANES), const2),
            pl.BlockSpec((1, LANES), const2),
            pl.BlockSpec((1, LANES), const2),
            pl.BlockSpec((1, LANES), const2),
        ],
        out_specs=pl.BlockSpec((1, l, LANES), lambda i, h: (i, 0, h)),
        out_shape=jax.ShapeDtypeStruct((b, l, A_HEADS * A_V_DIM), BF16),
        scratch_shapes=([pltpu.VMEM((lp, LANES), BF16)] * 3 + [pltpu.VMEM((LANES, lp), BF16)]
                        + [pltpu.VMEM((LANES, lp), F32)] * 4),
        compiler_params=pltpu.CompilerParams(
            dimension_semantics=("arbitrary", "arbitrary"), vmem_limit_bytes=VMEM_LIMIT),
        name="diff_attn",
    )(proj3, proj3, proj3, cos_t, sa_t, sb_t, neg_row, lam_row, sub_w)


def _log_sigmoid(x):
    return jnp.minimum(x, 0.0) - jnp.log(1.0 + jnp.exp(-jnp.abs(x)))


def _mlstm_kernel(q_ref, k_ref, v_ref, mo_ref, g_ref, gb_ref, nw_ref, o_ref,
                  c_ref, m_ref, kt_ref, h0_ref, u_ref, bcol_ref, amax_ref, aend_ref, bend_ref,
                  hf_ref, hb_ref, *, lr):
    l = lr + N_META
    t = CHUNK
    nc = lr // t
    dv = M_V_DIM
    scale = M_QK_DIM ** -0.5
    row = lax.broadcasted_iota(jnp.int32, (t, t), 0)
    col = lax.broadcasted_iota(jnp.int32, (t, t), 1)
    eye = row == col
    ones_tile = jnp.where(col == 0, 1.0, 0.0).astype(BF16)
    meta_r0 = l - t

    c_ref[...] = jnp.zeros(c_ref.shape, F32)
    m_ref[...] = jnp.zeros(m_ref.shape, F32)

    for c in range(nc + 1):
        r0 = c * t if c < nc else meta_r0
        kt_ref[:, c * t:(c + 1) * t] = k_ref[0, r0:r0 + t, :].astype(F32).T.astype(BF16)

    def scaled_q(r0):
        return (q_ref[0, pl.ds(r0, t), :].astype(F32) * scale).astype(BF16)

    def intra(ci, r0, g0):
        v_aug = jnp.concatenate([v_ref[0, pl.ds(r0, t), :], ones_tile], axis=1)
        kt = kt_ref[:, pl.ds(g0, t)]
        s_raw = _dot(scaled_q(r0), kt)
        for d, mask in ((0, col <= row), (1, col >= row)):
            gi = 2 * d
            ig = g_ref[0, 0, gi:gi + 1, pl.ds(g0, t)] + gb_ref[0, gi:gi + 1, :]
            lf = _log_sigmoid(g_ref[0, 0, gi + 1:gi + 2, pl.ds(g0, t)] + gb_ref[0, gi + 1:gi + 2, :])
            b_col = jnp.sum(jnp.where(mask, jnp.broadcast_to(lf, (t, t)), 0.0), axis=1, keepdims=True)
            b_row = jnp.sum(jnp.where(eye, jnp.broadcast_to(b_col, (t, t)), 0.0), axis=0, keepdims=True)
            a_row = ig - b_row
            dm = jnp.where(mask, jnp.broadcast_to(a_row, (t, t)), NEG_BIG)
            amax = jnp.max(dm, axis=1, keepdims=True)
            s0 = s_raw * jnp.exp(dm - amax)
            h0_ref[d, pl.ds(g0, t), :] = _dot(s0.astype(BF16), v_aug)
            a_end = jnp.max(a_row, axis=1, keepdims=True)
            kts = (kt.astype(F32) * jnp.exp(a_row - a_end)).astype(BF16)
            u_ref[d, ci] = _dot(kts, v_aug)
            bcol_ref[d, pl.ds(g0, t), :] = b_col
            amax_ref[d, pl.ds(g0, t), :] = amax
            aend_ref[d, ci] = jnp.broadcast_to(a_end, (8, LANES))
            bend_ref[d, ci] = jnp.broadcast_to(jnp.sum(lf, axis=1, keepdims=True), (8, LANES))

    ua = _pick(nc, (8, 4, 2, 1))

    def abody(i, c):
        for u in range(ua):
            ci = ua * i + u
            r0 = pl.multiple_of(ci * t, t)
            intra(ci, r0, r0)
        return c

    lax.fori_loop(0, nc // ua, abody, 0)
    intra(nc, meta_r0, lr)

    def step(d, ci, r0, g0):
        m_prev = m_ref[d, 0:1, 0:1]
        amax = amax_ref[d, pl.ds(g0, t), :]
        m_col = jnp.maximum(m_prev, amax)
        num = (jnp.exp(m_prev - m_col) * _dot(scaled_q(r0), c_ref[d].astype(BF16))
               + jnp.exp(amax - m_col) * h0_ref[d, pl.ds(g0, t), :])
        den = jnp.maximum(jnp.abs(num[:, dv:dv + 1]), jnp.exp(-(bcol_ref[d, pl.ds(g0, t), :] + m_col)))
        a_end = aend_ref[d, ci, 0:1, 0:1]
        m_end = jnp.maximum(m_prev, a_end)
        c_ref[d] = jnp.exp(m_prev - m_end) * c_ref[d] + jnp.exp(a_end - m_end) * u_ref[d, ci]
        m_ref[d] = jnp.broadcast_to(bend_ref[d, ci, 0:1, 0:1] + m_end, (8, LANES))
        return num[:, :dv] * (1.0 / den)

    hf_ref[lr:l, :] = step(0, nc, meta_r0, lr)[t - N_META:, :]

    ub = _pick(nc, (2, 1))

    def bbody(i, c):
        for u in range(ub):
            jf = ub * i + u
            rf = pl.multiple_of(jf * t, t)
            hf_ref[pl.ds(rf, t), :] = step(0, jf, rf, rf)
            jb = nc - 1 - jf
            rb = pl.multiple_of(jb * t, t)
            hb_ref[pl.ds(rb, t), :] = step(1, jb, rb, rb)
        return c

    lax.fori_loop(0, nc // ub, bbody, 0)
    hb_ref[lr:l, :] = step(1, nc, meta_r0, lr)[t - N_META:, :]

    def finish(r0, rows):
        hs = hf_ref[pl.ds(r0, rows), :] + hb_ref[pl.ds(r0, rows), :]
        y = hs * lax.rsqrt(jnp.mean(hs * hs, axis=-1, keepdims=True) + EPS) * nw_ref[0]
        gate = jax.nn.sigmoid(mo_ref[0, pl.ds(r0, rows), :].astype(F32))
        o_ref[0, pl.ds(r0, rows), :] = (y * gate).astype(BF16)

    def fbody(i, c):
        for u in range(ua):
            finish(pl.multiple_of((ua * i + u) * t, t), t)
        return c

    lax.fori_loop(0, nc // ua, fbody, 0)
    finish(lr, N_META)


def _mlstm(proj3, gates_row, gate_bias, norm_w, lr):
    b, l, _ = proj3.shape
    gl = gates_row.shape[-1]
    nw = lr // CHUNK + 1
    aug = M_V_DIM + LANES
    return pl.pallas_call(
        functools.partial(_mlstm_kernel, lr=lr),
        grid=(b, M_HEADS),
        in_specs=[
            pl.BlockSpec((1, l, M_QK_DIM), lambda i, h: (i, 0, COL_MQ + h)),
            pl.BlockSpec((1, l, M_QK_DIM), lambda i, h: (i, 0, COL_MK + h)),
            pl.BlockSpec((1, l, M_V_DIM), lambda i, h: (i, 0, COL_MV + h)),
            pl.BlockSpec((1, l, M_V_DIM), lambda i, h: (i, 0, COL_MO + h)),
            pl.BlockSpec((1, 1, 8, gl), lambda i, h: (i, h, 0, 0)),
            pl.BlockSpec((1, 8, LANES), lambda i, h: (h, 0, 0)),
            pl.BlockSpec((1, 1, M_V_DIM), lambda i, h: (h, 0, 0)),
        ],
        out_specs=pl.BlockSpec((1, l, M_V_DIM), lambda i, h: (i, 0, h)),
        out_shape=jax.ShapeDtypeStruct((b, l, M_HEADS * M_V_DIM), BF16),
        scratch_shapes=[
            pltpu.VMEM((2, M_QK_DIM, aug), F32),
            pltpu.VMEM((2, 8, LANES), F32),
            pltpu.VMEM((M_QK_DIM, gl), BF16),
            pltpu.VMEM((2, gl, aug), F32),
            pltpu.VMEM((2, nw, M_QK_DIM, aug), F32),
            pltpu.VMEM((2, gl, 1), F32),
            pltpu.VMEM((2, gl, 1), F32),
            pltpu.VMEM((2, nw, 8, LANES), F32),
            pltpu.VMEM((2, nw, 8, LANES), F32),
            pltpu.VMEM((l, M_V_DIM), F32), pltpu.VMEM((l, M_V_DIM), F32),
        ],
        compiler_params=pltpu.CompilerParams(
            dimension_semantics=("arbitrary", "arbitrary"), vmem_limit_bytes=VMEM_LIMIT),
        name="mlstm",
    )(proj3, proj3, proj3, proj3, gates_row, gate_bias, norm_w)


def _merge_kernel(oa_ref, hm_ref, ga_ref, gm_ref, h_ref, wa_ref, wm_ref, wo_ref, n2_ref,
                  rw_ref, rb_ref, tri_ref,
                  h1_ref, xn_ref, te_ref, tw_ref, rk_ref, cnt_ref, carry_ref, *, tm):
    @pl.when(pl.program_id(0) == 0)
    def _():
        carry_ref[...] = jnp.zeros(carry_ref.shape, F32)

    pa = _dot(oa_ref[...], wa_ref[...])
    pm = _dot(hm_ref[...], wm_ref[...])
    merged = (jax.nn.sigmoid(ga_ref[...].astype(F32)) * pa
              + jax.nn.sigmoid(gm_ref[...].astype(F32)) * pm)
    h1 = h_ref[...] + _dot(merged.astype(BF16), wo_ref[...])
    h1_ref[...] = h1
    xn32 = h1 * lax.rsqrt(jnp.mean(h1 * h1, axis=-1, keepdims=True) + EPS) * n2_ref[...]
    xn = xn32.astype(BF16)
    xn_ref[...] = _pack_rows(xn32)

    logits = _nt(rw_ref[...], xn) + rb_ref[:, 0:1]
    eidx = lax.broadcasted_iota(jnp.int32, (N_EXPERTS, tm), 0)
    work = logits
    vals, hots = [], []
    for _ in range(TOP_K):
        mx = jnp.max(work, axis=0, keepdims=True)
        sel = jnp.min(jnp.where(work == mx, eidx, N_EXPERTS), axis=0, keepdims=True)
        hot = eidx == sel
        vals.append(mx)
        hots.append(hot)
        work = jnp.where(hot, -jnp.inf, work)
    ex = [jnp.exp(v - vals[0]) for v in vals]
    inv = 1.0 / (ex[0] + ex[1] + ex[2] + ex[3])
    chosen = jnp.where(hots[0] | hots[1] | hots[2] | hots[3], 1.0, 0.0)

    carry = carry_ref[:, 0:1]
    ranks = []
    for s in range(tm // LANES):
        sub = chosen[:, s * LANES:(s + 1) * LANES]
        ranks.append(_dot(sub.astype(BF16), tri_ref[...]) + carry)
        carry = carry + jnp.sum(sub, axis=1, keepdims=True)
    before = jnp.concatenate(ranks, axis=1) if len(ranks) > 1 else ranks[0]
    carry_ref[...] = jnp.broadcast_to(carry, carry_ref.shape)
    cnt_ref[...] = jnp.broadcast_to(carry, cnt_ref.shape)

    for kk in range(TOP_K):
        te_ref[kk:kk + 1, :] = jnp.sum(jnp.where(hots[kk], eidx, 0), axis=0, keepdims=True)
        tw_ref[kk:kk + 1, :] = ex[kk] * inv
        rk_ref[kk:kk + 1, :] = jnp.sum(jnp.where(hots[kk], before, 0.0), axis=0,
                                       keepdims=True).astype(jnp.int32)


def _merge(o_a, h_m, proj, h0, w_a, w_m, w_o, norm2_w, rw_t, rb_col, tri, row0, n):
    tm = _pick(math.gcd(n, row0) if row0 else n, (384, 256, 128))
    t0 = row0 // tm
    row = lambda i: (i, 0)
    src = lambda i: (t0 + i, 0)
    const = lambda i: (0, 0)
    tok = lambda i: (0, i)
    wspec = pl.BlockSpec((D_MODEL, D_MODEL), const)
    return pl.pallas_call(
        functools.partial(_merge_kernel, tm=tm),
        grid=(n // tm,),
        in_specs=[
            pl.BlockSpec((tm, D_MODEL), src),
            pl.BlockSpec((tm, D_MODEL), src),
            pl.BlockSpec((tm, D_MODEL), lambda i: (t0 + i, COL_GA)),
            pl.BlockSpec((tm, D_MODEL), lambda i: (t0 + i, COL_GM)),
            pl.BlockSpec((tm, D_MODEL), src),
            wspec, wspec, wspec,
            pl.BlockSpec((1, D_MODEL), const),
            pl.BlockSpec((N_EXPERTS, D_MODEL), const),
            pl.BlockSpec((N_EXPERTS, LANES), const),
            pl.BlockSpec((LANES, LANES), const),
        ],
        out_specs=[
            pl.BlockSpec((tm, D_MODEL), row),
            pl.BlockSpec((tm, ROW_WORDS), row),
            pl.BlockSpec((TOP_K, tm), tok),
            pl.BlockSpec((TOP_K, tm), tok),
            pl.BlockSpec((TOP_K, tm), tok),
            pl.BlockSpec((N_EXPERTS, LANES), const),
        ],
        out_shape=[
            jax.ShapeDtypeStruct((n, D_MODEL), F32),
            jax.ShapeDtypeStruct((n, ROW_WORDS), jnp.int32),
            jax.ShapeDtypeStruct((TOP_K, n), jnp.int32),
            jax.ShapeDtypeStruct((TOP_K, n), F32),
            jax.ShapeDtypeStruct((TOP_K, n), jnp.int32),
            jax.ShapeDtypeStruct((N_EXPERTS, LANES), F32),
        ],
        scratch_shapes=[pltpu.VMEM((N_EXPERTS, LANES), F32)],
        compiler_params=pltpu.CompilerParams(
            dimension_semantics=("arbitrary",), vmem_limit_bytes=VMEM_LIMIT),
        name="merge_router",
    )(o_a, h_m, proj, proj, h0, w_a, w_m, w_o, norm2_w, rw_t, rb_col, tri)


def _expert_kernel(be_ref, nu_ref, x_ref, wg_ref, bg_ref, wu_ref, bu_ref, wd_ref, bd_ref, y_ref,
                   wg_s, wu_s, wd_s):
    i = pl.program_id(0)
    used = i < nu_ref[0]

    @pl.when(jnp.logical_or(i == 0, be_ref[i] != be_ref[jnp.maximum(i - 1, 0)]))
    def _():
        wg_s[...] = wg_ref[0].astype(BF16)
        wu_s[...] = wu_ref[0].astype(BF16)
        wd_s[...] = wd_ref[0].astype(BF16)

    @pl.when(used)
    def _():
        x = _unpack_rows(x_ref[...]).astype(BF16)
        gt = jnp.minimum(_dot(x, wg_s[...]) + bg_ref[0], SWIGLU_LIMIT)
        up = jnp.clip(_dot(x, wu_s[...]) + bu_ref[0], -SWIGLU_LIMIT, SWIGLU_LIMIT)
        glu = gt * jax.nn.sigmoid(SWIGLU_ALPHA * gt)
        act = ((up + 1.0) * glu).astype(BF16)
        y_ref[...] = _pack_rows(_dot(act, wd_s[...]) + bd_ref[0])

    @pl.when(jnp.logical_not(used))
    def _():
        y_ref[...] = jnp.zeros(y_ref.shape, jnp.int32)


def _experts(blk_e, n_used, xs, w_g, b_g, w_u, b_u, w_d, b_d):
    cap = xs.shape[0]
    n_blocks = cap // MOE_ROWS
    wspec = lambda d0, d1: pl.BlockSpec((1, d0, d1), lambda i, be, nu: (be[i], 0, 0))
    return pl.pallas_call(
        _expert_kernel,
        grid_spec=pltpu.PrefetchScalarGridSpec(
            num_scalar_prefetch=2,
            grid=(n_blocks,),
            in_specs=[
                pl.BlockSpec((MOE_ROWS, ROW_WORDS), lambda i, be, nu: (i, 0)),
                wspec(D_MODEL, D_FF), wspec(1, D_FF),
                wspec(D_MODEL, D_FF), wspec(1, D_FF),
                wspec(D_FF, D_MODEL), wspec(1, D_MODEL),
            ],
            out_specs=pl.BlockSpec((MOE_ROWS, ROW_WORDS), lambda i, be, nu: (i, 0)),
            scratch_shapes=[pltpu.VMEM((D_MODEL, D_FF), BF16), pltpu.VMEM((D_MODEL, D_FF), BF16),
                            pltpu.VMEM((D_FF, D_MODEL), BF16)],
        ),
        out_shape=jax.ShapeDtypeStruct((cap, ROW_WORDS), jnp.int32),
        compiler_params=pltpu.CompilerParams(
            dimension_semantics=("arbitrary",), vmem_limit_bytes=VMEM_LIMIT),
        name="experts",
    )(blk_e, n_used, xs, w_g, b_g, w_u, b_u, w_d, b_d)


def _sc_worker_windows(n_win):
    return -(-n_win // (SC_CORES * SC_SUBCORES))


def _sc_dispatch(x_words, dest3, cap):
    n = x_words.shape[0]
    n_win = n // SC_WINDOW
    per = _sc_worker_windows(n_win)

    def body(x_hbm, d_hbm, o_hbm, idx_v, rows_v):
        wid = lax.axis_index("s") * SC_CORES + lax.axis_index("c")

        @pl.loop(0, per)
        def _(i):
            win = jnp.minimum(wid * per + i, n_win - 1)
            pltpu.sync_copy(d_hbm.at[win], idx_v)
            pltpu.sync_copy(x_hbm.at[pl.ds(win * SC_WINDOW, SC_WINDOW)], rows_v)
            for kk in range(TOP_K):
                pltpu.sync_copy(rows_v, o_hbm.at[idx_v.at[kk]])

    return pl.kernel(
        body,
        out_type=jax.ShapeDtypeStruct((cap, ROW_WORDS), jnp.int32),
        mesh=plsc.VectorSubcoreMesh(core_axis_name="c", subcore_axis_name="s"),
        scratch_types=[pltpu.VMEM((TOP_K, SC_WINDOW), jnp.int32),
                       pltpu.VMEM((SC_WINDOW, ROW_WORDS), jnp.int32)],
        name="moe_dispatch",
    )(x_words, dest3)


def _sc_gather(y_words, idx2):
    n_win = idx2.shape[0]
    per = _sc_worker_windows(n_win)

    def body(y_hbm, i_hbm, o_hbm, idx_v, rows_v, sem):
        wid = lax.axis_index("s") * SC_CORES + lax.axis_index("c")

        @pl.loop(0, per)
        def _(i):
            win = jnp.minimum(wid * per + i, n_win - 1)
            pltpu.sync_copy(i_hbm.at[win], idx_v)
            pltpu.async_copy(y_hbm.at[idx_v], rows_v, sem).wait()
            pltpu.sync_copy(rows_v, o_hbm.at[pl.ds(win * SC_WINDOW, SC_WINDOW)])

    return pl.kernel(
        body,
        out_type=jax.ShapeDtypeStruct((n_win * SC_WINDOW, ROW_WORDS), jnp.int32),
        mesh=plsc.VectorSubcoreMesh(core_axis_name="c", subcore_axis_name="s"),
        scratch_types=[pltpu.VMEM((SC_WINDOW,), jnp.int32),
                       pltpu.VMEM((SC_WINDOW, ROW_WORDS), jnp.int32),
                       pltpu.SemaphoreType.DMA],
        name="moe_gather",
    )(y_words, idx2)


def _combine_kernel(h_ref, y_ref, w_ref, fw_ref, o_ref):
    acc = h_ref[0]
    for kk in range(TOP_K):
        acc = acc + w_ref[0, :, kk:kk + 1] * _unpack_rows(y_ref[kk, 0])
    o_ref[0] = acc * lax.rsqrt(jnp.mean(acc * acc, axis=-1, keepdims=True) + EPS) * fw_ref[...]


def _combine(h1, yg, wts, final_w, b0, nb, lr):
    tr = _pick(lr, (1024, 512, 256, 128))
    return pl.pallas_call(
        _combine_kernel,
        grid=(nb, lr // tr),
        in_specs=[
            pl.BlockSpec((1, tr, D_MODEL), lambda i, j: (b0 + i, j, 0)),
            pl.BlockSpec((TOP_K, 1, tr, ROW_WORDS), lambda i, j: (0, b0 + i, j, 0)),
            pl.BlockSpec((1, tr, TOP_K), lambda i, j: (b0 + i, j, 0)),
            pl.BlockSpec((1, D_MODEL), lambda i, j: (0, 0)),
        ],
        out_specs=pl.BlockSpec((1, tr, D_MODEL), lambda i, j: (i, j, 0)),
        out_shape=jax.ShapeDtypeStruct((nb, lr, D_MODEL), F32),
        compiler_params=pltpu.CompilerParams(
            dimension_semantics=("arbitrary", "arbitrary"), vmem_limit_bytes=VMEM_LIMIT),
        name="combine",
    )(h1, yg, wts, final_w)


def _rope_tables(lr):
    l = lr + N_META
    half = ROT_DIM // 2
    pos = jnp.concatenate([jnp.arange(N_META, l, dtype=F32), jnp.arange(N_META, dtype=F32)])
    inv_freq = ROPE_THETA ** (-jnp.arange(0, ROT_DIM, 2, dtype=F32) / ROT_DIM)
    ang = pos[:, None] * inv_freq[None, :]
    cos, sin = jnp.cos(ang), jnp.sin(ang)
    z = jnp.zeros((l, A_QK_DIM - ROT_DIM), F32)
    zh = jnp.zeros((l, half), F32)
    cos64 = jnp.concatenate([cos, cos, z + 1.0], axis=1)
    sa64 = jnp.concatenate([-sin, zh, z], axis=1)
    sb64 = jnp.concatenate([zh, sin, z], axis=1)
    dup = lambda a: jnp.concatenate([a, a], axis=1)
    return dup(cos64), dup(sa64), dup(sb64)


def kernel(x_prompt, x_sample, meta_tokens, norm1_w, w_in, lambda_q1, lambda_k1, lambda_q2, lambda_k2, attn_subln_w, mlstm_gate_b, mlstm_norm_w, w_br_attn, w_br_mlstm, w_out, norm2_w, router_w, router_b, w_gate, b_gate, w_up, b_up, w_down, b_down, final_norm_w):
    lr = x_prompt.shape[1]
    l = lr + N_META
    assert x_sample.shape[1:] == x_prompt.shape[1:] == (lr, D_MODEL)
    assert lr % (2 * LANES) == 0, "attention pipelines query blocks in pairs; mLSTM chunks are 128 rows"

    w_in0 = w_in[0]
    g0, g1 = OFF_GATES, OFF_GATES + 4 * M_HEADS
    w_main = jnp.concatenate([w_in0[:, :g0], w_in0[:, g1:]], axis=1).astype(BF16)
    w_gates = jnp.pad(w_in0[:, g0:g1], ((0, 0), (0, LANES - 4 * M_HEADS))).astype(BF16)
    lam = (jnp.exp(jnp.sum(lambda_q1[0] * lambda_k1[0])) - jnp.exp(jnp.sum(lambda_q2[0] * lambda_k2[0]))
           + LAM_INIT)
    lam_row = jnp.full((1, LANES), lam, F32)
    sub_w = (attn_subln_w[0] * (1.0 - LAM_INIT)).reshape(1, A_V_DIM)
    neg_row = jnp.where(jnp.arange(LANES) < N_META, 0.0, NEG_BIG).astype(F32).reshape(1, LANES)
    cos_t, sa_t, sb_t = _rope_tables(lr)
    gate_bias = jnp.broadcast_to(
        jnp.pad(mlstm_gate_b[0].T, ((0, 0), (0, 4)))[:, :, None], (M_HEADS, 8, LANES)).astype(F32)
    norm_w_m = mlstm_norm_w[0].reshape(M_HEADS, 1, M_V_DIM)
    rw_t = router_w[0].T.astype(BF16)
    rb_col = jnp.broadcast_to(router_b[0][:, None], (N_EXPERTS, LANES)).astype(F32)
    tri = (jnp.arange(LANES)[:, None] < jnp.arange(LANES)[None, :]).astype(BF16)

    w_a, w_m, w_o = w_br_attn[0].astype(BF16), w_br_mlstm[0].astype(BF16), w_out[0].astype(BF16)
    n1, n2, fw = norm1_w[0].reshape(1, D_MODEL), norm2_w[0].reshape(1, D_MODEL), final_norm_w.reshape(1, D_MODEL)
    bg3, bu3, bd3 = (b_gate[0].reshape(N_EXPERTS, 1, D_FF), b_up[0].reshape(N_EXPERTS, 1, D_FF),
                     b_down[0].reshape(N_EXPERTS, 1, D_MODEL))

    b_all = x_prompt.shape[0] + x_sample.shape[0]
    n_all = b_all * l
    x_all = jnp.concatenate([x_prompt, x_sample], axis=0)
    meta_b = jnp.broadcast_to(meta_tokens[None].astype(x_all.dtype), (b_all, N_META, D_MODEL))
    h0 = jnp.concatenate([x_all, meta_b], axis=1).reshape(n_all, D_MODEL)

    proj, gates = _inproj(h0, n1, w_main, w_gates)
    proj3 = proj.reshape(b_all, l, N_MAIN)
    o_a = _attention(proj3, cos_t, sa_t, sb_t, neg_row, lam_row, sub_w, lr).reshape(n_all, D_MODEL)

    g4 = gates.reshape(b_all, l, 4, M_HEADS).transpose(0, 3, 2, 1)
    off = jnp.array([NEG_BIG, -NEG_BIG, NEG_BIG, -NEG_BIG], F32)[None, None, :, None]
    fill = jnp.broadcast_to(off, (b_all, M_HEADS, 4, CHUNK - N_META))
    g_row = jnp.concatenate([g4[..., :lr], fill, g4[..., lr:]], axis=-1)
    g_row = jnp.pad(g_row, ((0, 0), (0, 0), (0, 4), (0, 0)))
    h_m = _mlstm(proj3, g_row, gate_bias, norm_w_m, lr).reshape(n_all, D_MODEL)

    def channel_mix(b0, b):
        n = b * l
        assert n % SC_WINDOW == 0, "SparseCore dispatch works on whole 128-token windows"
        h1, xn2, top_e, top_w, rank, cnt = _merge(
            o_a, h_m, proj, h0, w_a, w_m, w_o, n2, rw_t, rb_col, tri, b0 * l, n)

        counts = cnt[:, 0].astype(jnp.int32)
        padded = (counts + MOE_ROWS - 1) // MOE_ROWS * MOE_ROWS
        pends = jnp.cumsum(padded)
        pstarts = pends - padded
        e_ids = jnp.arange(N_EXPERTS, dtype=jnp.int32)
        pstart_of = jnp.sum(jnp.where(top_e[..., None] == e_ids, pstarts, 0), axis=-1)
        dest = pstart_of + rank
        n_blocks = (n * TOP_K + N_EXPERTS * (MOE_ROWS - 1) + MOE_ROWS - 1) // MOE_ROWS
        cap = n_blocks * MOE_ROWS
        blk_start = jnp.arange(n_blocks, dtype=jnp.int32) * MOE_ROWS
        blk_e = jnp.minimum(jnp.sum((pends[None, :] <= blk_start[:, None]).astype(jnp.int32), axis=1),
                            N_EXPERTS - 1)
        n_used = (pends[-1:] // MOE_ROWS).astype(jnp.int32)

        dest_win = dest.reshape(TOP_K, n // SC_WINDOW, SC_WINDOW).transpose(1, 0, 2)
        xs = _sc_dispatch(xn2, dest_win, cap)
        y = _experts(blk_e, n_used, xs, w_gate[0], bg3, w_up[0], bu3, w_down[0], bd3)
        yg = _sc_gather(y, dest.reshape(TOP_K * n // SC_WINDOW, SC_WINDOW)).reshape(TOP_K, b, l, ROW_WORDS)
        return _combine(h1.reshape(b, l, D_MODEL), yg, top_w.T.reshape(b, l, TOP_K), fw, 0, b, lr)

    return (channel_mix(0, x_prompt.shape[0]), channel_mix(x_prompt.shape[0], x_sample.shape[0]))
```

```python
import functools
import math

import jax
import jax.numpy as jnp
from jax import lax
from jax.experimental import pallas as pl
from jax.experimental.pallas import tpu as pltpu
from jax.experimental.pallas import tpu_sc as plsc

F32 = jnp.float32
BF16 = jnp.bfloat16

D_MODEL = 1024
N_META = 16
EPS = 1e-6
A_HEADS = 8
A_QK_DIM = 64
A_V_DIM = 128
ROT_DIM = 16
ROPE_THETA = 500000.0
M_HEADS = 4
M_QK_DIM = 128
M_V_DIM = 256
CHUNK = 128
N_EXPERTS = 32
TOP_K = 4
D_FF = 1024
SWIGLU_LIMIT = 7.0
SWIGLU_ALPHA = 1.702
LAM_INIT = 0.8 - 0.6 * math.exp(-0.3 * 0)

LANES = 128
NEG_BIG = -1e30
MOE_ROWS = 512
SC_CORES = 2
SC_SUBCORES = 16
SC_WINDOW = 128
ROW_WORDS = D_MODEL // 2
W_AQK = A_HEADS * 2 * A_QK_DIM
W_AV = A_HEADS * A_V_DIM
W_MQK = M_HEADS * M_QK_DIM
W_MV = M_HEADS * M_V_DIM
OFF_GATES = 2 * W_AQK + W_AV + 2 * W_MQK + 2 * W_MV
N_MAIN = OFF_GATES + 2 * D_MODEL
COL_AQ, COL_AK, COL_AV = 0, W_AQK // LANES, 2 * W_AQK // LANES
COL_MQ = (2 * W_AQK + W_AV) // M_QK_DIM
COL_MK = COL_MQ + M_HEADS
COL_MV = (2 * W_AQK + W_AV + 2 * W_MQK) // M_V_DIM
COL_MO = COL_MV + M_HEADS
COL_GA = OFF_GATES // D_MODEL
COL_GM = COL_GA + 1

VMEM_LIMIT = 56 * 1024 * 1024


def _pick(n, prefs):
    for p in prefs:
        if n % p == 0:
            return p
    raise ValueError(f"no tile for {n}")


def _nt(a, b):
    return lax.dot_general(a, b, (((1,), (1,)), ((), ())), preferred_element_type=F32)


def _dot(a, b):
    return jnp.dot(a, b, preferred_element_type=F32)


def _pack_rows(x):
    w = x.shape[1] // 2
    bits = lambda v: lax.bitcast_convert_type(v.astype(BF16).astype(F32), jnp.uint32)
    lo = lax.shift_right_logical(bits(x[:, :w]), jnp.uint32(16))
    hi = bits(x[:, w:]) & jnp.uint32(0xFFFF0000)
    return lax.bitcast_convert_type(lo | hi, jnp.int32)


def _unpack_rows(wds):
    u = lax.bitcast_convert_type(wds, jnp.uint32)
    lo = lax.bitcast_convert_type(lax.shift_left(u, jnp.uint32(16)), F32)
    hi = lax.bitcast_convert_type(u & jnp.uint32(0xFFFF0000), F32)
    return jnp.concatenate([lo, hi], axis=1)


def _inproj_kernel(x_ref, n1_ref, w_ref, wg_ref, o_ref, g_ref, xn_ref):
    @pl.when(pl.program_id(1) == 0)
    def _():
        x = x_ref[...]
        ms = jnp.mean(x * x, axis=-1, keepdims=True)
        xn = (x * lax.rsqrt(ms + EPS) * n1_ref[...]).astype(BF16)
        xn_ref[...] = xn
        g_ref[...] = _dot(xn, wg_ref[...])[:, :4 * M_HEADS]

    o_ref[...] = _dot(xn_ref[...], w_ref[...]).astype(BF16)


def _inproj(h0, norm1_w, w_main, w_gates):
    n = h0.shape[0]
    tm = _pick(n, (1152, 768, 512, 384, 256, 128))
    tn = 2048
    return pl.pallas_call(
        _inproj_kernel,
        grid=(n // tm, N_MAIN // tn),
        in_specs=[
            pl.BlockSpec((tm, D_MODEL), lambda i, j: (i, 0)),
            pl.BlockSpec((1, D_MODEL), lambda i, j: (0, 0)),
            pl.BlockSpec((D_MODEL, tn), lambda i, j: (0, j)),
            pl.BlockSpec((D_MODEL, LANES), lambda i, j: (0, 0)),
        ],
        out_specs=[
            pl.BlockSpec((tm, tn), lambda i, j: (i, j)),
            pl.BlockSpec((tm, 4 * M_HEADS), lambda i, j: (i, 0)),
        ],
        out_shape=[
            jax.ShapeDtypeStruct((n, N_MAIN), BF16),
            jax.ShapeDtypeStruct((n, 4 * M_HEADS), F32),
        ],
        scratch_shapes=[pltpu.VMEM((tm, D_MODEL), BF16)],
        compiler_params=pltpu.CompilerParams(
            dimension_semantics=("arbitrary", "arbitrary"), vmem_limit_bytes=VMEM_LIMIT),
        name="inproj",
    )(h0, norm1_w, w_main, w_gates)


def _attn_kernel(q_ref, k_ref, v_ref, cos_ref, sa_ref, sb_ref, neg_ref, lam_ref, sw_ref,
                 o_ref, q1_ref, q2_ref, vs_ref, kt_ref, sa1_ref, sa2_ref, sb1_ref, sb2_ref, *, lr):
    l = lr + N_META
    lp = lr + LANES
    nblk = lr // LANES
    scale = A_QK_DIM ** -0.5 * math.log2(math.e)

    def rope(t):
        return (t * cos_ref[...] + pltpu.roll(t, LANES - ROT_DIM // 2, 1) * sa_ref[...]
                + pltpu.roll(t, ROT_DIM // 2, 1) * sb_ref[...])

    lane = lax.broadcasted_iota(jnp.int32, (l, LANES), 1)
    qr = rope(q_ref[0].astype(F32)) * scale
    q1_ref[0:l, :] = jnp.where(lane < A_QK_DIM, qr, 0.0).astype(BF16)
    q2_ref[0:l, :] = jnp.where(lane >= A_QK_DIM, qr, 0.0).astype(BF16)
    vs_ref[0:l, :] = v_ref[0]
    pad = jnp.zeros((lp - l, LANES), BF16)
    q1_ref[l:lp, :] = pad
    q2_ref[l:lp, :] = pad
    vs_ref[l:lp, :] = pad
    kr = rope(k_ref[0].astype(F32))
    for c in range(nblk):
        kt_ref[:, c * LANES:(c + 1) * LANES] = kr[c * LANES:(c + 1) * LANES, :].T.astype(BF16)
    k_tail = jnp.concatenate([kr[lr:l, :], jnp.zeros((lp - l, LANES), F32)], axis=0)
    kt_ref[:, lr:lp] = k_tail.T.astype(BF16)

    lam = lam_ref[:, 0:1]

    def scores(r0, s1_ref, s2_ref):
        rows = pl.ds(r0, LANES)
        k_all = kt_ref[...]
        s1_ref[...] = _dot(q1_ref[rows, :], k_all)
        s2_ref[...] = _dot(q2_ref[rows, :], k_all)

    def softmax_parts(s_ref, nrows):
        sm = s_ref[0:nrows, :lr]
        sl = s_ref[0:nrows, lr:] + neg_ref[...]
        m = jnp.maximum(jnp.max(sm, axis=1, keepdims=True), jnp.max(sl, axis=1, keepdims=True))
        pm = jnp.exp2(sm - m)
        pl_ = jnp.exp2(sl - m)
        tot = jnp.sum(pm, axis=1, keepdims=True) + jnp.sum(pl_, axis=1, keepdims=True)
        return pm, pl_, tot

    def finish(r0, nrows, s1_ref, s2_ref):
        p1m, p1l, t1 = softmax_parts(s1_ref, nrows)
        p2m, p2l, t2 = softmax_parts(s2_ref, nrows)
        c = lam * t1 / t2
        am = (p1m - p2m * c).astype(BF16)
        al = (p1l - p2l * c).astype(BF16)
        o = (_dot(am, vs_ref[0:lr, :]) + _dot(al, vs_ref[lr:lp, :])) * (1.0 / t1)
        o = o * lax.rsqrt(jnp.mean(o * o, axis=-1, keepdims=True) + EPS) * sw_ref[...]
        o_ref[0, pl.ds(r0, nrows), :] = o.astype(BF16)

    scores(0, sa1_ref, sa2_ref)

    pairs = _pick(nblk // 2, (8, 4, 2, 1))

    def body(j, c):
        for u in range(pairs):
            ra = pl.multiple_of((j * pairs + u) * (2 * LANES), LANES)
            rb = ra + LANES
            scores(rb, sb1_ref, sb2_ref)
            finish(ra, LANES, sa1_ref, sa2_ref)
            scores(rb + LANES, sa1_ref, sa2_ref)
            finish(rb, LANES, sb1_ref, sb2_ref)
        return c

    lax.fori_loop(0, nblk // (2 * pairs), body, 0)
    finish(lr, N_META, sa1_ref, sa2_ref)


def _attention(proj3, cos_t, sa_t, sb_t, neg_row, lam_row, sub_w, lr):
    b, l, _ = proj3.shape
    lp = lr + LANES
    const2 = lambda i, h: (0, 0)
    return pl.pallas_call(
        functools.partial(_attn_kernel, lr=lr),
        grid=(b, A_HEADS),
        in_specs=[
            pl.BlockSpec((1, l, LANES), lambda i, h: (i, 0, COL_AQ + h)),
            pl.BlockSpec((1, l, LANES), lambda i, h: (i, 0, COL_AK + h)),
            pl.BlockSpec((1, l, LANES), lambda i, h: (i, 0, COL_AV + h)),
            pl.BlockSpec((l, LANES), const2),
            pl.BlockSpec((l, LANES), const2),
            pl.BlockSpec((l, LANES), const2),
            pl.BlockSpec((1, LANES), const2),
            pl.BlockSpec((1, LANES), const2),
            pl.BlockSpec((1, LANES), const2),
        ],
        out_specs=pl.BlockSpec((1, l, LANES), lambda i, h: (i, 0, h)),
        out_shape=jax.ShapeDtypeStruct((b, l, A_HEADS * A_V_DIM), BF16),
        scratch_shapes=([pltpu.VMEM((lp, LANES), BF16)] * 3 + [pltpu.VMEM((LANES, lp), BF16)]
                        + [pltpu.VMEM((LANES, lp), F32)] * 4),
        compiler_params=pltpu.CompilerParams(
            dimension_semantics=("arbitrary", "arbitrary"), vmem_limit_bytes=VMEM_LIMIT),
        name="diff_attn",
    )(proj3, proj3, proj3, cos_t, sa_t, sb_t, neg_row, lam_row, sub_w)


def _log_sigmoid(x):
    return jnp.minimum(x, 0.0) - jnp.log(1.0 + jnp.exp(-jnp.abs(x)))


def _mlstm_kernel(q_ref, k_ref, v_ref, mo_ref, g_ref, gb_ref, nw_ref, o_ref,
                  c_ref, m_ref, kt_ref, h0_ref, u_ref, bcol_ref, amax_ref, aend_ref, bend_ref,
                  hf_ref, hb_ref, *, lr):
    l = lr + N_META
    t = CHUNK
    nc = lr // t
    dv = M_V_DIM
    scale = M_QK_DIM ** -0.5
    row = lax.broadcasted_iota(jnp.int32, (t, t), 0)
    col = lax.broadcasted_iota(jnp.int32, (t, t), 1)
    eye = row == col
    ones_tile = jnp.where(col == 0, 1.0, 0.0).astype(BF16)
    meta_r0 = l - t

    c_ref[...] = jnp.zeros(c_ref.shape, F32)
    m_ref[...] = jnp.zeros(m_ref.shape, F32)

    for c in range(nc + 1):
        r0 = c * t if c < nc else meta_r0
        kt_ref[:, c * t:(c + 1) * t] = k_ref[0, r0:r0 + t, :].astype(F32).T.astype(BF16)

    def scaled_q(r0):
        return (q_ref[0, pl.ds(r0, t), :].astype(F32) * scale).astype(BF16)

    def intra(ci, r0, g0):
        v_aug = jnp.concatenate([v_ref[0, pl.ds(r0, t), :], ones_tile], axis=1)
        kt = kt_ref[:, pl.ds(g0, t)]
        s_raw = _dot(scaled_q(r0), kt)
        for d, mask in ((0, col <= row), (1, col >= row)):
            gi = 2 * d
            ig = g_ref[0, 0, gi:gi + 1, pl.ds(g0, t)] + gb_ref[0, gi:gi + 1, :]
            lf = _log_sigmoid(g_ref[0, 0, gi + 1:gi + 2, pl.ds(g0, t)] + gb_ref[0, gi + 1:gi + 2, :])
            b_col = jnp.sum(jnp.where(mask, jnp.broadcast_to(lf, (t, t)), 0.0), axis=1, keepdims=True)
            b_row = jnp.sum(jnp.where(eye, jnp.broadcast_to(b_col, (t, t)), 0.0), axis=0, keepdims=True)
            a_row = ig - b_row
            dm = jnp.where(mask, jnp.broadcast_to(a_row, (t, t)), NEG_BIG)
            amax = jnp.max(dm, axis=1, keepdims=True)
            s0 = s_raw * jnp.exp(dm - amax)
            h0_ref[d, pl.ds(g0, t), :] = _dot(s0.astype(BF16), v_aug)
            a_end = jnp.max(a_row, axis=1, keepdims=True)
            kts = (kt.astype(F32) * jnp.exp(a_row - a_end)).astype(BF16)
            u_ref[d, ci] = _dot(kts, v_aug)
            bcol_ref[d, pl.ds(g0, t), :] = b_col
            amax_ref[d, pl.ds(g0, t), :] = amax
            aend_ref[d, ci] = jnp.broadcast_to(a_end, (8, LANES))
            bend_ref[d, ci] = jnp.broadcast_to(jnp.sum(lf, axis=1, keepdims=True), (8, LANES))

    ua = _pick(nc, (16, 8, 4, 2, 1))

    def abody(i, c):
        for u in range(ua):
            ci = ua * i + u
            r0 = pl.multiple_of(ci * t, t)
            intra(ci, r0, r0)
        return c

    lax.fori_loop(0, nc // ua, abody, 0)
    intra(nc, meta_r0, lr)

    def step(d, ci, r0, g0):
        m_prev = m_ref[d, 0:1, 0:1]
        amax = amax_ref[d, pl.ds(g0, t), :]
        m_col = jnp.maximum(m_prev, amax)
        num = (jnp.exp(m_prev - m_col) * _dot(scaled_q(r0), c_ref[d].astype(BF16))
               + jnp.exp(amax - m_col) * h0_ref[d, pl.ds(g0, t), :])
        den = jnp.maximum(jnp.abs(num[:, dv:dv + 1]), jnp.exp(-(bcol_ref[d, pl.ds(g0, t), :] + m_col)))
        a_end = aend_ref[d, ci, 0:1, 0:1]
        m_end = jnp.maximum(m_prev, a_end)
        c_ref[d] = jnp.exp(m_prev - m_end) * c_ref[d] + jnp.exp(a_end - m_end) * u_ref[d, ci]
        m_ref[d] = jnp.broadcast_to(bend_ref[d, ci, 0:1, 0:1] + m_end, (8, LANES))
        return num[:, :dv] * (1.0 / den)

    hf_ref[lr:l, :] = step(0, nc, meta_r0, lr)[t - N_META:, :]

    ub = _pick(nc, (2, 1))

    def bbody(i, c):
        for u in range(ub):
            jf = ub * i + u
            rf = pl.multiple_of(jf * t, t)
            hf_ref[pl.ds(rf, t), :] = step(0, jf, rf, rf)
            jb = nc - 1 - jf
            rb = pl.multiple_of(jb * t, t)
            hb_ref[pl.ds(rb, t), :] = step(1, jb, rb, rb)
        return c

    lax.fori_loop(0, nc // ub, bbody, 0)
    hb_ref[lr:l, :] = step(1, nc, meta_r0, lr)[t - N_META:, :]

    def finish(r0, rows):
        hs = hf_ref[pl.ds(r0, rows), :] + hb_ref[pl.ds(r0, rows), :]
        y = hs * lax.rsqrt(jnp.mean(hs * hs, axis=-1, keepdims=True) + EPS) * nw_ref[0]
        gate = jax.nn.sigmoid(mo_ref[0, pl.ds(r0, rows), :].astype(F32))
        o_ref[0, pl.ds(r0, rows), :] = (y * gate).astype(BF16)

    def fbody(i, c):
        for u in range(ua):
            finish(pl.multiple_of((ua * i + u) * t, t), t)
        return c

    lax.fori_loop(0, nc // ua, fbody, 0)
    finish(lr, N_META)


def _mlstm(proj3, gates_row, gate_bias, norm_w, lr):
    b, l, _ = proj3.shape
    gl = gates_row.shape[-1]
    nw = lr // CHUNK + 1
    aug = M_V_DIM + LANES
    return pl.pallas_call(
        functools.partial(_mlstm_kernel, lr=lr),
        grid=(b, M_HEADS),
        in_specs=[
            pl.BlockSpec((1, l, M_QK_DIM), lambda i, h: (i, 0, COL_MQ + h)),
            pl.BlockSpec((1, l, M_QK_DIM), lambda i, h: (i, 0, COL_MK + h)),
            pl.BlockSpec((1, l, M_V_DIM), lambda i, h: (i, 0, COL_MV + h)),
            pl.BlockSpec((1, l, M_V_DIM), lambda i, h: (i, 0, COL_MO + h)),
            pl.BlockSpec((1, 1, 8, gl), lambda i, h: (i, h, 0, 0)),
            pl.BlockSpec((1, 8, LANES), lambda i, h: (h, 0, 0)),
            pl.BlockSpec((1, 1, M_V_DIM), lambda i, h: (h, 0, 0)),
        ],
        out_specs=pl.BlockSpec((1, l, M_V_DIM), lambda i, h: (i, 0, h)),
        out_shape=jax.ShapeDtypeStruct((b, l, M_HEADS * M_V_DIM), BF16),
        scratch_shapes=[
            pltpu.VMEM((2, M_QK_DIM, aug), F32),
            pltpu.VMEM((2, 8, LANES), F32),
            pltpu.VMEM((M_QK_DIM, gl), BF16),
            pltpu.VMEM((2, gl, aug), F32),
            pltpu.VMEM((2, nw, M_QK_DIM, aug), F32),
            pltpu.VMEM((2, gl, 1), F32),
            pltpu.VMEM((2, gl, 1), F32),
            pltpu.VMEM((2, nw, 8, LANES), F32),
            pltpu.VMEM((2, nw, 8, LANES), F32),
            pltpu.VMEM((l, M_V_DIM), F32), pltpu.VMEM((l, M_V_DIM), F32),
        ],
        compiler_params=pltpu.CompilerParams(
            dimension_semantics=("arbitrary", "arbitrary"), vmem_limit_bytes=VMEM_LIMIT),
        name="mlstm",
    )(proj3, proj3, proj3, proj3, gates_row, gate_bias, norm_w)


def _merge_kernel(oa_ref, hm_ref, ga_ref, gm_ref, h_ref, wa_ref, wm_ref, wo_ref, n2_ref,
                  rw_ref, rb_ref, tri_ref,
                  h1_ref, xn_ref, te_ref, tw_ref, rk_ref, cnt_ref, carry_ref, *, tm):
    @pl.when(pl.program_id(0) == 0)
    def _():
        carry_ref[...] = jnp.zeros(carry_ref.shape, F32)

    pa = _dot(oa_ref[...], wa_ref[...])
    pm = _dot(hm_ref[...], wm_ref[...])
    merged = (jax.nn.sigmoid(ga_ref[...].astype(F32)) * pa
              + jax.nn.sigmoid(gm_ref[...].astype(F32)) * pm)
    h1 = h_ref[...] + _dot(merged.astype(BF16), wo_ref[...])
    h1_ref[...] = h1
    xn32 = h1 * lax.rsqrt(jnp.mean(h1 * h1, axis=-1, keepdims=True) + EPS) * n2_ref[...]
    xn = xn32.astype(BF16)
    xn_ref[...] = _pack_rows(xn32)

    logits = _nt(rw_ref[...], xn) + rb_ref[:, 0:1]
    eidx = lax.broadcasted_iota(jnp.int32, (N_EXPERTS, tm), 0)
    work = logits
    vals, hots = [], []
    for _ in range(TOP_K):
        mx = jnp.max(work, axis=0, keepdims=True)
        sel = jnp.min(jnp.where(work == mx, eidx, N_EXPERTS), axis=0, keepdims=True)
        hot = eidx == sel
        vals.append(mx)
        hots.append(hot)
        work = jnp.where(hot, -jnp.inf, work)
    ex = [jnp.exp(v - vals[0]) for v in vals]
    inv = 1.0 / (ex[0] + ex[1] + ex[2] + ex[3])
    chosen = jnp.where(hots[0] | hots[1] | hots[2] | hots[3], 1.0, 0.0)

    carry = carry_ref[:, 0:1]
    ranks = []
    for s in range(tm // LANES):
        sub = chosen[:, s * LANES:(s + 1) * LANES]
        ranks.append(_dot(sub.astype(BF16), tri_ref[...]) + carry)
        carry = carry + jnp.sum(sub, axis=1, keepdims=True)
    before = jnp.concatenate(ranks, axis=1) if len(ranks) > 1 else ranks[0]
    carry_ref[...] = jnp.broadcast_to(carry, carry_ref.shape)
    cnt_ref[...] = jnp.broadcast_to(carry, cnt_ref.shape)

    for kk in range(TOP_K):
        te_ref[kk:kk + 1, :] = jnp.sum(jnp.where(hots[kk], eidx, 0), axis=0, keepdims=True)
        tw_ref[kk:kk + 1, :] = ex[kk] * inv
        rk_ref[kk:kk + 1, :] = jnp.sum(jnp.where(hots[kk], before, 0.0), axis=0,
                                       keepdims=True).astype(jnp.int32)


def _merge(o_a, h_m, proj, h0, w_a, w_m, w_o, norm2_w, rw_t, rb_col, tri, row0, n):
    tm = _pick(math.gcd(n, row0) if row0 else n, (384, 256, 128))
    t0 = row0 // tm
    row = lambda i: (i, 0)
    src = lambda i: (t0 + i, 0)
    const = lambda i: (0, 0)
    tok = lambda i: (0, i)
    wspec = pl.BlockSpec((D_MODEL, D_MODEL), const)
    return pl.pallas_call(
        functools.partial(_merge_kernel, tm=tm),
        grid=(n // tm,),
        in_specs=[
            pl.BlockSpec((tm, D_MODEL), src),
            pl.BlockSpec((tm, D_MODEL), src),
            pl.BlockSpec((tm, D_MODEL), lambda i: (t0 + i, COL_GA)),
            pl.BlockSpec((tm, D_MODEL), lambda i: (t0 + i, COL_GM)),
            pl.BlockSpec((tm, D_MODEL), src),
            wspec, wspec, wspec,
            pl.BlockSpec((1, D_MODEL), const),
            pl.BlockSpec((N_EXPERTS, D_MODEL), const),
            pl.BlockSpec((N_EXPERTS, LANES), const),
            pl.BlockSpec((LANES, LANES), const),
        ],
        out_specs=[
            pl.BlockSpec((tm, D_MODEL), row),
            pl.BlockSpec((tm, ROW_WORDS), row),
            pl.BlockSpec((TOP_K, tm), tok),
            pl.BlockSpec((TOP_K, tm), tok),
            pl.BlockSpec((TOP_K, tm), tok),
            pl.BlockSpec((N_EXPERTS, LANES), const),
        ],
        out_shape=[
            jax.ShapeDtypeStruct((n, D_MODEL), F32),
            jax.ShapeDtypeStruct((n, ROW_WORDS), jnp.int32),
            jax.ShapeDtypeStruct((TOP_K, n), jnp.int32),
            jax.ShapeDtypeStruct((TOP_K, n), F32),
            jax.ShapeDtypeStruct((TOP_K, n), jnp.int32),
            jax.ShapeDtypeStruct((N_EXPERTS, LANES), F32),
        ],
        scratch_shapes=[pltpu.VMEM((N_EXPERTS, LANES), F32)],
        compiler_params=pltpu.CompilerParams(
            dimension_semantics=("arbitrary",), vmem_limit_bytes=VMEM_LIMIT),
        name="merge_router",
    )(o_a, h_m, proj, proj, h0, w_a, w_m, w_o, norm2_w, rw_t, rb_col, tri)


def _expert_kernel(be_ref, nu_ref, x_ref, wg_ref, bg_ref, wu_ref, bu_ref, wd_ref, bd_ref, y_ref,
                   wg_s, wu_s, wd_s):
    i = pl.program_id(0)
    used = i < nu_ref[0]

    @pl.when(jnp.logical_or(i == 0, be_ref[i] != be_ref[jnp.maximum(i - 1, 0)]))
    def _():
        wg_s[...] = wg_ref[0].astype(BF16)
        wu_s[...] = wu_ref[0].astype(BF16)
        wd_s[...] = wd_ref[0].astype(BF16)

    @pl.when(used)
    def _():
        x = _unpack_rows(x_ref[...]).astype(BF16)
        gt = jnp.minimum(_dot(x, wg_s[...]) + bg_ref[0], SWIGLU_LIMIT)
        up = jnp.clip(_dot(x, wu_s[...]) + bu_ref[0], -SWIGLU_LIMIT, SWIGLU_LIMIT)
        glu = gt * jax.nn.sigmoid(SWIGLU_ALPHA * gt)
        act = ((up + 1.0) * glu).astype(BF16)
        y_ref[...] = _pack_rows(_dot(act, wd_s[...]) + bd_ref[0])

    @pl.when(jnp.logical_not(used))
    def _():
        y_ref[...] = jnp.zeros(y_ref.shape, jnp.int32)


def _experts(blk_e, n_used, xs, w_g, b_g, w_u, b_u, w_d, b_d):
    cap = xs.shape[0]
    n_blocks = cap // MOE_ROWS
    wspec = lambda d0, d1: pl.BlockSpec((1, d0, d1), lambda i, be, nu: (be[i], 0, 0))
    return pl.pallas_call(
        _expert_kernel,
        grid_spec=pltpu.PrefetchScalarGridSpec(
            num_scalar_prefetch=2,
            grid=(n_blocks,),
            in_specs=[
                pl.BlockSpec((MOE_ROWS, ROW_WORDS), lambda i, be, nu: (i, 0)),
                wspec(D_MODEL, D_FF), wspec(1, D_FF),
                wspec(D_MODEL, D_FF), wspec(1, D_FF),
                wspec(D_FF, D_MODEL), wspec(1, D_MODEL),
            ],
            out_specs=pl.BlockSpec((MOE_ROWS, ROW_WORDS), lambda i, be, nu: (i, 0)),
            scratch_shapes=[pltpu.VMEM((D_MODEL, D_FF), BF16), pltpu.VMEM((D_MODEL, D_FF), BF16),
                            pltpu.VMEM((D_FF, D_MODEL), BF16)],
        ),
        out_shape=jax.ShapeDtypeStruct((cap, ROW_WORDS), jnp.int32),
        compiler_params=pltpu.CompilerParams(
            dimension_semantics=("arbitrary",), vmem_limit_bytes=VMEM_LIMIT),
        name="experts",
    )(blk_e, n_used, xs, w_g, b_g, w_u, b_u, w_d, b_d)


def _sc_worker_windows(n_win):
    return -(-n_win // (SC_CORES * SC_SUBCORES))


def _sc_dispatch(x_words, dest3, cap):
    n = x_words.shape[0]
    n_win = n // SC_WINDOW
    per = _sc_worker_windows(n_win)

    def body(x_hbm, d_hbm, o_hbm, idx_v, rows_v):
        wid = lax.axis_index("s") * SC_CORES + lax.axis_index("c")

        @pl.loop(0, per)
        def _(i):
            win = jnp.minimum(wid * per + i, n_win - 1)
            pltpu.sync_copy(d_hbm.at[win], idx_v)
            pltpu.sync_copy(x_hbm.at[pl.ds(win * SC_WINDOW, SC_WINDOW)], rows_v)
            for kk in range(TOP_K):
                pltpu.sync_copy(rows_v, o_hbm.at[idx_v.at[kk]])

    return pl.kernel(
        body,
        out_type=jax.ShapeDtypeStruct((cap, ROW_WORDS), jnp.int32),
        mesh=plsc.VectorSubcoreMesh(core_axis_name="c", subcore_axis_name="s"),
        scratch_types=[pltpu.VMEM((TOP_K, SC_WINDOW), jnp.int32),
                       pltpu.VMEM((SC_WINDOW, ROW_WORDS), jnp.int32)],
        name="moe_dispatch",
    )(x_words, dest3)


def _sc_gather(y_words, idx2):
    n_win = idx2.shape[0]
    per = _sc_worker_windows(n_win)

    def body(y_hbm, i_hbm, o_hbm, idx_v, rows_v, sem):
        wid = lax.axis_index("s") * SC_CORES + lax.axis_index("c")

        @pl.loop(0, per)
        def _(i):
            win = jnp.minimum(wid * per + i, n_win - 1)
            pltpu.sync_copy(i_hbm.at[win], idx_v)
            pltpu.async_copy(y_hbm.at[idx_v], rows_v, sem).wait()
            pltpu.sync_copy(rows_v, o_hbm.at[pl.ds(win * SC_WINDOW, SC_WINDOW)])

    return pl.kernel(
        body,
        out_type=jax.ShapeDtypeStruct((n_win * SC_WINDOW, ROW_WORDS), jnp.int32),
        mesh=plsc.VectorSubcoreMesh(core_axis_name="c", subcore_axis_name="s"),
        scratch_types=[pltpu.VMEM((SC_WINDOW,), jnp.int32),
                       pltpu.VMEM((SC_WINDOW, ROW_WORDS), jnp.int32),
                       pltpu.SemaphoreType.DMA],
        name="moe_gather",
    )(y_words, idx2)


def _combine_kernel(h_ref, y_ref, w_ref, fw_ref, o_ref):
    acc = h_ref[0]
    for kk in range(TOP_K):
        acc = acc + w_ref[0, :, kk:kk + 1] * _unpack_rows(y_ref[kk, 0])
    o_ref[0] = acc * lax.rsqrt(jnp.mean(acc * acc, axis=-1, keepdims=True) + EPS) * fw_ref[...]


def _combine(h1, yg, wts, final_w, b0, nb, lr):
    tr = _pick(lr, (1024, 512, 256, 128))
    return pl.pallas_call(
        _combine_kernel,
        grid=(nb, lr // tr),
        in_specs=[
            pl.BlockSpec((1, tr, D_MODEL), lambda i, j: (b0 + i, j, 0)),
            pl.BlockSpec((TOP_K, 1, tr, ROW_WORDS), lambda i, j: (0, b0 + i, j, 0)),
            pl.BlockSpec((1, tr, TOP_K), lambda i, j: (b0 + i, j, 0)),
            pl.BlockSpec((1, D_MODEL), lambda i, j: (0, 0)),
        ],
        out_specs=pl.BlockSpec((1, tr, D_MODEL), lambda i, j: (i, j, 0)),
        out_shape=jax.ShapeDtypeStruct((nb, lr, D_MODEL), F32),
        compiler_params=pltpu.CompilerParams(
            dimension_semantics=("arbitrary", "arbitrary"), vmem_limit_bytes=VMEM_LIMIT),
        name="combine",
    )(h1, yg, wts, final_w)


def _rope_tables(lr):
    l = lr + N_META
    half = ROT_DIM // 2
    pos = jnp.concatenate([jnp.arange(N_META, l, dtype=F32), jnp.arange(N_META, dtype=F32)])
    inv_freq = ROPE_THETA ** (-jnp.arange(0, ROT_DIM, 2, dtype=F32) / ROT_DIM)
    ang = pos[:, None] * inv_freq[None, :]
    cos, sin = jnp.cos(ang), jnp.sin(ang)
    z = jnp.zeros((l, A_QK_DIM - ROT_DIM), F32)
    zh = jnp.zeros((l, half), F32)
    cos64 = jnp.concatenate([cos, cos, z + 1.0], axis=1)
    sa64 = jnp.concatenate([-sin, zh, z], axis=1)
    sb64 = jnp.concatenate([zh, sin, z], axis=1)
    dup = lambda a: jnp.concatenate([a, a], axis=1)
    return dup(cos64), dup(sa64), dup(sb64)


def kernel(x_prompt, x_sample, meta_tokens, norm1_w, w_in, lambda_q1, lambda_k1, lambda_q2, lambda_k2, attn_subln_w, mlstm_gate_b, mlstm_norm_w, w_br_attn, w_br_mlstm, w_out, norm2_w, router_w, router_b, w_gate, b_gate, w_up, b_up, w_down, b_down, final_norm_w):
    lr = x_prompt.shape[1]
    l = lr + N_META
    assert x_sample.shape[1:] == x_prompt.shape[1:] == (lr, D_MODEL)
    assert lr % (2 * LANES) == 0, "attention pipelines query blocks in pairs; mLSTM chunks are 128 rows"

    w_in0 = w_in[0]
    g0, g1 = OFF_GATES, OFF_GATES + 4 * M_HEADS
    w_main = jnp.concatenate([w_in0[:, :g0], w_in0[:, g1:]], axis=1).astype(BF16)
    w_gates = jnp.pad(w_in0[:, g0:g1], ((0, 0), (0, LANES - 4 * M_HEADS))).astype(BF16)
    lam = (jnp.exp(jnp.sum(lambda_q1[0] * lambda_k1[0])) - jnp.exp(jnp.sum(lambda_q2[0] * lambda_k2[0]))
           + LAM_INIT)
    lam_row = jnp.full((1, LANES), lam, F32)
    sub_w = (attn_subln_w[0] * (1.0 - LAM_INIT)).reshape(1, A_V_DIM)
    neg_row = jnp.where(jnp.arange(LANES) < N_META, 0.0, NEG_BIG).astype(F32).reshape(1, LANES)
    cos_t, sa_t, sb_t = _rope_tables(lr)
    gate_bias = jnp.broadcast_to(
        jnp.pad(mlstm_gate_b[0].T, ((0, 0), (0, 4)))[:, :, None], (M_HEADS, 8, LANES)).astype(F32)
    norm_w_m = mlstm_norm_w[0].reshape(M_HEADS, 1, M_V_DIM)
    rw_t = router_w[0].T.astype(BF16)
    rb_col = jnp.broadcast_to(router_b[0][:, None], (N_EXPERTS, LANES)).astype(F32)
    tri = (jnp.arange(LANES)[:, None] < jnp.arange(LANES)[None, :]).astype(BF16)

    w_a, w_m, w_o = w_br_attn[0].astype(BF16), w_br_mlstm[0].astype(BF16), w_out[0].astype(BF16)
    n1, n2, fw = norm1_w[0].reshape(1, D_MODEL), norm2_w[0].reshape(1, D_MODEL), final_norm_w.reshape(1, D_MODEL)
    bg3, bu3, bd3 = (b_gate[0].reshape(N_EXPERTS, 1, D_FF), b_up[0].reshape(N_EXPERTS, 1, D_FF),
                     b_down[0].reshape(N_EXPERTS, 1, D_MODEL))

    b_all = x_prompt.shape[0] + x_sample.shape[0]
    n_all = b_all * l
    x_all = jnp.concatenate([x_prompt, x_sample], axis=0)
    meta_b = jnp.broadcast_to(meta_tokens[None].astype(x_all.dtype), (b_all, N_META, D_MODEL))
    h0 = jnp.concatenate([x_all, meta_b], axis=1).reshape(n_all, D_MODEL)

    proj, gates = _inproj(h0, n1, w_main, w_gates)
    proj3 = proj.reshape(b_all, l, N_MAIN)
    o_a = _attention(proj3, cos_t, sa_t, sb_t, neg_row, lam_row, sub_w, lr).reshape(n_all, D_MODEL)

    g4 = gates.reshape(b_all, l, 4, M_HEADS).transpose(0, 3, 2, 1)
    off = jnp.array([NEG_BIG, -NEG_BIG, NEG_BIG, -NEG_BIG], F32)[None, None, :, None]
    fill = jnp.broadcast_to(off, (b_all, M_HEADS, 4, CHUNK - N_META))
    g_row = jnp.concatenate([g4[..., :lr], fill, g4[..., lr:]], axis=-1)
    g_row = jnp.pad(g_row, ((0, 0), (0, 0), (0, 4), (0, 0)))
    h_m = _mlstm(proj3, g_row, gate_bias, norm_w_m, lr).reshape(n_all, D_MODEL)

    def channel_mix(b0, b):
        n = b * l
        assert n % SC_WINDOW == 0, "SparseCore dispatch works on whole 128-token windows"
        h1, xn2, top_e, top_w, rank, cnt = _merge(
            o_a, h_m, proj, h0, w_a, w_m, w_o, n2, rw_t, rb_col, tri, b0 * l, n)

        counts = cnt[:, 0].astype(jnp.int32)
        padded = (counts + MOE_ROWS - 1) // MOE_ROWS * MOE_ROWS
        pends = jnp.cumsum(padded)
        pstarts = pends - padded
        e_ids = jnp.arange(N_EXPERTS, dtype=jnp.int32)
        pstart_of = jnp.sum(jnp.where(top_e[..., None] == e_ids, pstarts, 0), axis=-1)
        dest = pstart_of + rank
        n_blocks = (n * TOP_K + N_EXPERTS * (MOE_ROWS - 1) + MOE_ROWS - 1) // MOE_ROWS
        cap = n_blocks * MOE_ROWS
        blk_start = jnp.arange(n_blocks, dtype=jnp.int32) * MOE_ROWS
        blk_e = jnp.minimum(jnp.sum((pends[None, :] <= blk_start[:, None]).astype(jnp.int32), axis=1),
                            N_EXPERTS - 1)
        n_used = (pends[-1:] // MOE_ROWS).astype(jnp.int32)

        dest_win = dest.reshape(TOP_K, n // SC_WINDOW, SC_WINDOW).transpose(1, 0, 2)
        xs = _sc_dispatch(xn2, dest_win, cap)
        y = _experts(blk_e, n_used, xs, w_gate[0], bg3, w_up[0], bu3, w_down[0], bd3)
        yg = _sc_gather(y, dest.reshape(TOP_K * n // SC_WINDOW, SC_WINDOW)).reshape(TOP_K, b, l, ROW_WORDS)
        return _combine(h1.reshape(b, l, D_MODEL), yg, top_w.T.reshape(b, l, TOP_K), fw, 0, b, lr)

    return (channel_mix(0, x_prompt.shape[0]), channel_mix(x_prompt.shape[0], x_sample.shape[0]))
```

```python
import functools
import math

import jax
import jax.numpy as jnp
from jax import lax
from jax.experimental import pallas as pl
from jax.experimental.pallas import tpu as pltpu
from jax.experimental.pallas import tpu_sc as plsc

F32 = jnp.float32
BF16 = jnp.bfloat16

D_MODEL = 1024
N_META = 16
EPS = 1e-6
A_HEADS = 8
A_QK_DIM = 64
A_V_DIM = 128
ROT_DIM = 16
ROPE_THETA = 500000.0
M_HEADS = 4
M_QK_DIM = 128
M_V_DIM = 256
CHUNK = 128
N_EXPERTS = 32
TOP_K = 4
D_FF = 1024
SWIGLU_LIMIT = 7.0
SWIGLU_ALPHA = 1.702
LAM_INIT = 0.8 - 0.6 * math.exp(-0.3 * 0)

LANES = 128
NEG_BIG = -1e30
MOE_ROWS = 512
SC_CORES = 2
SC_SUBCORES = 16
SC_WINDOW = 128
ROW_WORDS = D_MODEL // 2
W_AQK = A_HEADS * 2 * A_QK_DIM
W_AV = A_HEADS * A_V_DIM
W_MQK = M_HEADS * M_QK_DIM
W_MV = M_HEADS * M_V_DIM
OFF_GATES = 2 * W_AQK + W_AV + 2 * W_MQK + 2 * W_MV
N_MAIN = OFF_GATES + 2 * D_MODEL
COL_AQ, COL_AK, COL_AV = 0, W_AQK // LANES, 2 * W_AQK // LANES
COL_MQ = (2 * W_AQK + W_AV) // M_QK_DIM
COL_MK = COL_MQ + M_HEADS
COL_MV = (2 * W_AQK + W_AV + 2 * W_MQK) // M_V_DIM
COL_MO = COL_MV + M_HEADS
COL_GA = OFF_GATES // D_MODEL
COL_GM = COL_GA + 1

VMEM_LIMIT = 56 * 1024 * 1024


def _pick(n, prefs):
    for p in prefs:
        if n % p == 0:
            return p
    raise ValueError(f"no tile for {n}")


def _nt(a, b):
    return lax.dot_general(a, b, (((1,), (1,)), ((), ())), preferred_element_type=F32)


def _dot(a, b):
    return jnp.dot(a, b, preferred_element_type=F32)


def _pack_rows(x):
    w = x.shape[1] // 2
    bits = lambda v: lax.bitcast_convert_type(v.astype(BF16).astype(F32), jnp.uint32)
    lo = lax.shift_right_logical(bits(x[:, :w]), jnp.uint32(16))
    hi = bits(x[:, w:]) & jnp.uint32(0xFFFF0000)
    return lax.bitcast_convert_type(lo | hi, jnp.int32)


def _unpack_rows(wds):
    u = lax.bitcast_convert_type(wds, jnp.uint32)
    lo = lax.bitcast_convert_type(lax.shift_left(u, jnp.uint32(16)), F32)
    hi = lax.bitcast_convert_type(u & jnp.uint32(0xFFFF0000), F32)
    return jnp.concatenate([lo, hi], axis=1)


def _inproj_kernel(x_ref, n1_ref, w_ref, wg_ref, o_ref, g_ref, xn_ref):
    @pl.when(pl.program_id(1) == 0)
    def _():
        x = x_ref[...]
        ms = jnp.mean(x * x, axis=-1, keepdims=True)
        xn = (x * lax.rsqrt(ms + EPS) * n1_ref[...]).astype(BF16)
        xn_ref[...] = xn
        g_ref[...] = _dot(xn, wg_ref[...])[:, :4 * M_HEADS]

    o_ref[...] = _dot(xn_ref[...], w_ref[...]).astype(BF16)


def _inproj(h0, norm1_w, w_main, w_gates):
    n = h0.shape[0]
    tm = _pick(n, (1152, 768, 512, 384, 256, 128))
    tn = 2048
    return pl.pallas_call(
        _inproj_kernel,
        grid=(n // tm, N_MAIN // tn),
        in_specs=[
            pl.BlockSpec((tm, D_MODEL), lambda i, j: (i, 0)),
            pl.BlockSpec((1, D_MODEL), lambda i, j: (0, 0)),
            pl.BlockSpec((D_MODEL, tn), lambda i, j: (0, j)),
            pl.BlockSpec((D_MODEL, LANES), lambda i, j: (0, 0)),
        ],
        out_specs=[
            pl.BlockSpec((tm, tn), lambda i, j: (i, j)),
            pl.BlockSpec((tm, 4 * M_HEADS), lambda i, j: (i, 0)),
        ],
        out_shape=[
            jax.ShapeDtypeStruct((n, N_MAIN), BF16),
            jax.ShapeDtypeStruct((n, 4 * M_HEADS), F32),
        ],
        scratch_shapes=[pltpu.VMEM((tm, D_MODEL), BF16)],
        compiler_params=pltpu.CompilerParams(
            dimension_semantics=("arbitrary", "arbitrary"), vmem_limit_bytes=VMEM_LIMIT),
        name="inproj",
    )(h0, norm1_w, w_main, w_gates)


def _attn_kernel(q_ref, k_ref, v_ref, cos_ref, sa_ref, sb_ref, neg_ref, lam_ref, sw_ref,
                 o_ref, q1_ref, q2_ref, vs_ref, kt_ref, sa1_ref, sa2_ref, sb1_ref, sb2_ref, *, lr):
    l = lr + N_META
    lp = lr + LANES
    nblk = lr // LANES
    scale = A_QK_DIM ** -0.5 * math.log2(math.e)

    def rope(t):
        return (t * cos_ref[...] + pltpu.roll(t, LANES - ROT_DIM // 2, 1) * sa_ref[...]
                + pltpu.roll(t, ROT_DIM // 2, 1) * sb_ref[...])

    lane = lax.broadcasted_iota(jnp.int32, (l, LANES), 1)
    qr = rope(q_ref[0].astype(F32)) * scale
    q1_ref[0:l, :] = jnp.where(lane < A_QK_DIM, qr, 0.0).astype(BF16)
    q2_ref[0:l, :] = jnp.where(lane >= A_QK_DIM, qr, 0.0).astype(BF16)
    vs_ref[0:l, :] = v_ref[0]
    pad = jnp.zeros((lp - l, LANES), BF16)
    q1_ref[l:lp, :] = pad
    q2_ref[l:lp, :] = pad
    vs_ref[l:lp, :] = pad
    kr = rope(k_ref[0].astype(F32))
    for c in range(nblk):
        kt_ref[:, c * LANES:(c + 1) * LANES] = kr[c * LANES:(c + 1) * LANES, :].T.astype(BF16)
    k_tail = jnp.concatenate([kr[lr:l, :], jnp.zeros((lp - l, LANES), F32)], axis=0)
    kt_ref[:, lr:lp] = k_tail.T.astype(BF16)

    lam = lam_ref[:, 0:1]

    def scores(r0, s1_ref, s2_ref):
        rows = pl.ds(r0, LANES)
        k_all = kt_ref[...]
        s1_ref[...] = _dot(q1_ref[rows, :], k_all)
        s2_ref[...] = _dot(q2_ref[rows, :], k_all)

    def softmax_parts(s_ref, nrows):
        sm = s_ref[0:nrows, :lr]
        sl = s_ref[0:nrows, lr:] + neg_ref[...]
        m = jnp.maximum(jnp.max(sm, axis=1, keepdims=True), jnp.max(sl, axis=1, keepdims=True))
        pm = jnp.exp2(sm - m)
        pl_ = jnp.exp2(sl - m)
        tot = jnp.sum(pm, axis=1, keepdims=True) + jnp.sum(pl_, axis=1, keepdims=True)
        return pm, pl_, tot

    def finish(r0, nrows, s1_ref, s2_ref):
        p1m, p1l, t1 = softmax_parts(s1_ref, nrows)
        p2m, p2l, t2 = softmax_parts(s2_ref, nrows)
        c = lam * t1 / t2
        am = (p1m - p2m * c).astype(BF16)
        al = (p1l - p2l * c).astype(BF16)
        o = (_dot(am, vs_ref[0:lr, :]) + _dot(al, vs_ref[lr:lp, :])) * (1.0 / t1)
        o = o * lax.rsqrt(jnp.mean(o * o, axis=-1, keepdims=True) + EPS) * sw_ref[...]
        o_ref[0, pl.ds(r0, nrows), :] = o.astype(BF16)

    scores(0, sa1_ref, sa2_ref)

    pairs = _pick(nblk // 2, (8, 4, 2, 1))

    def body(j, c):
        for u in range(pairs):
            ra = pl.multiple_of((j * pairs + u) * (2 * LANES), LANES)
            rb = ra + LANES
            scores(rb, sb1_ref, sb2_ref)
            finish(ra, LANES, sa1_ref, sa2_ref)
            scores(rb + LANES, sa1_ref, sa2_ref)
            finish(rb, LANES, sb1_ref, sb2_ref)
        return c

    lax.fori_loop(0, nblk // (2 * pairs), body, 0)
    finish(lr, N_META, sa1_ref, sa2_ref)


def _attention(proj3, cos_t, sa_t, sb_t, neg_row, lam_row, sub_w, lr):
    b, l, _ = proj3.shape
    lp = lr + LANES
    const2 = lambda i, h: (0, 0)
    return pl.pallas_call(
        functools.partial(_attn_kernel, lr=lr),
        grid=(b, A_HEADS),
        in_specs=[
            pl.BlockSpec((1, l, LANES), lambda i, h: (i, 0, COL_AQ + h)),
            pl.BlockSpec((1, l, LANES), lambda i, h: (i, 0, COL_AK + h)),
            pl.BlockSpec((1, l, LANES), lambda i, h: (i, 0, COL_AV + h)),
            pl.BlockSpec((l, LANES), const2),
            pl.BlockSpec((l, LANES), const2),
            pl.BlockSpec((l, LANES), const2),
            pl.BlockSpec((1, LANES), const2),
            pl.BlockSpec((1, LANES), const2),
            pl.BlockSpec((1, LANES), const2),
        ],
        out_specs=pl.BlockSpec((1, l, LANES), lambda i, h: (i, 0, h)),
        out_shape=jax.ShapeDtypeStruct((b, l, A_HEADS * A_V_DIM), BF16),
        scratch_shapes=([pltpu.VMEM((lp, LANES), BF16)] * 3 + [pltpu.VMEM((LANES, lp), BF16)]
                        + [pltpu.VMEM((LANES, lp), F32)] * 4),
        compiler_params=pltpu.CompilerParams(
            dimension_semantics=("arbitrary", "arbitrary"), vmem_limit_bytes=VMEM_LIMIT),
        name="diff_attn",
    )(proj3, proj3, proj3, cos_t, sa_t, sb_t, neg_row, lam_row, sub_w)


def _log_sigmoid(x):
    return jnp.minimum(x, 0.0) - jnp.log(1.0 + jnp.exp(-jnp.abs(x)))


def _mlstm_kernel(q_ref, k_ref, v_ref, mo_ref, g_ref, gb_ref, nw_ref, o_ref,
                  c_ref, m_ref, kt_ref, h0_ref, u_ref, bcol_ref, amax_ref, aend_ref, bend_ref,
                  hf_ref, hb_ref, *, lr):
    l = lr + N_META
    t = CHUNK
    nc = lr // t
    dv = M_V_DIM
    scale = M_QK_DIM ** -0.5
    row = lax.broadcasted_iota(jnp.int32, (t, t), 0)
    col = lax.broadcasted_iota(jnp.int32, (t, t), 1)
    eye = row == col
    ones_tile = jnp.where(col == 0, 1.0, 0.0).astype(BF16)
    meta_r0 = l - t

    c_ref[...] = jnp.zeros(c_ref.shape, F32)
    m_ref[...] = jnp.zeros(m_ref.shape, F32)

    for c in range(nc + 1):
        r0 = c * t if c < nc else meta_r0
        kt_ref[:, c * t:(c + 1) * t] = k_ref[0, r0:r0 + t, :].astype(F32).T.astype(BF16)

    def scaled_q(r0):
        return (q_ref[0, pl.ds(r0, t), :].astype(F32) * scale).astype(BF16)

    def intra(ci, r0, g0):
        v_aug = jnp.concatenate([v_ref[0, pl.ds(r0, t), :], ones_tile], axis=1)
        kt = kt_ref[:, pl.ds(g0, t)]
        s_raw = _dot(scaled_q(r0), kt)
        for d, mask in ((0, col <= row), (1, col >= row)):
            gi = 2 * d
            ig = g_ref[0, 0, gi:gi + 1, pl.ds(g0, t)] + gb_ref[0, gi:gi + 1, :]
            lf = _log_sigmoid(g_ref[0, 0, gi + 1:gi + 2, pl.ds(g0, t)] + gb_ref[0, gi + 1:gi + 2, :])
            b_col = jnp.sum(jnp.where(mask, jnp.broadcast_to(lf, (t, t)), 0.0), axis=1, keepdims=True)
            b_row = jnp.sum(jnp.where(eye, jnp.broadcast_to(b_col, (t, t)), 0.0), axis=0, keepdims=True)
            a_row = ig - b_row
            dm = jnp.where(mask, jnp.broadcast_to(a_row, (t, t)), NEG_BIG)
            amax = jnp.max(dm, axis=1, keepdims=True)
            s0 = s_raw * jnp.exp(dm - amax)
            h0_ref[d, pl.ds(g0, t), :] = _dot(s0.astype(BF16), v_aug)
            a_end = jnp.max(a_row, axis=1, keepdims=True)
            kts = (kt.astype(F32) * jnp.exp(a_row - a_end)).astype(BF16)
            u_ref[d, ci] = _dot(kts, v_aug)
            bcol_ref[d, pl.ds(g0, t), :] = b_col
            amax_ref[d, pl.ds(g0, t), :] = amax
            aend_ref[d, ci] = jnp.broadcast_to(a_end, (8, LANES))
            bend_ref[d, ci] = jnp.broadcast_to(jnp.sum(lf, axis=1, keepdims=True), (8, LANES))

    ua = _pick(nc, (16, 8, 4, 2, 1))

    def abody(i, c):
        for u in range(ua):
            ci = ua * i + u
            r0 = pl.multiple_of(ci * t, t)
            intra(ci, r0, r0)
        return c

    lax.fori_loop(0, nc // ua, abody, 0)
    intra(nc, meta_r0, lr)

    def step(d, ci, r0, g0):
        m_prev = m_ref[d, 0:1, 0:1]
        amax = amax_ref[d, pl.ds(g0, t), :]
        m_col = jnp.maximum(m_prev, amax)
        num = (jnp.exp(m_prev - m_col) * _dot(scaled_q(r0), c_ref[d].astype(BF16))
               + jnp.exp(amax - m_col) * h0_ref[d, pl.ds(g0, t), :])
        den = jnp.maximum(jnp.abs(num[:, dv:dv + 1]), jnp.exp(-(bcol_ref[d, pl.ds(g0, t), :] + m_col)))
        a_end = aend_ref[d, ci, 0:1, 0:1]
        m_end = jnp.maximum(m_prev, a_end)
        c_ref[d] = jnp.exp(m_prev - m_end) * c_ref[d] + jnp.exp(a_end - m_end) * u_ref[d, ci]
        m_ref[d] = jnp.broadcast_to(bend_ref[d, ci, 0:1, 0:1] + m_end, (8, LANES))
        return num[:, :dv] * (1.0 / den)

    hf_ref[lr:l, :] = step(0, nc, meta_r0, lr)[t - N_META:, :]

    ub = _pick(nc, (2, 1))

    def bbody(i, c):
        for u in range(ub):
            jf = ub * i + u
            rf = pl.multiple_of(jf * t, t)
            hf_ref[pl.ds(rf, t), :] = step(0, jf, rf, rf)
            jb = nc - 1 - jf
            rb = pl.multiple_of(jb * t, t)
            hb_ref[pl.ds(rb, t), :] = step(1, jb, rb, rb)
        return c

    lax.fori_loop(0, nc // ub, bbody, 0)
    hb_ref[lr:l, :] = step(1, nc, meta_r0, lr)[t - N_META:, :]

    def finish(r0, rows):
        hs = hf_ref[pl.ds(r0, rows), :] + hb_ref[pl.ds(r0, rows), :]
        y = hs * lax.rsqrt(jnp.mean(hs * hs, axis=-1, keepdims=True) + EPS) * nw_ref[0]
        gate = jax.nn.sigmoid(mo_ref[0, pl.ds(r0, rows), :].astype(F32))
        o_ref[0, pl.ds(r0, rows), :] = (y * gate).astype(BF16)

    def fbody(i, c):
        for u in range(ua):
            finish(pl.multiple_of((ua * i + u) * t, t), t)
        return c

    lax.fori_loop(0, nc // ua, fbody, 0)
    finish(lr, N_META)


def _mlstm(proj3, gates_row, gate_bias, norm_w, lr):
    b, l, _ = proj3.shape
    gl = gates_row.shape[-1]
    nw = lr // CHUNK + 1
    aug = M_V_DIM + LANES
    return pl.pallas_call(
        functools.partial(_mlstm_kernel, lr=lr),
        grid=(b, M_HEADS),
        in_specs=[
            pl.BlockSpec((1, l, M_QK_DIM), lambda i, h: (i, 0, COL_MQ + h)),
            pl.BlockSpec((1, l, M_QK_DIM), lambda i, h: (i, 0, COL_MK + h)),
            pl.BlockSpec((1, l, M_V_DIM), lambda i, h: (i, 0, COL_MV + h)),
            pl.BlockSpec((1, l, M_V_DIM), lambda i, h: (i, 0, COL_MO + h)),
            pl.BlockSpec((1, 1, 8, gl), lambda i, h: (i, h, 0, 0)),
            pl.BlockSpec((1, 8, LANES), lambda i, h: (h, 0, 0)),
            pl.BlockSpec((1, 1, M_V_DIM), lambda i, h: (h, 0, 0)),
        ],
        out_specs=pl.BlockSpec((1, l, M_V_DIM), lambda i, h: (i, 0, h)),
        out_shape=jax.ShapeDtypeStruct((b, l, M_HEADS * M_V_DIM), BF16),
        scratch_shapes=[
            pltpu.VMEM((2, M_QK_DIM, aug), F32),
            pltpu.VMEM((2, 8, LANES), F32),
            pltpu.VMEM((M_QK_DIM, gl), BF16),
            pltpu.VMEM((2, gl, aug), F32),
            pltpu.VMEM((2, nw, M_QK_DIM, aug), F32),
            pltpu.VMEM((2, gl, 1), F32),
            pltpu.VMEM((2, gl, 1), F32),
            pltpu.VMEM((2, nw, 8, LANES), F32),
            pltpu.VMEM((2, nw, 8, LANES), F32),
            pltpu.VMEM((l, M_V_DIM), F32), pltpu.VMEM((l, M_V_DIM), F32),
        ],
        compiler_params=pltpu.CompilerParams(
            dimension_semantics=("arbitrary", "arbitrary"), vmem_limit_bytes=VMEM_LIMIT),
        name="mlstm",
    )(proj3, proj3, proj3, proj3, gates_row, gate_bias, norm_w)


def _merge_kernel(oa_ref, hm_ref, ga_ref, gm_ref, h_ref, wa_ref, wm_ref, wo_ref, n2_ref,
                  rw_ref, rb_ref, tri_ref,
                  h1_ref, xn_ref, te_ref, tw_ref, rk_ref, cnt_ref, carry_ref, *, tm):
    @pl.when(pl.program_id(0) == 0)
    def _():
        carry_ref[...] = jnp.zeros(carry_ref.shape, F32)

    pa = _dot(oa_ref[...], wa_ref[...])
    pm = _dot(hm_ref[...], wm_ref[...])
    merged = (jax.nn.sigmoid(ga_ref[...].astype(F32)) * pa
              + jax.nn.sigmoid(gm_ref[...].astype(F32)) * pm)
    h1 = h_ref[...] + _dot(merged.astype(BF16), wo_ref[...])
    h1_ref[...] = h1
    xn32 = h1 * lax.rsqrt(jnp.mean(h1 * h1, axis=-1, keepdims=True) + EPS) * n2_ref[...]
    xn = xn32.astype(BF16)
    xn_ref[...] = _pack_rows(xn32)

    logits = _nt(rw_ref[...], xn) + rb_ref[:, 0:1]
    eidx = lax.broadcasted_iota(jnp.int32, (N_EXPERTS, tm), 0)
    work = logits
    vals, hots = [], []
    for _ in range(TOP_K):
        mx = jnp.max(work, axis=0, keepdims=True)
        sel = jnp.min(jnp.where(work == mx, eidx, N_EXPERTS), axis=0, keepdims=True)
        hot = eidx == sel
        vals.append(mx)
        hots.append(hot)
        work = jnp.where(hot, -jnp.inf, work)
    ex = [jnp.exp(v - vals[0]) for v in vals]
    inv = 1.0 / (ex[0] + ex[1] + ex[2] + ex[3])
    chosen = jnp.where(hots[0] | hots[1] | hots[2] | hots[3], 1.0, 0.0)

    carry = carry_ref[:, 0:1]
    ranks = []
    for s in range(tm // LANES):
        sub = chosen[:, s * LANES:(s + 1) * LANES]
        ranks.append(_dot(sub.astype(BF16), tri_ref[...]) + carry)
        carry = carry + jnp.sum(sub, axis=1, keepdims=True)
    before = jnp.concatenate(ranks, axis=1) if len(ranks) > 1 else ranks[0]
    carry_ref[...] = jnp.broadcast_to(carry, carry_ref.shape)
    cnt_ref[...] = jnp.broadcast_to(carry, cnt_ref.shape)

    for kk in range(TOP_K):
        te_ref[kk:kk + 1, :] = jnp.sum(jnp.where(hots[kk], eidx, 0), axis=0, keepdims=True)
        tw_ref[kk:kk + 1, :] = ex[kk] * inv
        rk_ref[kk:kk + 1, :] = jnp.sum(jnp.where(hots[kk], before, 0.0), axis=0,
                                       keepdims=True).astype(jnp.int32)


def _merge(o_a, h_m, proj, h0, w_a, w_m, w_o, norm2_w, rw_t, rb_col, tri, row0, n):
    tm = _pick(math.gcd(n, row0) if row0 else n, (384, 256, 128))
    t0 = row0 // tm
    row = lambda i: (i, 0)
    src = lambda i: (t0 + i, 0)
    const = lambda i: (0, 0)
    tok = lambda i: (0, i)
    wspec = pl.BlockSpec((D_MODEL, D_MODEL), const)
    return pl.pallas_call(
        functools.partial(_merge_kernel, tm=tm),
        grid=(n // tm,),
        in_specs=[
            pl.BlockSpec((tm, D_MODEL), src),
            pl.BlockSpec((tm, D_MODEL), src),
            pl.BlockSpec((tm, D_MODEL), lambda i: (t0 + i, COL_GA)),
            pl.BlockSpec((tm, D_MODEL), lambda i: (t0 + i, COL_GM)),
            pl.BlockSpec((tm, D_MODEL), src),
            wspec, wspec, wspec,
            pl.BlockSpec((1, D_MODEL), const),
            pl.BlockSpec((N_EXPERTS, D_MODEL), const),
            pl.BlockSpec((N_EXPERTS, LANES), const),
            pl.BlockSpec((LANES, LANES), const),
        ],
        out_specs=[
            pl.BlockSpec((tm, D_MODEL), row),
            pl.BlockSpec((tm, ROW_WORDS), row),
            pl.BlockSpec((TOP_K, tm), tok),
            pl.BlockSpec((TOP_K, tm), tok),
            pl.BlockSpec((TOP_K, tm), tok),
            pl.BlockSpec((N_EXPERTS, LANES), const),
        ],
        out_shape=[
            jax.ShapeDtypeStruct((n, D_MODEL), F32),
            jax.ShapeDtypeStruct((n, ROW_WORDS), jnp.int32),
            jax.ShapeDtypeStruct((TOP_K, n), jnp.int32),
            jax.ShapeDtypeStruct((TOP_K, n), F32),
            jax.ShapeDtypeStruct((TOP_K, n), jnp.int32),
            jax.ShapeDtypeStruct((N_EXPERTS, LANES), F32),
        ],
        scratch_shapes=[pltpu.VMEM((N_EXPERTS, LANES), F32)],
        compiler_params=pltpu.CompilerParams(
            dimension_semantics=("arbitrary",), vmem_limit_bytes=VMEM_LIMIT),
        name="merge_router",
    )(o_a, h_m, proj, proj, h0, w_a, w_m, w_o, norm2_w, rw_t, rb_col, tri)


def _expert_kernel(be_ref, nu_ref, x_ref, wg_ref, bg_ref, wu_ref, bu_ref, wd_ref, bd_ref, y_ref,
                   wg_s, wu_s, wd_s):
    i = pl.program_id(0)
    used = i < nu_ref[0]

    @pl.when(jnp.logical_or(i == 0, be_ref[i] != be_ref[jnp.maximum(i - 1, 0)]))
    def _():
        wg_s[...] = wg_ref[0].astype(BF16)
        wu_s[...] = wu_ref[0].astype(BF16)
        wd_s[...] = wd_ref[0].astype(BF16)

    @pl.when(used)
    def _():
        x = _unpack_rows(x_ref[...]).astype(BF16)
        gt = jnp.minimum(_dot(x, wg_s[...]) + bg_ref[0], SWIGLU_LIMIT)
        up = jnp.clip(_dot(x, wu_s[...]) + bu_ref[0], -SWIGLU_LIMIT, SWIGLU_LIMIT)
        glu = gt * jax.nn.sigmoid(SWIGLU_ALPHA * gt)
        act = ((up + 1.0) * glu).astype(BF16)
        y_ref[...] = _pack_rows(_dot(act, wd_s[...]) + bd_ref[0])

    @pl.when(jnp.logical_not(used))
    def _():
        y_ref[...] = jnp.zeros(y_ref.shape, jnp.int32)


def _experts(blk_e, n_used, xs, w_g, b_g, w_u, b_u, w_d, b_d):
    cap = xs.shape[0]
    n_blocks = cap // MOE_ROWS
    wspec = lambda d0, d1: pl.BlockSpec((1, d0, d1), lambda i, be, nu: (be[i], 0, 0))
    return pl.pallas_call(
        _expert_kernel,
        grid_spec=pltpu.PrefetchScalarGridSpec(
            num_scalar_prefetch=2,
            grid=(n_blocks,),
            in_specs=[
                pl.BlockSpec((MOE_ROWS, ROW_WORDS), lambda i, be, nu: (jnp.minimum(i, nu[0] - 1), 0)),
                wspec(D_MODEL, D_FF), wspec(1, D_FF),
                wspec(D_MODEL, D_FF), wspec(1, D_FF),
                wspec(D_FF, D_MODEL), wspec(1, D_MODEL),
            ],
            out_specs=pl.BlockSpec((MOE_ROWS, ROW_WORDS), lambda i, be, nu: (i, 0)),
            scratch_shapes=[pltpu.VMEM((D_MODEL, D_FF), BF16), pltpu.VMEM((D_MODEL, D_FF), BF16),
                            pltpu.VMEM((D_FF, D_MODEL), BF16)],
        ),
        out_shape=jax.ShapeDtypeStruct((cap, ROW_WORDS), jnp.int32),
        compiler_params=pltpu.CompilerParams(
            dimension_semantics=("arbitrary",), vmem_limit_bytes=VMEM_LIMIT),
        name="experts",
    )(blk_e, n_used, xs, w_g, b_g, w_u, b_u, w_d, b_d)


def _sc_worker_windows(n_win):
    return -(-n_win // (SC_CORES * SC_SUBCORES))


def _sc_dispatch(x_words, dest3, cap):
    n = x_words.shape[0]
    n_win = n // SC_WINDOW
    per = _sc_worker_windows(n_win)

    def body(x_hbm, d_hbm, o_hbm, idx_v, rows_v):
        wid = lax.axis_index("s") * SC_CORES + lax.axis_index("c")

        @pl.loop(0, per)
        def _(i):
            win = jnp.minimum(wid * per + i, n_win - 1)
            pltpu.sync_copy(d_hbm.at[win], idx_v)
            pltpu.sync_copy(x_hbm.at[pl.ds(win * SC_WINDOW, SC_WINDOW)], rows_v)
            for kk in range(TOP_K):
                pltpu.sync_copy(rows_v, o_hbm.at[idx_v.at[kk]])

    return pl.kernel(
        body,
        out_type=jax.ShapeDtypeStruct((cap, ROW_WORDS), jnp.int32),
        mesh=plsc.VectorSubcoreMesh(core_axis_name="c", subcore_axis_name="s"),
        scratch_types=[pltpu.VMEM((TOP_K, SC_WINDOW), jnp.int32),
                       pltpu.VMEM((SC_WINDOW, ROW_WORDS), jnp.int32)],
        name="moe_dispatch",
    )(x_words, dest3)


def _sc_gather(y_words, idx2):
    n_win = idx2.shape[0]
    per = _sc_worker_windows(n_win)

    def body(y_hbm, i_hbm, o_hbm, idx_v, rows_v, sem):
        wid = lax.axis_index("s") * SC_CORES + lax.axis_index("c")

        @pl.loop(0, per)
        def _(i):
            win = jnp.minimum(wid * per + i, n_win - 1)
            pltpu.sync_copy(i_hbm.at[win], idx_v)
            pltpu.async_copy(y_hbm.at[idx_v], rows_v, sem).wait()
            pltpu.sync_copy(rows_v, o_hbm.at[pl.ds(win * SC_WINDOW, SC_WINDOW)])

    return pl.kernel(
        body,
        out_type=jax.ShapeDtypeStruct((n_win * SC_WINDOW, ROW_WORDS), jnp.int32),
        mesh=plsc.VectorSubcoreMesh(core_axis_name="c", subcore_axis_name="s"),
        scratch_types=[pltpu.VMEM((SC_WINDOW,), jnp.int32),
                       pltpu.VMEM((SC_WINDOW, ROW_WORDS), jnp.int32),
                       pltpu.SemaphoreType.DMA],
        name="moe_gather",
    )(y_words, idx2)


def _combine_kernel(h_ref, y_ref, w_ref, fw_ref, o_ref):
    acc = h_ref[0]
    for kk in range(TOP_K):
        acc = acc + w_ref[0, :, kk:kk + 1] * _unpack_rows(y_ref[kk, 0])
    o_ref[0] = acc * lax.rsqrt(jnp.mean(acc * acc, axis=-1, keepdims=True) + EPS) * fw_ref[...]


def _combine(h1, yg, wts, final_w, b0, nb, lr):
    tr = _pick(lr, (1024, 512, 256, 128))
    return pl.pallas_call(
        _combine_kernel,
        grid=(nb, lr // tr),
        in_specs=[
            pl.BlockSpec((1, tr, D_MODEL), lambda i, j: (b0 + i, j, 0)),
            pl.BlockSpec((TOP_K, 1, tr, ROW_WORDS), lambda i, j: (0, b0 + i, j, 0)),
            pl.BlockSpec((1, tr, TOP_K), lambda i, j: (b0 + i, j, 0)),
            pl.BlockSpec((1, D_MODEL), lambda i, j: (0, 0)),
        ],
        out_specs=pl.BlockSpec((1, tr, D_MODEL), lambda i, j: (i, j, 0)),
        out_shape=jax.ShapeDtypeStruct((nb, lr, D_MODEL), F32),
        compiler_params=pltpu.CompilerParams(
            dimension_semantics=("arbitrary", "arbitrary"), vmem_limit_bytes=VMEM_LIMIT),
        name="combine",
    )(h1, yg, wts, final_w)


def _rope_tables(lr):
    l = lr + N_META
    half = ROT_DIM // 2
    pos = jnp.concatenate([jnp.arange(N_META, l, dtype=F32), jnp.arange(N_META, dtype=F32)])
    inv_freq = ROPE_THETA ** (-jnp.arange(0, ROT_DIM, 2, dtype=F32) / ROT_DIM)
    ang = pos[:, None] * inv_freq[None, :]
    cos, sin = jnp.cos(ang), jnp.sin(ang)
    z = jnp.zeros((l, A_QK_DIM - ROT_DIM), F32)
    zh = jnp.zeros((l, half), F32)
    cos64 = jnp.concatenate([cos, cos, z + 1.0], axis=1)
    sa64 = jnp.concatenate([-sin, zh, z], axis=1)
    sb64 = jnp.concatenate([zh, sin, z], axis=1)
    dup = lambda a: jnp.concatenate([a, a], axis=1)
    return dup(cos64), dup(sa64), dup(sb64)


def kernel(x_prompt, x_sample, meta_tokens, norm1_w, w_in, lambda_q1, lambda_k1, lambda_q2, lambda_k2, attn_subln_w, mlstm_gate_b, mlstm_norm_w, w_br_attn, w_br_mlstm, w_out, norm2_w, router_w, router_b, w_gate, b_gate, w_up, b_up, w_down, b_down, final_norm_w):
    lr = x_prompt.shape[1]
    l = lr + N_META
    assert x_sample.shape[1:] == x_prompt.shape[1:] == (lr, D_MODEL)
    assert lr % (2 * LANES) == 0, "attention pipelines query blocks in pairs; mLSTM chunks are 128 rows"

    w_in0 = w_in[0]
    g0, g1 = OFF_GATES, OFF_GATES + 4 * M_HEADS
    w_main = jnp.concatenate([w_in0[:, :g0], w_in0[:, g1:]], axis=1).astype(BF16)
    w_gates = jnp.pad(w_in0[:, g0:g1], ((0, 0), (0, LANES - 4 * M_HEADS))).astype(BF16)
    lam = (jnp.exp(jnp.sum(lambda_q1[0] * lambda_k1[0])) - jnp.exp(jnp.sum(lambda_q2[0] * lambda_k2[0]))
           + LAM_INIT)
    lam_row = jnp.full((1, LANES), lam, F32)
    sub_w = (attn_subln_w[0] * (1.0 - LAM_INIT)).reshape(1, A_V_DIM)
    neg_row = jnp.where(jnp.arange(LANES) < N_META, 0.0, NEG_BIG).astype(F32).reshape(1, LANES)
    cos_t, sa_t, sb_t = _rope_tables(lr)
    gate_bias = jnp.broadcast_to(
        jnp.pad(mlstm_gate_b[0].T, ((0, 0), (0, 4)))[:, :, None], (M_HEADS, 8, LANES)).astype(F32)
    norm_w_m = mlstm_norm_w[0].reshape(M_HEADS, 1, M_V_DIM)
    rw_t = router_w[0].T.astype(BF16)
    rb_col = jnp.broadcast_to(router_b[0][:, None], (N_EXPERTS, LANES)).astype(F32)
    tri = (jnp.arange(LANES)[:, None] < jnp.arange(LANES)[None, :]).astype(BF16)

    w_a, w_m, w_o = w_br_attn[0].astype(BF16), w_br_mlstm[0].astype(BF16), w_out[0].astype(BF16)
    n1, n2, fw = norm1_w[0].reshape(1, D_MODEL), norm2_w[0].reshape(1, D_MODEL), final_norm_w.reshape(1, D_MODEL)
    bg3, bu3, bd3 = (b_gate[0].reshape(N_EXPERTS, 1, D_FF), b_up[0].reshape(N_EXPERTS, 1, D_FF),
                     b_down[0].reshape(N_EXPERTS, 1, D_MODEL))

    b_all = x_prompt.shape[0] + x_sample.shape[0]
    n_all = b_all * l
    x_all = jnp.concatenate([x_prompt, x_sample], axis=0)
    meta_b = jnp.broadcast_to(meta_tokens[None].astype(x_all.dtype), (b_all, N_META, D_MODEL))
    h0 = jnp.concatenate([x_all, meta_b], axis=1).reshape(n_all, D_MODEL)

    proj, gates = _inproj(h0, n1, w_main, w_gates)
    proj3 = proj.reshape(b_all, l, N_MAIN)
    o_a = _attention(proj3, cos_t, sa_t, sb_t, neg_row, lam_row, sub_w, lr).reshape(n_all, D_MODEL)

    g4 = gates.reshape(b_all, l, 4, M_HEADS).transpose(0, 3, 2, 1)
    off = jnp.array([NEG_BIG, -NEG_BIG, NEG_BIG, -NEG_BIG], F32)[None, None, :, None]
    fill = jnp.broadcast_to(off, (b_all, M_HEADS, 4, CHUNK - N_META))
    g_row = jnp.concatenate([g4[..., :lr], fill, g4[..., lr:]], axis=-1)
    g_row = jnp.pad(g_row, ((0, 0), (0, 0), (0, 4), (0, 0)))
    h_m = _mlstm(proj3, g_row, gate_bias, norm_w_m, lr).reshape(n_all, D_MODEL)

    def channel_mix(b0, b):
        n = b * l
        assert n % SC_WINDOW == 0, "SparseCore dispatch works on whole 128-token windows"
        h1, xn2, top_e, top_w, rank, cnt = _merge(
            o_a, h_m, proj, h0, w_a, w_m, w_o, n2, rw_t, rb_col, tri, b0 * l, n)

        counts = cnt[:, 0].astype(jnp.int32)
        padded = (counts + MOE_ROWS - 1) // MOE_ROWS * MOE_ROWS
        pends = jnp.cumsum(padded)
        pstarts = pends - padded
        e_ids = jnp.arange(N_EXPERTS, dtype=jnp.int32)
        pstart_of = jnp.sum(jnp.where(top_e[..., None] == e_ids, pstarts, 0), axis=-1)
        dest = pstart_of + rank
        n_blocks = (n * TOP_K + N_EXPERTS * (MOE_ROWS - 1) + MOE_ROWS - 1) // MOE_ROWS
        cap = n_blocks * MOE_ROWS
        blk_start = jnp.arange(n_blocks, dtype=jnp.int32) * MOE_ROWS
        blk_e = jnp.minimum(jnp.sum((pends[None, :] <= blk_start[:, None]).astype(jnp.int32), axis=1),
                            N_EXPERTS - 1)
        n_used = (pends[-1:] // MOE_ROWS).astype(jnp.int32)

        dest_win = dest.reshape(TOP_K, n // SC_WINDOW, SC_WINDOW).transpose(1, 0, 2)
        xs = _sc_dispatch(xn2, dest_win, cap)
        y = _experts(blk_e, n_used, xs, w_gate[0], bg3, w_up[0], bu3, w_down[0], bd3)
        yg = _sc_gather(y, dest.reshape(TOP_K * n // SC_WINDOW, SC_WINDOW)).reshape(TOP_K, b, l, ROW_WORDS)
        return _combine(h1.reshape(b, l, D_MODEL), yg, top_w.T.reshape(b, l, TOP_K), fw, 0, b, lr)

    return (channel_mix(0, x_prompt.shape[0]), channel_mix(x_prompt.shape[0], x_sample.shape[0]))
```

```python
import functools
import math

import jax
import jax.numpy as jnp
from jax import lax
from jax.experimental import pallas as pl
from jax.experimental.pallas import tpu as pltpu
from jax.experimental.pallas import tpu_sc as plsc

F32 = jnp.float32
BF16 = jnp.bfloat16

D_MODEL = 1024
N_META = 16
EPS = 1e-6
A_HEADS = 8
A_QK_DIM = 64
A_V_DIM = 128
ROT_DIM = 16
ROPE_THETA = 500000.0
M_HEADS = 4
M_QK_DIM = 128
M_V_DIM = 256
CHUNK = 128
N_EXPERTS = 32
TOP_K = 4
D_FF = 1024
SWIGLU_LIMIT = 7.0
SWIGLU_ALPHA = 1.702
LAM_INIT = 0.8 - 0.6 * math.exp(-0.3 * 0)

LANES = 128
NEG_BIG = -1e30
MOE_ROWS = 512
SC_CORES = 2
SC_SUBCORES = 16
SC_WINDOW = 128
ROW_WORDS = D_MODEL // 2
W_AQK = A_HEADS * 2 * A_QK_DIM
W_AV = A_HEADS * A_V_DIM
W_MQK = M_HEADS * M_QK_DIM
W_MV = M_HEADS * M_V_DIM
OFF_GATES = 2 * W_AQK + W_AV + 2 * W_MQK + 2 * W_MV
N_MAIN = OFF_GATES + 2 * D_MODEL
COL_AQ, COL_AK, COL_AV = 0, W_AQK // LANES, 2 * W_AQK // LANES
COL_MQ = (2 * W_AQK + W_AV) // M_QK_DIM
COL_MK = COL_MQ + M_HEADS
COL_MV = (2 * W_AQK + W_AV + 2 * W_MQK) // M_V_DIM
COL_MO = COL_MV + M_HEADS
COL_GA = OFF_GATES // D_MODEL
COL_GM = COL_GA + 1

VMEM_LIMIT = 56 * 1024 * 1024


def _pick(n, prefs):
    for p in prefs:
        if n % p == 0:
            return p
    raise ValueError(f"no tile for {n}")


def _nt(a, b):
    return lax.dot_general(a, b, (((1,), (1,)), ((), ())), preferred_element_type=F32)


def _dot(a, b):
    return jnp.dot(a, b, preferred_element_type=F32)


def _pack_rows(x):
    w = x.shape[1] // 2
    bits = lambda v: lax.bitcast_convert_type(v.astype(BF16).astype(F32), jnp.uint32)
    lo = lax.shift_right_logical(bits(x[:, :w]), jnp.uint32(16))
    hi = bits(x[:, w:]) & jnp.uint32(0xFFFF0000)
    return lax.bitcast_convert_type(lo | hi, jnp.int32)


def _unpack_rows(wds):
    u = lax.bitcast_convert_type(wds, jnp.uint32)
    lo = lax.bitcast_convert_type(lax.shift_left(u, jnp.uint32(16)), F32)
    hi = lax.bitcast_convert_type(u & jnp.uint32(0xFFFF0000), F32)
    return jnp.concatenate([lo, hi], axis=1)


def _inproj_kernel(x_ref, n1_ref, w_ref, wg_ref, o_ref, g_ref, xn_ref):
    @pl.when(pl.program_id(1) == 0)
    def _():
        x = x_ref[...]
        ms = jnp.mean(x * x, axis=-1, keepdims=True)
        xn = (x * lax.rsqrt(ms + EPS) * n1_ref[...]).astype(BF16)
        xn_ref[...] = xn
        g_ref[...] = _dot(xn, wg_ref[...])[:, :4 * M_HEADS]

    o_ref[...] = _dot(xn_ref[...], w_ref[...]).astype(BF16)


def _inproj(h0, norm1_w, w_main, w_gates):
    n = h0.shape[0]
    tm = _pick(n, (1152, 768, 512, 384, 256, 128))
    tn = 2048
    return pl.pallas_call(
        _inproj_kernel,
        grid=(n // tm, N_MAIN // tn),
        in_specs=[
            pl.BlockSpec((tm, D_MODEL), lambda i, j: (i, 0)),
            pl.BlockSpec((1, D_MODEL), lambda i, j: (0, 0)),
            pl.BlockSpec((D_MODEL, tn), lambda i, j: (0, j)),
            pl.BlockSpec((D_MODEL, LANES), lambda i, j: (0, 0)),
        ],
        out_specs=[
            pl.BlockSpec((tm, tn), lambda i, j: (i, j)),
            pl.BlockSpec((tm, 4 * M_HEADS), lambda i, j: (i, 0)),
        ],
        out_shape=[
            jax.ShapeDtypeStruct((n, N_MAIN), BF16),
            jax.ShapeDtypeStruct((n, 4 * M_HEADS), F32),
        ],
        scratch_shapes=[pltpu.VMEM((tm, D_MODEL), BF16)],
        compiler_params=pltpu.CompilerParams(
            dimension_semantics=("arbitrary", "arbitrary"), vmem_limit_bytes=VMEM_LIMIT,
            allow_input_fusion=[True, False, False, False]),
        name="inproj",
    )(h0, norm1_w, w_main, w_gates)


def _attn_kernel(q_ref, k_ref, v_ref, cos_ref, sa_ref, sb_ref, neg_ref, lam_ref, sw_ref,
                 o_ref, q1_ref, q2_ref, vs_ref, kt_ref, sa1_ref, sa2_ref, sb1_ref, sb2_ref, *, lr):
    l = lr + N_META
    lp = lr + LANES
    nblk = lr // LANES
    scale = A_QK_DIM ** -0.5 * math.log2(math.e)

    def rope(t):
        return (t * cos_ref[...] + pltpu.roll(t, LANES - ROT_DIM // 2, 1) * sa_ref[...]
                + pltpu.roll(t, ROT_DIM // 2, 1) * sb_ref[...])

    lane = lax.broadcasted_iota(jnp.int32, (l, LANES), 1)
    qr = rope(q_ref[0].astype(F32)) * scale
    q1_ref[0:l, :] = jnp.where(lane < A_QK_DIM, qr, 0.0).astype(BF16)
    q2_ref[0:l, :] = jnp.where(lane >= A_QK_DIM, qr, 0.0).astype(BF16)
    vs_ref[0:l, :] = v_ref[0]
    pad = jnp.zeros((lp - l, LANES), BF16)
    q1_ref[l:lp, :] = pad
    q2_ref[l:lp, :] = pad
    vs_ref[l:lp, :] = pad
    kr = rope(k_ref[0].astype(F32))
    for c in range(nblk):
        kt_ref[:, c * LANES:(c + 1) * LANES] = kr[c * LANES:(c + 1) * LANES, :].T.astype(BF16)
    k_tail = jnp.concatenate([kr[lr:l, :], jnp.zeros((lp - l, LANES), F32)], axis=0)
    kt_ref[:, lr:lp] = k_tail.T.astype(BF16)

    lam = lam_ref[:, 0:1]

    def scores(r0, s1_ref, s2_ref):
        rows = pl.ds(r0, LANES)
        k_all = kt_ref[...]
        s1_ref[...] = _dot(q1_ref[rows, :], k_all)
        s2_ref[...] = _dot(q2_ref[rows, :], k_all)

    def softmax_parts(s_ref, nrows):
        sm = s_ref[0:nrows, :lr]
        sl = s_ref[0:nrows, lr:] + neg_ref[...]
        m = jnp.maximum(jnp.max(sm, axis=1, keepdims=True), jnp.max(sl, axis=1, keepdims=True))
        pm = jnp.exp2(sm - m)
        pl_ = jnp.exp2(sl - m)
        tot = jnp.sum(pm, axis=1, keepdims=True) + jnp.sum(pl_, axis=1, keepdims=True)
        return pm, pl_, tot

    def finish(r0, nrows, s1_ref, s2_ref):
        p1m, p1l, t1 = softmax_parts(s1_ref, nrows)
        p2m, p2l, t2 = softmax_parts(s2_ref, nrows)
        c = lam * t1 / t2
        am = (p1m - p2m * c).astype(BF16)
        al = (p1l - p2l * c).astype(BF16)
        o = (_dot(am, vs_ref[0:lr, :]) + _dot(al, vs_ref[lr:lp, :])) * (1.0 / t1)
        o = o * lax.rsqrt(jnp.mean(o * o, axis=-1, keepdims=True) + EPS) * sw_ref[...]
        o_ref[0, pl.ds(r0, nrows), :] = o.astype(BF16)

    scores(0, sa1_ref, sa2_ref)

    pairs = _pick(nblk // 2, (8, 4, 2, 1))

    def body(j, c):
        for u in range(pairs):
            ra = pl.multiple_of((j * pairs + u) * (2 * LANES), LANES)
            rb = ra + LANES
            scores(rb, sb1_ref, sb2_ref)
            finish(ra, LANES, sa1_ref, sa2_ref)
            scores(rb + LANES, sa1_ref, sa2_ref)
            finish(rb, LANES, sb1_ref, sb2_ref)
        return c

    lax.fori_loop(0, nblk // (2 * pairs), body, 0)
    finish(lr, N_META, sa1_ref, sa2_ref)


def _attention(proj3, cos_t, sa_t, sb_t, neg_row, lam_row, sub_w, lr):
    b, l, _ = proj3.shape
    lp = lr + LANES
    const2 = lambda i, h: (0, 0)
    return pl.pallas_call(
        functools.partial(_attn_kernel, lr=lr),
        grid=(b, A_HEADS),
        in_specs=[
            pl.BlockSpec((1, l, LANES), lambda i, h: (i, 0, COL_AQ + h)),
            pl.BlockSpec((1, l, LANES), lambda i, h: (i, 0, COL_AK + h)),
            pl.BlockSpec((1, l, LANES), lambda i, h: (i, 0, COL_AV + h)),
            pl.BlockSpec((l, LANES), const2),
            pl.BlockSpec((l, LANES), const2),
            pl.BlockSpec((l, LANES), const2),
            pl.BlockSpec((1, LANES), const2),
            pl.BlockSpec((1, LANES), const2),
            pl.BlockSpec((1, LANES), const2),
        ],
        out_specs=pl.BlockSpec((1, l, LANES), lambda i, h: (i, 0, h)),
        out_shape=jax.ShapeDtypeStruct((b, l, A_HEADS * A_V_DIM), BF16),
        scratch_shapes=([pltpu.VMEM((lp, LANES), BF16)] * 3 + [pltpu.VMEM((LANES, lp), BF16)]
                        + [pltpu.VMEM((LANES, lp), F32)] * 4),
        compiler_params=pltpu.CompilerParams(
            dimension_semantics=("arbitrary", "arbitrary"), vmem_limit_bytes=VMEM_LIMIT),
        name="diff_attn",
    )(proj3, proj3, proj3, cos_t, sa_t, sb_t, neg_row, lam_row, sub_w)


def _log_sigmoid(x):
    return jnp.minimum(x, 0.0) - jnp.log(1.0 + jnp.exp(-jnp.abs(x)))


def _mlstm_kernel(q_ref, k_ref, v_ref, mo_ref, g_ref, gb_ref, nw_ref, o_ref,
                  c_ref, m_ref, kt_ref, h0_ref, u_ref, bcol_ref, amax_ref, aend_ref, bend_ref,
                  hf_ref, hb_ref, *, lr):
    l = lr + N_META
    t = CHUNK
    nc = lr // t
    dv = M_V_DIM
    scale = M_QK_DIM ** -0.5
    row = lax.broadcasted_iota(jnp.int32, (t, t), 0)
    col = lax.broadcasted_iota(jnp.int32, (t, t), 1)
    eye = row == col
    ones_tile = jnp.where(col == 0, 1.0, 0.0).astype(BF16)
    meta_r0 = l - t

    c_ref[...] = jnp.zeros(c_ref.shape, F32)
    m_ref[...] = jnp.zeros(m_ref.shape, F32)

    for c in range(nc + 1):
        r0 = c * t if c < nc else meta_r0
        kt_ref[:, c * t:(c + 1) * t] = k_ref[0, r0:r0 + t, :].astype(F32).T.astype(BF16)

    def scaled_q(r0):
        return (q_ref[0, pl.ds(r0, t), :].astype(F32) * scale).astype(BF16)

    def intra(ci, r0, g0):
        v_aug = jnp.concatenate([v_ref[0, pl.ds(r0, t), :], ones_tile], axis=1)
        kt = kt_ref[:, pl.ds(g0, t)]
        s_raw = _dot(scaled_q(r0), kt)
        for d, mask in ((0, col <= row), (1, col >= row)):
            gi = 2 * d
            ig = g_ref[0, 0, gi:gi + 1, pl.ds(g0, t)] + gb_ref[0, gi:gi + 1, :]
            lf = _log_sigmoid(g_ref[0, 0, gi + 1:gi + 2, pl.ds(g0, t)] + gb_ref[0, gi + 1:gi + 2, :])
            b_col = jnp.sum(jnp.where(mask, jnp.broadcast_to(lf, (t, t)), 0.0), axis=1, keepdims=True)
            b_row = jnp.sum(jnp.where(eye, jnp.broadcast_to(b_col, (t, t)), 0.0), axis=0, keepdims=True)
            a_row = ig - b_row
            dm = jnp.where(mask, jnp.broadcast_to(a_row, (t, t)), NEG_BIG)
            amax = jnp.max(dm, axis=1, keepdims=True)
            s0 = s_raw * jnp.exp(dm - amax)
            h0_ref[d, pl.ds(g0, t), :] = _dot(s0.astype(BF16), v_aug)
            a_end = jnp.max(a_row, axis=1, keepdims=True)
            kts = (kt.astype(F32) * jnp.exp(a_row - a_end)).astype(BF16)
            u_ref[d, ci] = _dot(kts, v_aug)
            bcol_ref[d, pl.ds(g0, t), :] = b_col
            amax_ref[d, pl.ds(g0, t), :] = amax
            aend_ref[d, ci] = jnp.broadcast_to(a_end, (8, LANES))
            bend_ref[d, ci] = jnp.broadcast_to(jnp.sum(lf, axis=1, keepdims=True), (8, LANES))

    ua = _pick(nc, (16, 8, 4, 2, 1))

    def abody(i, c):
        for u in range(ua):
            ci = ua * i + u
            r0 = pl.multiple_of(ci * t, t)
            intra(ci, r0, r0)
        return c

    lax.fori_loop(0, nc // ua, abody, 0)
    intra(nc, meta_r0, lr)

    def step(d, ci, r0, g0):
        m_prev = m_ref[d, 0:1, 0:1]
        amax = amax_ref[d, pl.ds(g0, t), :]
        m_col = jnp.maximum(m_prev, amax)
        num = (jnp.exp(m_prev - m_col) * _dot(scaled_q(r0), c_ref[d].astype(BF16))
               + jnp.exp(amax - m_col) * h0_ref[d, pl.ds(g0, t), :])
        den = jnp.maximum(jnp.abs(num[:, dv:dv + 1]), jnp.exp(-(bcol_ref[d, pl.ds(g0, t), :] + m_col)))
        a_end = aend_ref[d, ci, 0:1, 0:1]
        m_end = jnp.maximum(m_prev, a_end)
        c_ref[d] = jnp.exp(m_prev - m_end) * c_ref[d] + jnp.exp(a_end - m_end) * u_ref[d, ci]
        m_ref[d] = jnp.broadcast_to(bend_ref[d, ci, 0:1, 0:1] + m_end, (8, LANES))
        return num[:, :dv] * (1.0 / den)

    hf_ref[lr:l, :] = step(0, nc, meta_r0, lr)[t - N_META:, :]

    ub = _pick(nc, (2, 1))

    def bbody(i, c):
        for u in range(ub):
            jf = ub * i + u
            rf = pl.multiple_of(jf * t, t)
            hf_ref[pl.ds(rf, t), :] = step(0, jf, rf, rf)
            jb = nc - 1 - jf
            rb = pl.multiple_of(jb * t, t)
            hb_ref[pl.ds(rb, t), :] = step(1, jb, rb, rb)
        return c

    lax.fori_loop(0, nc // ub, bbody, 0)
    hb_ref[lr:l, :] = step(1, nc, meta_r0, lr)[t - N_META:, :]

    def finish(r0, rows):
        hs = hf_ref[pl.ds(r0, rows), :] + hb_ref[pl.ds(r0, rows), :]
        y = hs * lax.rsqrt(jnp.mean(hs * hs, axis=-1, keepdims=True) + EPS) * nw_ref[0]
        gate = jax.nn.sigmoid(mo_ref[0, pl.ds(r0, rows), :].astype(F32))
        o_ref[0, pl.ds(r0, rows), :] = (y * gate).astype(BF16)

    def fbody(i, c):
        for u in range(ua):
            finish(pl.multiple_of((ua * i + u) * t, t), t)
        return c

    lax.fori_loop(0, nc // ua, fbody, 0)
    finish(lr, N_META)


def _mlstm(proj3, gates_row, gate_bias, norm_w, lr):
    b, l, _ = proj3.shape
    gl = gates_row.shape[-1]
    nw = lr // CHUNK + 1
    aug = M_V_DIM + LANES
    return pl.pallas_call(
        functools.partial(_mlstm_kernel, lr=lr),
        grid=(b, M_HEADS),
        in_specs=[
            pl.BlockSpec((1, l, M_QK_DIM), lambda i, h: (i, 0, COL_MQ + h)),
            pl.BlockSpec((1, l, M_QK_DIM), lambda i, h: (i, 0, COL_MK + h)),
            pl.BlockSpec((1, l, M_V_DIM), lambda i, h: (i, 0, COL_MV + h)),
            pl.BlockSpec((1, l, M_V_DIM), lambda i, h: (i, 0, COL_MO + h)),
            pl.BlockSpec((1, 1, 8, gl), lambda i, h: (i, h, 0, 0)),
            pl.BlockSpec((1, 8, LANES), lambda i, h: (h, 0, 0)),
            pl.BlockSpec((1, 1, M_V_DIM), lambda i, h: (h, 0, 0)),
        ],
        out_specs=pl.BlockSpec((1, l, M_V_DIM), lambda i, h: (i, 0, h)),
        out_shape=jax.ShapeDtypeStruct((b, l, M_HEADS * M_V_DIM), BF16),
        scratch_shapes=[
            pltpu.VMEM((2, M_QK_DIM, aug), F32),
            pltpu.VMEM((2, 8, LANES), F32),
            pltpu.VMEM((M_QK_DIM, gl), BF16),
            pltpu.VMEM((2, gl, aug), F32),
            pltpu.VMEM((2, nw, M_QK_DIM, aug), F32),
            pltpu.VMEM((2, gl, 1), F32),
            pltpu.VMEM((2, gl, 1), F32),
            pltpu.VMEM((2, nw, 8, LANES), F32),
            pltpu.VMEM((2, nw, 8, LANES), F32),
            pltpu.VMEM((l, M_V_DIM), F32), pltpu.VMEM((l, M_V_DIM), F32),
        ],
        compiler_params=pltpu.CompilerParams(
            dimension_semantics=("arbitrary", "arbitrary"), vmem_limit_bytes=VMEM_LIMIT),
        name="mlstm",
    )(proj3, proj3, proj3, proj3, gates_row, gate_bias, norm_w)


def _merge_kernel(oa_ref, hm_ref, ga_ref, gm_ref, h_ref, wa_ref, wm_ref, wo_ref, n2_ref,
                  rw_ref, rb_ref, tri_ref,
                  h1_ref, xn_ref, te_ref, tw_ref, rk_ref, cnt_ref, carry_ref, *, tm):
    @pl.when(pl.program_id(0) == 0)
    def _():
        carry_ref[...] = jnp.zeros(carry_ref.shape, F32)

    pa = _dot(oa_ref[...], wa_ref[...])
    pm = _dot(hm_ref[...], wm_ref[...])
    merged = (jax.nn.sigmoid(ga_ref[...].astype(F32)) * pa
              + jax.nn.sigmoid(gm_ref[...].astype(F32)) * pm)
    h1 = h_ref[...] + _dot(merged.astype(BF16), wo_ref[...])
    h1_ref[...] = h1
    xn32 = h1 * lax.rsqrt(jnp.mean(h1 * h1, axis=-1, keepdims=True) + EPS) * n2_ref[...]
    xn = xn32.astype(BF16)
    xn_ref[...] = _pack_rows(xn32)

    logits = _nt(rw_ref[...], xn) + rb_ref[:, 0:1]
    eidx = lax.broadcasted_iota(jnp.int32, (N_EXPERTS, tm), 0)
    work = logits
    vals, hots = [], []
    for _ in range(TOP_K):
        mx = jnp.max(work, axis=0, keepdims=True)
        sel = jnp.min(jnp.where(work == mx, eidx, N_EXPERTS), axis=0, keepdims=True)
        hot = eidx == sel
        vals.append(mx)
        hots.append(hot)
        work = jnp.where(hot, -jnp.inf, work)
    ex = [jnp.exp(v - vals[0]) for v in vals]
    inv = 1.0 / (ex[0] + ex[1] + ex[2] + ex[3])
    chosen = jnp.where(hots[0] | hots[1] | hots[2] | hots[3], 1.0, 0.0)

    carry = carry_ref[:, 0:1]
    ranks = []
    for s in range(tm // LANES):
        sub = chosen[:, s * LANES:(s + 1) * LANES]
        ranks.append(_dot(sub.astype(BF16), tri_ref[...]) + carry)
        carry = carry + jnp.sum(sub, axis=1, keepdims=True)
    before = jnp.concatenate(ranks, axis=1) if len(ranks) > 1 else ranks[0]
    carry_ref[...] = jnp.broadcast_to(carry, carry_ref.shape)
    cnt_ref[...] = jnp.broadcast_to(carry, cnt_ref.shape)

    for kk in range(TOP_K):
        te_ref[kk:kk + 1, :] = jnp.sum(jnp.where(hots[kk], eidx, 0), axis=0, keepdims=True)
        tw_ref[kk:kk + 1, :] = ex[kk] * inv
        rk_ref[kk:kk + 1, :] = jnp.sum(jnp.where(hots[kk], before, 0.0), axis=0,
                                       keepdims=True).astype(jnp.int32)


def _merge(o_a, h_m, proj, h0, w_a, w_m, w_o, norm2_w, rw_t, rb_col, tri, row0, n):
    tm = _pick(math.gcd(n, row0) if row0 else n, (384, 256, 128))
    t0 = row0 // tm
    row = lambda i: (i, 0)
    src = lambda i: (t0 + i, 0)
    const = lambda i: (0, 0)
    tok = lambda i: (0, i)
    wspec = pl.BlockSpec((D_MODEL, D_MODEL), const)
    return pl.pallas_call(
        functools.partial(_merge_kernel, tm=tm),
        grid=(n // tm,),
        in_specs=[
            pl.BlockSpec((tm, D_MODEL), src),
            pl.BlockSpec((tm, D_MODEL), src),
            pl.BlockSpec((tm, D_MODEL), lambda i: (t0 + i, COL_GA)),
            pl.BlockSpec((tm, D_MODEL), lambda i: (t0 + i, COL_GM)),
            pl.BlockSpec((tm, D_MODEL), src),
            wspec, wspec, wspec,
            pl.BlockSpec((1, D_MODEL), const),
            pl.BlockSpec((N_EXPERTS, D_MODEL), const),
            pl.BlockSpec((N_EXPERTS, LANES), const),
            pl.BlockSpec((LANES, LANES), const),
        ],
        out_specs=[
            pl.BlockSpec((tm, D_MODEL), row),
            pl.BlockSpec((tm, ROW_WORDS), row),
            pl.BlockSpec((TOP_K, tm), tok),
            pl.BlockSpec((TOP_K, tm), tok),
            pl.BlockSpec((TOP_K, tm), tok),
            pl.BlockSpec((N_EXPERTS, LANES), const),
        ],
        out_shape=[
            jax.ShapeDtypeStruct((n, D_MODEL), F32),
            jax.ShapeDtypeStruct((n, ROW_WORDS), jnp.int32),
            jax.ShapeDtypeStruct((TOP_K, n), jnp.int32),
            jax.ShapeDtypeStruct((TOP_K, n), F32),
            jax.ShapeDtypeStruct((TOP_K, n), jnp.int32),
            jax.ShapeDtypeStruct((N_EXPERTS, LANES), F32),
        ],
        scratch_shapes=[pltpu.VMEM((N_EXPERTS, LANES), F32)],
        compiler_params=pltpu.CompilerParams(
            dimension_semantics=("arbitrary",), vmem_limit_bytes=VMEM_LIMIT,
            allow_input_fusion=[i == 4 for i in range(12)]),
        name="merge_router",
    )(o_a, h_m, proj, proj, h0, w_a, w_m, w_o, norm2_w, rw_t, rb_col, tri)


def _expert_kernel(be_ref, nu_ref, x_ref, wg_ref, bg_ref, wu_ref, bu_ref, wd_ref, bd_ref, y_ref,
                   wg_s, wu_s, wd_s):
    i = pl.program_id(0)
    used = i < nu_ref[0]

    @pl.when(jnp.logical_or(i == 0, be_ref[i] != be_ref[jnp.maximum(i - 1, 0)]))
    def _():
        wg_s[...] = wg_ref[0].astype(BF16)
        wu_s[...] = wu_ref[0].astype(BF16)
        wd_s[...] = wd_ref[0].astype(BF16)

    @pl.when(used)
    def _():
        x = _unpack_rows(x_ref[...]).astype(BF16)
        gt = jnp.minimum(_dot(x, wg_s[...]) + bg_ref[0], SWIGLU_LIMIT)
        up = jnp.clip(_dot(x, wu_s[...]) + bu_ref[0], -SWIGLU_LIMIT, SWIGLU_LIMIT)
        glu = gt * jax.nn.sigmoid(SWIGLU_ALPHA * gt)
        act = ((up + 1.0) * glu).astype(BF16)
        y_ref[...] = _pack_rows(_dot(act, wd_s[...]) + bd_ref[0])

    @pl.when(jnp.logical_not(used))
    def _():
        y_ref[...] = jnp.zeros(y_ref.shape, jnp.int32)


def _experts(blk_e, n_used, xs, w_g, b_g, w_u, b_u, w_d, b_d):
    cap = xs.shape[0]
    n_blocks = cap // MOE_ROWS
    wspec = lambda d0, d1: pl.BlockSpec((1, d0, d1), lambda i, be, nu: (be[i], 0, 0))
    return pl.pallas_call(
        _expert_kernel,
        grid_spec=pltpu.PrefetchScalarGridSpec(
            num_scalar_prefetch=2,
            grid=(n_blocks,),
            in_specs=[
                pl.BlockSpec((MOE_ROWS, ROW_WORDS), lambda i, be, nu: (i, 0)),
                wspec(D_MODEL, D_FF), wspec(1, D_FF),
                wspec(D_MODEL, D_FF), wspec(1, D_FF),
                wspec(D_FF, D_MODEL), wspec(1, D_MODEL),
            ],
            out_specs=pl.BlockSpec((MOE_ROWS, ROW_WORDS), lambda i, be, nu: (i, 0)),
            scratch_shapes=[pltpu.VMEM((D_MODEL, D_FF), BF16), pltpu.VMEM((D_MODEL, D_FF), BF16),
                            pltpu.VMEM((D_FF, D_MODEL), BF16)],
        ),
        out_shape=jax.ShapeDtypeStruct((cap, ROW_WORDS), jnp.int32),
        compiler_params=pltpu.CompilerParams(
            dimension_semantics=("arbitrary",), vmem_limit_bytes=VMEM_LIMIT),
        name="experts",
    )(blk_e, n_used, xs, w_g, b_g, w_u, b_u, w_d, b_d)


def _sc_worker_windows(n_win):
    return -(-n_win // (SC_CORES * SC_SUBCORES))


def _sc_dispatch(x_words, dest3, cap):
    n = x_words.shape[0]
    n_win = n // SC_WINDOW
    per = _sc_worker_windows(n_win)

    def body(x_hbm, d_hbm, o_hbm, idx_v, rows_v):
        wid = lax.axis_index("s") * SC_CORES + lax.axis_index("c")

        @pl.loop(0, per)
        def _(i):
            win = jnp.minimum(wid * per + i, n_win - 1)
            pltpu.sync_copy(d_hbm.at[win], idx_v)
            pltpu.sync_copy(x_hbm.at[pl.ds(win * SC_WINDOW, SC_WINDOW)], rows_v)
            for kk in range(TOP_K):
                pltpu.sync_copy(rows_v, o_hbm.at[idx_v.at[kk]])

    return pl.kernel(
        body,
        out_type=jax.ShapeDtypeStruct((cap, ROW_WORDS), jnp.int32),
        mesh=plsc.VectorSubcoreMesh(core_axis_name="c", subcore_axis_name="s"),
        scratch_types=[pltpu.VMEM((TOP_K, SC_WINDOW), jnp.int32),
                       pltpu.VMEM((SC_WINDOW, ROW_WORDS), jnp.int32)],
        name="moe_dispatch",
    )(x_words, dest3)


def _sc_gather(y_words, idx2):
    n_win = idx2.shape[0]
    per = _sc_worker_windows(n_win)

    def body(y_hbm, i_hbm, o_hbm, idx_v, rows_v, sem):
        wid = lax.axis_index("s") * SC_CORES + lax.axis_index("c")

        @pl.loop(0, per)
        def _(i):
            win = jnp.minimum(wid * per + i, n_win - 1)
            pltpu.sync_copy(i_hbm.at[win], idx_v)
            pltpu.async_copy(y_hbm.at[idx_v], rows_v, sem).wait()
            pltpu.sync_copy(rows_v, o_hbm.at[pl.ds(win * SC_WINDOW, SC_WINDOW)])

    return pl.kernel(
        body,
        out_type=jax.ShapeDtypeStruct((n_win * SC_WINDOW, ROW_WORDS), jnp.int32),
        mesh=plsc.VectorSubcoreMesh(core_axis_name="c", subcore_axis_name="s"),
        scratch_types=[pltpu.VMEM((SC_WINDOW,), jnp.int32),
                       pltpu.VMEM((SC_WINDOW, ROW_WORDS), jnp.int32),
                       pltpu.SemaphoreType.DMA],
        name="moe_gather",
    )(y_words, idx2)


def _combine_kernel(h_ref, y_ref, w_ref, fw_ref, o_ref):
    acc = h_ref[0]
    for kk in range(TOP_K):
        acc = acc + w_ref[0, :, kk:kk + 1] * _unpack_rows(y_ref[kk, 0])
    o_ref[0] = acc * lax.rsqrt(jnp.mean(acc * acc, axis=-1, keepdims=True) + EPS) * fw_ref[...]


def _combine(h1, yg, wts, final_w, b0, nb, lr):
    tr = _pick(lr, (1024, 512, 256, 128))
    return pl.pallas_call(
        _combine_kernel,
        grid=(nb, lr // tr),
        in_specs=[
            pl.BlockSpec((1, tr, D_MODEL), lambda i, j: (b0 + i, j, 0)),
            pl.BlockSpec((TOP_K, 1, tr, ROW_WORDS), lambda i, j: (0, b0 + i, j, 0)),
            pl.BlockSpec((1, tr, TOP_K), lambda i, j: (b0 + i, j, 0)),
            pl.BlockSpec((1, D_MODEL), lambda i, j: (0, 0)),
        ],
        out_specs=pl.BlockSpec((1, tr, D_MODEL), lambda i, j: (i, j, 0)),
        out_shape=jax.ShapeDtypeStruct((nb, lr, D_MODEL), F32),
        compiler_params=pltpu.CompilerParams(
            dimension_semantics=("arbitrary", "arbitrary"), vmem_limit_bytes=VMEM_LIMIT),
        name="combine",
    )(h1, yg, wts, final_w)


def _rope_tables(lr):
    l = lr + N_META
    half = ROT_DIM // 2
    pos = jnp.concatenate([jnp.arange(N_META, l, dtype=F32), jnp.arange(N_META, dtype=F32)])
    inv_freq = ROPE_THETA ** (-jnp.arange(0, ROT_DIM, 2, dtype=F32) / ROT_DIM)
    ang = pos[:, None] * inv_freq[None, :]
    cos, sin = jnp.cos(ang), jnp.sin(ang)
    z = jnp.zeros((l, A_QK_DIM - ROT_DIM), F32)
    zh = jnp.zeros((l, half), F32)
    cos64 = jnp.concatenate([cos, cos, z + 1.0], axis=1)
    sa64 = jnp.concatenate([-sin, zh, z], axis=1)
    sb64 = jnp.concatenate([zh, sin, z], axis=1)
    dup = lambda a: jnp.concatenate([a, a], axis=1)
    return dup(cos64), dup(sa64), dup(sb64)


def kernel(x_prompt, x_sample, meta_tokens, norm1_w, w_in, lambda_q1, lambda_k1, lambda_q2, lambda_k2, attn_subln_w, mlstm_gate_b, mlstm_norm_w, w_br_attn, w_br_mlstm, w_out, norm2_w, router_w, router_b, w_gate, b_gate, w_up, b_up, w_down, b_down, final_norm_w):
    lr = x_prompt.shape[1]
    l = lr + N_META
    assert x_sample.shape[1:] == x_prompt.shape[1:] == (lr, D_MODEL)
    assert lr % (2 * LANES) == 0, "attention pipelines query blocks in pairs; mLSTM chunks are 128 rows"

    w_in0 = w_in[0]
    g0, g1 = OFF_GATES, OFF_GATES + 4 * M_HEADS
    w_main = jnp.concatenate([w_in0[:, :g0], w_in0[:, g1:]], axis=1).astype(BF16)
    w_gates = jnp.pad(w_in0[:, g0:g1], ((0, 0), (0, LANES - 4 * M_HEADS))).astype(BF16)
    lam = (jnp.exp(jnp.sum(lambda_q1[0] * lambda_k1[0])) - jnp.exp(jnp.sum(lambda_q2[0] * lambda_k2[0]))
           + LAM_INIT)
    lam_row = jnp.full((1, LANES), lam, F32)
    sub_w = (attn_subln_w[0] * (1.0 - LAM_INIT)).reshape(1, A_V_DIM)
    neg_row = jnp.where(jnp.arange(LANES) < N_META, 0.0, NEG_BIG).astype(F32).reshape(1, LANES)
    cos_t, sa_t, sb_t = _rope_tables(lr)
    gate_bias = jnp.broadcast_to(
        jnp.pad(mlstm_gate_b[0].T, ((0, 0), (0, 4)))[:, :, None], (M_HEADS, 8, LANES)).astype(F32)
    norm_w_m = mlstm_norm_w[0].reshape(M_HEADS, 1, M_V_DIM)
    rw_t = router_w[0].T.astype(BF16)
    rb_col = jnp.broadcast_to(router_b[0][:, None], (N_EXPERTS, LANES)).astype(F32)
    tri = (jnp.arange(LANES)[:, None] < jnp.arange(LANES)[None, :]).astype(BF16)

    w_a, w_m, w_o = w_br_attn[0].astype(BF16), w_br_mlstm[0].astype(BF16), w_out[0].astype(BF16)
    n1, n2, fw = norm1_w[0].reshape(1, D_MODEL), norm2_w[0].reshape(1, D_MODEL), final_norm_w.reshape(1, D_MODEL)
    bg3, bu3, bd3 = (b_gate[0].reshape(N_EXPERTS, 1, D_FF), b_up[0].reshape(N_EXPERTS, 1, D_FF),
                     b_down[0].reshape(N_EXPERTS, 1, D_MODEL))

    b_all = x_prompt.shape[0] + x_sample.shape[0]
    n_all = b_all * l
    x_all = jnp.concatenate([x_prompt, x_sample], axis=0)
    meta_b = jnp.broadcast_to(meta_tokens[None].astype(x_all.dtype), (b_all, N_META, D_MODEL))
    h0 = jnp.concatenate([x_all, meta_b], axis=1).reshape(n_all, D_MODEL)

    proj, gates = _inproj(h0, n1, w_main, w_gates)
    proj3 = proj.reshape(b_all, l, N_MAIN)
    o_a = _attention(proj3, cos_t, sa_t, sb_t, neg_row, lam_row, sub_w, lr).reshape(n_all, D_MODEL)

    g4 = gates.reshape(b_all, l, 4, M_HEADS).transpose(0, 3, 2, 1)
    off = jnp.array([NEG_BIG, -NEG_BIG, NEG_BIG, -NEG_BIG], F32)[None, None, :, None]
    fill = jnp.broadcast_to(off, (b_all, M_HEADS, 4, CHUNK - N_META))
    g_row = jnp.concatenate([g4[..., :lr], fill, g4[..., lr:]], axis=-1)
    g_row = jnp.pad(g_row, ((0, 0), (0, 0), (0, 4), (0, 0)))
    h_m = _mlstm(proj3, g_row, gate_bias, norm_w_m, lr).reshape(n_all, D_MODEL)

    def channel_mix(b0, b):
        n = b * l
        assert n % SC_WINDOW == 0, "SparseCore dispatch works on whole 128-token windows"
        h1, xn2, top_e, top_w, rank, cnt = _merge(
            o_a, h_m, proj, h0, w_a, w_m, w_o, n2, rw_t, rb_col, tri, b0 * l, n)

        counts = cnt[:, 0].astype(jnp.int32)
        padded = (counts + MOE_ROWS - 1) // MOE_ROWS * MOE_ROWS
        pends = jnp.cumsum(padded)
        pstarts = pends - padded
        e_ids = jnp.arange(N_EXPERTS, dtype=jnp.int32)
        pstart_of = jnp.sum(jnp.where(top_e[..., None] == e_ids, pstarts, 0), axis=-1)
        dest = pstart_of + rank
        n_blocks = (n * TOP_K + N_EXPERTS * (MOE_ROWS - 1) + MOE_ROWS - 1) // MOE_ROWS
        cap = n_blocks * MOE_ROWS
        blk_start = jnp.arange(n_blocks, dtype=jnp.int32) * MOE_ROWS
        blk_e = jnp.minimum(jnp.sum((pends[None, :] <= blk_start[:, None]).astype(jnp.int32), axis=1),
                            N_EXPERTS - 1)
        n_used = (pends[-1:] // MOE_ROWS).astype(jnp.int32)

        dest_win = dest.reshape(TOP_K, n // SC_WINDOW, SC_WINDOW).transpose(1, 0, 2)
        xs = _sc_dispatch(xn2, dest_win, cap)
        y = _experts(blk_e, n_used, xs, w_gate[0], bg3, w_up[0], bu3, w_down[0], bd3)
        yg = _sc_gather(y, dest.reshape(TOP_K * n // SC_WINDOW, SC_WINDOW)).reshape(TOP_K, b, l, ROW_WORDS)
        return _combine(h1.reshape(b, l, D_MODEL), yg, top_w.T.reshape(b, l, TOP_K), fw, 0, b, lr)

    return (channel_mix(0, x_prompt.shape[0]), channel_mix(x_prompt.shape[0], x_sample.shape[0]))
```

```python
import functools
import math

import jax
import jax.numpy as jnp
from jax import lax
from jax.experimental import pallas as pl
from jax.experimental.pallas import tpu as pltpu
from jax.experimental.pallas import tpu_sc as plsc

F32 = jnp.float32
BF16 = jnp.bfloat16

D_MODEL = 1024
N_META = 16
EPS = 1e-6
A_HEADS = 8
A_QK_DIM = 64
A_V_DIM = 128
ROT_DIM = 16
ROPE_THETA = 500000.0
M_HEADS = 4
M_QK_DIM = 128
M_V_DIM = 256
CHUNK = 128
N_EXPERTS = 32
TOP_K = 4
D_FF = 1024
SWIGLU_LIMIT = 7.0
SWIGLU_ALPHA = 1.702
LAM_INIT = 0.8 - 0.6 * math.exp(-0.3 * 0)

LANES = 128
NEG_BIG = -1e30
MOE_ROWS = 512
SC_CORES = 2
SC_SUBCORES = 16
SC_WINDOW = 128
ROW_WORDS = D_MODEL // 2
W_AQK = A_HEADS * 2 * A_QK_DIM
W_AV = A_HEADS * A_V_DIM
W_MQK = M_HEADS * M_QK_DIM
W_MV = M_HEADS * M_V_DIM
OFF_GATES = 2 * W_AQK + W_AV + 2 * W_MQK + 2 * W_MV
N_MAIN = OFF_GATES + 2 * D_MODEL
COL_AQ, COL_AK, COL_AV = 0, W_AQK // LANES, 2 * W_AQK // LANES
COL_MQ = (2 * W_AQK + W_AV) // M_QK_DIM
COL_MK = COL_MQ + M_HEADS
COL_MV = (2 * W_AQK + W_AV + 2 * W_MQK) // M_V_DIM
COL_MO = COL_MV + M_HEADS
COL_GA = OFF_GATES // D_MODEL
COL_GM = COL_GA + 1

VMEM_LIMIT = 56 * 1024 * 1024


def _pick(n, prefs):
    for p in prefs:
        if n % p == 0:
            return p
    raise ValueError(f"no tile for {n}")


def _nt(a, b):
    return lax.dot_general(a, b, (((1,), (1,)), ((), ())), preferred_element_type=F32)


def _dot(a, b):
    return jnp.dot(a, b, preferred_element_type=F32)


def _pack_rows(x):
    w = x.shape[1] // 2
    bits = lambda v: lax.bitcast_convert_type(v.astype(BF16).astype(F32), jnp.uint32)
    lo = lax.shift_right_logical(bits(x[:, :w]), jnp.uint32(16))
    hi = bits(x[:, w:]) & jnp.uint32(0xFFFF0000)
    return lax.bitcast_convert_type(lo | hi, jnp.int32)


def _unpack_rows(wds):
    u = lax.bitcast_convert_type(wds, jnp.uint32)
    lo = lax.bitcast_convert_type(lax.shift_left(u, jnp.uint32(16)), F32)
    hi = lax.bitcast_convert_type(u & jnp.uint32(0xFFFF0000), F32)
    return jnp.concatenate([lo, hi], axis=1)


def _inproj_kernel(x_ref, n1_ref, w_ref, wg_ref, o_ref, g_ref, xn_ref):
    @pl.when(pl.program_id(1) == 0)
    def _():
        x = x_ref[...]
        ms = jnp.mean(x * x, axis=-1, keepdims=True)
        xn = (x * lax.rsqrt(ms + EPS) * n1_ref[...]).astype(BF16)
        xn_ref[...] = xn
        g_ref[...] = _dot(xn, wg_ref[...])[:, :4 * M_HEADS]

    o_ref[...] = _dot(xn_ref[...], w_ref[...]).astype(BF16)


def _inproj(h0, norm1_w, w_main, w_gates):
    n = h0.shape[0]
    tm = _pick(n, (1152, 768, 512, 384, 256, 128))
    tn = 2048
    return pl.pallas_call(
        _inproj_kernel,
        grid=(n // tm, N_MAIN // tn),
        in_specs=[
            pl.BlockSpec((tm, D_MODEL), lambda i, j: (i, 0)),
            pl.BlockSpec((1, D_MODEL), lambda i, j: (0, 0)),
            pl.BlockSpec((D_MODEL, tn), lambda i, j: (0, j)),
            pl.BlockSpec((D_MODEL, LANES), lambda i, j: (0, 0)),
        ],
        out_specs=[
            pl.BlockSpec((tm, tn), lambda i, j: (i, j)),
            pl.BlockSpec((tm, 4 * M_HEADS), lambda i, j: (i, 0)),
        ],
        out_shape=[
            jax.ShapeDtypeStruct((n, N_MAIN), BF16),
            jax.ShapeDtypeStruct((n, 4 * M_HEADS), F32),
        ],
        scratch_shapes=[pltpu.VMEM((tm, D_MODEL), BF16)],
        compiler_params=pltpu.CompilerParams(
            dimension_semantics=("arbitrary", "arbitrary"), vmem_limit_bytes=VMEM_LIMIT),
        name="inproj",
    )(h0, norm1_w, w_main, w_gates)


def _attn_kernel(q_ref, k_ref, v_ref, cos_ref, sa_ref, sb_ref, neg_ref, lam_ref, sw_ref,
                 o_ref, q1_ref, q2_ref, vs_ref, kt_ref, sa1_ref, sa2_ref, sb1_ref, sb2_ref, *, lr):
    l = lr + N_META
    lp = lr + LANES
    nblk = lr // LANES
    scale = A_QK_DIM ** -0.5 * math.log2(math.e)

    def rope(t):
        return (t * cos_ref[...] + pltpu.roll(t, LANES - ROT_DIM // 2, 1) * sa_ref[...]
                + pltpu.roll(t, ROT_DIM // 2, 1) * sb_ref[...])

    lane = lax.broadcasted_iota(jnp.int32, (l, LANES), 1)
    qr = rope(q_ref[0].astype(F32)) * scale
    q1_ref[0:l, :] = jnp.where(lane < A_QK_DIM, qr, 0.0).astype(BF16)
    q2_ref[0:l, :] = jnp.where(lane >= A_QK_DIM, qr, 0.0).astype(BF16)
    vs_ref[0:l, :] = v_ref[0]
    pad = jnp.zeros((lp - l, LANES), BF16)
    q1_ref[l:lp, :] = pad
    q2_ref[l:lp, :] = pad
    vs_ref[l:lp, :] = pad
    kr = rope(k_ref[0].astype(F32))
    for c in range(nblk):
        kt_ref[:, c * LANES:(c + 1) * LANES] = kr[c * LANES:(c + 1) * LANES, :].T.astype(BF16)
    k_tail = jnp.concatenate([kr[lr:l, :], jnp.zeros((lp - l, LANES), F32)], axis=0)
    kt_ref[:, lr:lp] = k_tail.T.astype(BF16)

    lam = lam_ref[:, 0:1]

    def scores(r0, s1_ref, s2_ref):
        rows = pl.ds(r0, LANES)
        k_all = kt_ref[...]
        s1_ref[...] = _dot(q1_ref[rows, :], k_all)
        s2_ref[...] = _dot(q2_ref[rows, :], k_all)

    def softmax_parts(s_ref, nrows):
        sm = s_ref[0:nrows, :lr]
        sl = s_ref[0:nrows, lr:] + neg_ref[...]
        m = jnp.maximum(jnp.max(sm, axis=1, keepdims=True), jnp.max(sl, axis=1, keepdims=True))
        pm = jnp.exp2(sm - m)
        pl_ = jnp.exp2(sl - m)
        tot = jnp.sum(pm, axis=1, keepdims=True) + jnp.sum(pl_, axis=1, keepdims=True)
        return pm, pl_, tot

    def finish(r0, nrows, s1_ref, s2_ref):
        p1m, p1l, t1 = softmax_parts(s1_ref, nrows)
        p2m, p2l, t2 = softmax_parts(s2_ref, nrows)
        c = lam * t1 / t2
        am = (p1m - p2m * c).astype(BF16)
        al = (p1l - p2l * c).astype(BF16)
        o = (_dot(am, vs_ref[0:lr, :]) + _dot(al, vs_ref[lr:lp, :])) * (1.0 / t1)
        o = o * lax.rsqrt(jnp.mean(o * o, axis=-1, keepdims=True) + EPS) * sw_ref[...]
        o_ref[0, pl.ds(r0, nrows), :] = o.astype(BF16)

    scores(0, sa1_ref, sa2_ref)

    pairs = _pick(nblk // 2, (8, 4, 2, 1))

    def body(j, c):
        for u in range(pairs):
            ra = pl.multiple_of((j * pairs + u) * (2 * LANES), LANES)
            rb = ra + LANES
            scores(rb, sb1_ref, sb2_ref)
            finish(ra, LANES, sa1_ref, sa2_ref)
            scores(rb + LANES, sa1_ref, sa2_ref)
            finish(rb, LANES, sb1_ref, sb2_ref)
        return c

    lax.fori_loop(0, nblk // (2 * pairs), body, 0)
    finish(lr, N_META, sa1_ref, sa2_ref)


def _attention(proj3, cos_t, sa_t, sb_t, neg_row, lam_row, sub_w, lr):
    b, l, _ = proj3.shape
    lp = lr + LANES
    const2 = lambda i, h: (0, 0)
    return pl.pallas_call(
        functools.partial(_attn_kernel, lr=lr),
        grid=(b, A_HEADS),
        in_specs=[
            pl.BlockSpec((1, l, LANES), lambda i, h: (i, 0, COL_AQ + h)),
            pl.BlockSpec((1, l, LANES), lambda i, h: (i, 0, COL_AK + h)),
            pl.BlockSpec((1, l, LANES), lambda i, h: (i, 0, COL_AV + h)),
            pl.BlockSpec((l, LANES), const2),
            pl.BlockSpec((l, LANES), const2),
            pl.BlockSpec((l, LANES), const2),
            pl.BlockSpec((1, LANES), const2),
            pl.BlockSpec((1, LANES), const2),
            pl.BlockSpec((1, LANES), const2),
        ],
        out_specs=pl.BlockSpec((1, l, LANES), lambda i, h: (i, 0, h)),
        out_shape=jax.ShapeDtypeStruct((b, l, A_HEADS * A_V_DIM), BF16),
        scratch_shapes=([pltpu.VMEM((lp, LANES), BF16)] * 3 + [pltpu.VMEM((LANES, lp), BF16)]
                        + [pltpu.VMEM((LANES, lp), F32)] * 4),
        compiler_params=pltpu.CompilerParams(
            dimension_semantics=("arbitrary", "arbitrary"), vmem_limit_bytes=VMEM_LIMIT),
        name="diff_attn",
    )(proj3, proj3, proj3, cos_t, sa_t, sb_t, neg_row, lam_row, sub_w)


def _log_sigmoid(x):
    return jnp.minimum(x, 0.0) - jnp.log(1.0 + jnp.exp(-jnp.abs(x)))


def _mlstm_kernel(q_ref, k_ref, v_ref, mo_ref, g_ref, gb_ref, nw_ref, o_ref,
                  c_ref, m_ref, kt_ref, h0_ref, u_ref, bcol_ref, amax_ref, aend_ref, bend_ref,
                  hf_ref, hb_ref, *, lr):
    l = lr + N_META
    t = CHUNK
    nc = lr // t
    dv = M_V_DIM
    scale = M_QK_DIM ** -0.5
    row = lax.broadcasted_iota(jnp.int32, (t, t), 0)
    col = lax.broadcasted_iota(jnp.int32, (t, t), 1)
    eye = row == col
    ones_tile = jnp.where(col == 0, 1.0, 0.0).astype(BF16)
    meta_r0 = l - t

    c_ref[...] = jnp.zeros(c_ref.shape, F32)
    m_ref[...] = jnp.zeros(m_ref.shape, F32)

    for c in range(nc + 1):
        r0 = c * t if c < nc else meta_r0
        kt_ref[:, c * t:(c + 1) * t] = k_ref[0, r0:r0 + t, :].astype(F32).T.astype(BF16)

    def scaled_q(r0):
        return (q_ref[0, pl.ds(r0, t), :].astype(F32) * scale).astype(BF16)

    def intra(ci, r0, g0):
        v_aug = jnp.concatenate([v_ref[0, pl.ds(r0, t), :], ones_tile], axis=1)
        kt = kt_ref[:, pl.ds(g0, t)]
        s_raw = _dot(scaled_q(r0), kt)
        for d, mask in ((0, col <= row), (1, col >= row)):
            gi = 2 * d
            ig = g_ref[0, 0, gi:gi + 1, pl.ds(g0, t)] + gb_ref[0, gi:gi + 1, :]
            lf = _log_sigmoid(g_ref[0, 0, gi + 1:gi + 2, pl.ds(g0, t)] + gb_ref[0, gi + 1:gi + 2, :])
            b_col = jnp.sum(jnp.where(mask, jnp.broadcast_to(lf, (t, t)), 0.0), axis=1, keepdims=True)
            b_row = jnp.sum(jnp.where(eye, jnp.broadcast_to(b_col, (t, t)), 0.0), axis=0, keepdims=True)
            a_row = ig - b_row
            dm = jnp.where(mask, jnp.broadcast_to(a_row, (t, t)), NEG_BIG)
            amax = jnp.max(dm, axis=1, keepdims=True)
            s0 = s_raw * jnp.exp(dm - amax)
            h0_ref[d, pl.ds(g0, t), :] = _dot(s0.astype(BF16), v_aug)
            a_end = jnp.max(a_row, axis=1, keepdims=True)
            kts = (kt.astype(F32) * jnp.exp(a_row - a_end)).astype(BF16)
            u_ref[d, ci] = _dot(kts, v_aug)
            bcol_ref[d, pl.ds(g0, t), :] = b_col
            amax_ref[d, pl.ds(g0, t), :] = amax
            aend_ref[d, ci] = jnp.broadcast_to(a_end, (8, LANES))
            bend_ref[d, ci] = jnp.broadcast_to(jnp.sum(lf, axis=1, keepdims=True), (8, LANES))

    ua = _pick(nc, (16, 8, 4, 2, 1))

    def abody(i, c):
        for u in range(ua):
            ci = ua * i + u
            r0 = pl.multiple_of(ci * t, t)
            intra(ci, r0, r0)
        return c

    lax.fori_loop(0, nc // ua, abody, 0)
    intra(nc, meta_r0, lr)

    def step(d, ci, r0, g0):
        m_prev = m_ref[d, 0:1, 0:1]
        amax = amax_ref[d, pl.ds(g0, t), :]
        m_col = jnp.maximum(m_prev, amax)
        num = (jnp.exp(m_prev - m_col) * _dot(scaled_q(r0), c_ref[d].astype(BF16))
               + jnp.exp(amax - m_col) * h0_ref[d, pl.ds(g0, t), :])
        den = jnp.maximum(jnp.abs(num[:, dv:dv + 1]), jnp.exp(-(bcol_ref[d, pl.ds(g0, t), :] + m_col)))
        a_end = aend_ref[d, ci, 0:1, 0:1]
        m_end = jnp.maximum(m_prev, a_end)
        c_ref[d] = jnp.exp(m_prev - m_end) * c_ref[d] + jnp.exp(a_end - m_end) * u_ref[d, ci]
        m_ref[d] = jnp.broadcast_to(bend_ref[d, ci, 0:1, 0:1] + m_end, (8, LANES))
        return num[:, :dv] * (1.0 / den)

    hf_ref[lr:l, :] = step(0, nc, meta_r0, lr)[t - N_META:, :]

    ub = _pick(nc, (2, 1))

    def bbody(i, c):
        for u in range(ub):
            jf = ub * i + u
            rf = pl.multiple_of(jf * t, t)
            hf_ref[pl.ds(rf, t), :] = step(0, jf, rf, rf)
            jb = nc - 1 - jf
            rb = pl.multiple_of(jb * t, t)
            hb_ref[pl.ds(rb, t), :] = step(1, jb, rb, rb)
        return c

    lax.fori_loop(0, nc // ub, bbody, 0)
    hb_ref[lr:l, :] = step(1, nc, meta_r0, lr)[t - N_META:, :]

    def finish(r0, rows):
        hs = hf_ref[pl.ds(r0, rows), :] + hb_ref[pl.ds(r0, rows), :]
        y = hs * lax.rsqrt(jnp.mean(hs * hs, axis=-1, keepdims=True) + EPS) * nw_ref[0]
        gate = jax.nn.sigmoid(mo_ref[0, pl.ds(r0, rows), :].astype(F32))
        o_ref[0, pl.ds(r0, rows), :] = (y * gate).astype(BF16)

    def fbody(i, c):
        for u in range(ua):
            finish(pl.multiple_of((ua * i + u) * t, t), t)
        return c

    lax.fori_loop(0, nc // ua, fbody, 0)
    finish(lr, N_META)


def _mlstm(proj3, gates_row, gate_bias, norm_w, lr):
    b, l, _ = proj3.shape
    gl = gates_row.shape[-1]
    nw = lr // CHUNK + 1
    aug = M_V_DIM + LANES
    return pl.pallas_call(
        functools.partial(_mlstm_kernel, lr=lr),
        grid=(b, M_HEADS),
        in_specs=[
            pl.BlockSpec((1, l, M_QK_DIM), lambda i, h: (i, 0, COL_MQ + h)),
            pl.BlockSpec((1, l, M_QK_DIM), lambda i, h: (i, 0, COL_MK + h)),
            pl.BlockSpec((1, l, M_V_DIM), lambda i, h: (i, 0, COL_MV + h)),
            pl.BlockSpec((1, l, M_V_DIM), lambda i, h: (i, 0, COL_MO + h)),
            pl.BlockSpec((1, 1, 8, gl), lambda i, h: (i, h, 0, 0)),
            pl.BlockSpec((1, 8, LANES), lambda i, h: (h, 0, 0)),
            pl.BlockSpec((1, 1, M_V_DIM), lambda i, h: (h, 0, 0)),
        ],
        out_specs=pl.BlockSpec((1, l, M_V_DIM), lambda i, h: (i, 0, h)),
        out_shape=jax.ShapeDtypeStruct((b, l, M_HEADS * M_V_DIM), BF16),
        scratch_shapes=[
            pltpu.VMEM((2, M_QK_DIM, aug), F32),
            pltpu.VMEM((2, 8, LANES), F32),
            pltpu.VMEM((M_QK_DIM, gl), BF16),
            pltpu.VMEM((2, gl, aug), F32),
            pltpu.VMEM((2, nw, M_QK_DIM, aug), F32),
            pltpu.VMEM((2, gl, 1), F32),
            pltpu.VMEM((2, gl, 1), F32),
            pltpu.VMEM((2, nw, 8, LANES), F32),
            pltpu.VMEM((2, nw, 8, LANES), F32),
            pltpu.VMEM((l, M_V_DIM), F32), pltpu.VMEM((l, M_V_DIM), F32),
        ],
        compiler_params=pltpu.CompilerParams(
            dimension_semantics=("arbitrary", "arbitrary"), vmem_limit_bytes=VMEM_LIMIT),
        name="mlstm",
    )(proj3, proj3, proj3, proj3, gates_row, gate_bias, norm_w)


def _merge_kernel(oa_ref, hm_ref, ga_ref, gm_ref, h_ref, wa_ref, wm_ref, wo_ref, n2_ref,
                  rw_ref, rb_ref, tri_ref,
                  h1_ref, xn_ref, te_ref, tw_ref, rk_ref, cnt_ref, carry_ref, *, tm):
    @pl.when(pl.program_id(0) == 0)
    def _():
        carry_ref[...] = jnp.zeros(carry_ref.shape, F32)

    pa = _dot(oa_ref[...], wa_ref[...])
    pm = _dot(hm_ref[...], wm_ref[...])
    merged = (jax.nn.sigmoid(ga_ref[...].astype(F32)) * pa
              + jax.nn.sigmoid(gm_ref[...].astype(F32)) * pm)
    h1 = h_ref[...] + _dot(merged.astype(BF16), wo_ref[...])
    h1_ref[...] = h1
    xn32 = h1 * lax.rsqrt(jnp.mean(h1 * h1, axis=-1, keepdims=True) + EPS) * n2_ref[...]
    xn = xn32.astype(BF16)
    xn_ref[...] = _pack_rows(xn32)

    logits = _nt(rw_ref[...], xn) + rb_ref[:, 0:1]
    eidx = lax.broadcasted_iota(jnp.int32, (N_EXPERTS, tm), 0)
    work = logits
    vals, hots = [], []
    for _ in range(TOP_K):
        mx = jnp.max(work, axis=0, keepdims=True)
        sel = jnp.min(jnp.where(work == mx, eidx, N_EXPERTS), axis=0, keepdims=True)
        hot = eidx == sel
        vals.append(mx)
        hots.append(hot)
        work = jnp.where(hot, -jnp.inf, work)
    ex = [jnp.exp(v - vals[0]) for v in vals]
    inv = 1.0 / (ex[0] + ex[1] + ex[2] + ex[3])
    chosen = jnp.where(hots[0] | hots[1] | hots[2] | hots[3], 1.0, 0.0)

    carry = carry_ref[:, 0:1]
    ranks = []
    for s in range(tm // LANES):
        sub = chosen[:, s * LANES:(s + 1) * LANES]
        ranks.append(_dot(sub.astype(BF16), tri_ref[...]) + carry)
        carry = carry + jnp.sum(sub, axis=1, keepdims=True)
    before = jnp.concatenate(ranks, axis=1) if len(ranks) > 1 else ranks[0]
    carry_ref[...] = jnp.broadcast_to(carry, carry_ref.shape)
    cnt_ref[...] = jnp.broadcast_to(carry, cnt_ref.shape)

    for kk in range(TOP_K):
        te_ref[kk:kk + 1, :] = jnp.sum(jnp.where(hots[kk], eidx, 0), axis=0, keepdims=True)
        tw_ref[kk:kk + 1, :] = ex[kk] * inv
        rk_ref[kk:kk + 1, :] = jnp.sum(jnp.where(hots[kk], before, 0.0), axis=0,
                                       keepdims=True).astype(jnp.int32)


def _merge(o_a, h_m, proj, h0, w_a, w_m, w_o, norm2_w, rw_t, rb_col, tri, row0, n):
    tm = _pick(math.gcd(n, row0) if row0 else n, (384, 256, 128))
    t0 = row0 // tm
    row = lambda i: (i, 0)
    src = lambda i: (t0 + i, 0)
    const = lambda i: (0, 0)
    tok = lambda i: (0, i)
    wspec = pl.BlockSpec((D_MODEL, D_MODEL), const)
    return pl.pallas_call(
        functools.partial(_merge_kernel, tm=tm),
        grid=(n // tm,),
        in_specs=[
            pl.BlockSpec((tm, D_MODEL), src),
            pl.BlockSpec((tm, D_MODEL), src),
            pl.BlockSpec((tm, D_MODEL), lambda i: (t0 + i, COL_GA)),
            pl.BlockSpec((tm, D_MODEL), lambda i: (t0 + i, COL_GM)),
            pl.BlockSpec((tm, D_MODEL), src),
            wspec, wspec, wspec,
            pl.BlockSpec((1, D_MODEL), const),
            pl.BlockSpec((N_EXPERTS, D_MODEL), const),
            pl.BlockSpec((N_EXPERTS, LANES), const),
            pl.BlockSpec((LANES, LANES), const),
        ],
        out_specs=[
            pl.BlockSpec((tm, D_MODEL), row),
            pl.BlockSpec((tm, ROW_WORDS), row),
            pl.BlockSpec((TOP_K, tm), tok),
            pl.BlockSpec((TOP_K, tm), tok),
            pl.BlockSpec((TOP_K, tm), tok),
            pl.BlockSpec((N_EXPERTS, LANES), const),
        ],
        out_shape=[
            jax.ShapeDtypeStruct((n, D_MODEL), F32),
            jax.ShapeDtypeStruct((n, ROW_WORDS), jnp.int32),
            jax.ShapeDtypeStruct((TOP_K, n), jnp.int32),
            jax.ShapeDtypeStruct((TOP_K, n), F32),
            jax.ShapeDtypeStruct((TOP_K, n), jnp.int32),
            jax.ShapeDtypeStruct((N_EXPERTS, LANES), F32),
        ],
        scratch_shapes=[pltpu.VMEM((N_EXPERTS, LANES), F32)],
        compiler_params=pltpu.CompilerParams(
            dimension_semantics=("arbitrary",), vmem_limit_bytes=VMEM_LIMIT),
        name="merge_router",
    )(o_a, h_m, proj, proj, h0, w_a, w_m, w_o, norm2_w, rw_t, rb_col, tri)


def _expert_kernel(be_ref, nu_ref, x_ref, wg_ref, bg_ref, wu_ref, bu_ref, wd_ref, bd_ref, y_ref,
                   wg_s, wu_s, wd_s):
    i = pl.program_id(0)
    used = i < nu_ref[0]

    @pl.when(jnp.logical_or(i == 0, be_ref[i] != be_ref[jnp.maximum(i - 1, 0)]))
    def _():
        wg_s[...] = wg_ref[0].astype(BF16)
        wu_s[...] = wu_ref[0].astype(BF16)
        wd_s[...] = wd_ref[0].astype(BF16)

    @pl.when(used)
    def _():
        x = _unpack_rows(x_ref[...]).astype(BF16)
        half = D_FF // 2
        acc = bd_ref[0]
        for c0 in (0, half):
            gt = jnp.minimum(_dot(x, wg_s[:, c0:c0 + half]) + bg_ref[0, :, c0:c0 + half], SWIGLU_LIMIT)
            up = jnp.clip(_dot(x, wu_s[:, c0:c0 + half]) + bu_ref[0, :, c0:c0 + half],
                          -SWIGLU_LIMIT, SWIGLU_LIMIT)
            act = ((up + 1.0) * (gt * jax.nn.sigmoid(SWIGLU_ALPHA * gt))).astype(BF16)
            acc = acc + _dot(act, wd_s[c0:c0 + half, :])
        y_ref[...] = _pack_rows(acc)

    @pl.when(jnp.logical_not(used))
    def _():
        y_ref[...] = jnp.zeros(y_ref.shape, jnp.int32)


def _experts(blk_e, n_used, xs, w_g, b_g, w_u, b_u, w_d, b_d):
    cap = xs.shape[0]
    n_blocks = cap // MOE_ROWS
    wspec = lambda d0, d1: pl.BlockSpec((1, d0, d1), lambda i, be, nu: (be[i], 0, 0))
    return pl.pallas_call(
        _expert_kernel,
        grid_spec=pltpu.PrefetchScalarGridSpec(
            num_scalar_prefetch=2,
            grid=(n_blocks,),
            in_specs=[
                pl.BlockSpec((MOE_ROWS, ROW_WORDS), lambda i, be, nu: (i, 0)),
                wspec(D_MODEL, D_FF), wspec(1, D_FF),
                wspec(D_MODEL, D_FF), wspec(1, D_FF),
                wspec(D_FF, D_MODEL), wspec(1, D_MODEL),
            ],
            out_specs=pl.BlockSpec((MOE_ROWS, ROW_WORDS), lambda i, be, nu: (i, 0)),
            scratch_shapes=[pltpu.VMEM((D_MODEL, D_FF), BF16), pltpu.VMEM((D_MODEL, D_FF), BF16),
                            pltpu.VMEM((D_FF, D_MODEL), BF16)],
        ),
        out_shape=jax.ShapeDtypeStruct((cap, ROW_WORDS), jnp.int32),
        compiler_params=pltpu.CompilerParams(
            dimension_semantics=("arbitrary",), vmem_limit_bytes=VMEM_LIMIT),
        name="experts",
    )(blk_e, n_used, xs, w_g, b_g, w_u, b_u, w_d, b_d)


def _sc_worker_windows(n_win):
    return -(-n_win // (SC_CORES * SC_SUBCORES))


def _sc_dispatch(x_words, dest3, cap):
    n = x_words.shape[0]
    n_win = n // SC_WINDOW
    per = _sc_worker_windows(n_win)

    def body(x_hbm, d_hbm, o_hbm, idx_v, rows_v):
        wid = lax.axis_index("s") * SC_CORES + lax.axis_index("c")

        @pl.loop(0, per)
        def _(i):
            win = jnp.minimum(wid * per + i, n_win - 1)
            pltpu.sync_copy(d_hbm.at[win], idx_v)
            pltpu.sync_copy(x_hbm.at[pl.ds(win * SC_WINDOW, SC_WINDOW)], rows_v)
            for kk in range(TOP_K):
                pltpu.sync_copy(rows_v, o_hbm.at[idx_v.at[kk]])

    return pl.kernel(
        body,
        out_type=jax.ShapeDtypeStruct((cap, ROW_WORDS), jnp.int32),
        mesh=plsc.VectorSubcoreMesh(core_axis_name="c", subcore_axis_name="s"),
        scratch_types=[pltpu.VMEM((TOP_K, SC_WINDOW), jnp.int32),
                       pltpu.VMEM((SC_WINDOW, ROW_WORDS), jnp.int32)],
        name="moe_dispatch",
    )(x_words, dest3)


def _sc_gather(y_words, idx2):
    n_win = idx2.shape[0]
    per = _sc_worker_windows(n_win)

    def body(y_hbm, i_hbm, o_hbm, idx_v, rows_v, sem):
        wid = lax.axis_index("s") * SC_CORES + lax.axis_index("c")

        @pl.loop(0, per)
        def _(i):
            win = jnp.minimum(wid * per + i, n_win - 1)
            pltpu.sync_copy(i_hbm.at[win], idx_v)
            pltpu.async_copy(y_hbm.at[idx_v], rows_v, sem).wait()
            pltpu.sync_copy(rows_v, o_hbm.at[pl.ds(win * SC_WINDOW, SC_WINDOW)])

    return pl.kernel(
        body,
        out_type=jax.ShapeDtypeStruct((n_win * SC_WINDOW, ROW_WORDS), jnp.int32),
        mesh=plsc.VectorSubcoreMesh(core_axis_name="c", subcore_axis_name="s"),
        scratch_types=[pltpu.VMEM((SC_WINDOW,), jnp.int32),
                       pltpu.VMEM((SC_WINDOW, ROW_WORDS), jnp.int32),
                       pltpu.SemaphoreType.DMA],
        name="moe_gather",
    )(y_words, idx2)


def _combine_kernel(h_ref, y_ref, w_ref, fw_ref, o_ref):
    acc = h_ref[0]
    for kk in range(TOP_K):
        acc = acc + w_ref[0, :, kk:kk + 1] * _unpack_rows(y_ref[kk, 0])
    o_ref[0] = acc * lax.rsqrt(jnp.mean(acc * acc, axis=-1, keepdims=True) + EPS) * fw_ref[...]


def _combine(h1, yg, wts, final_w, b0, nb, lr):
    tr = _pick(lr, (1024, 512, 256, 128))
    return pl.pallas_call(
        _combine_kernel,
        grid=(nb, lr // tr),
        in_specs=[
            pl.BlockSpec((1, tr, D_MODEL), lambda i, j: (b0 + i, j, 0)),
            pl.BlockSpec((TOP_K, 1, tr, ROW_WORDS), lambda i, j: (0, b0 + i, j, 0)),
            pl.BlockSpec((1, tr, TOP_K), lambda i, j: (b0 + i, j, 0)),
            pl.BlockSpec((1, D_MODEL), lambda i, j: (0, 0)),
        ],
        out_specs=pl.BlockSpec((1, tr, D_MODEL), lambda i, j: (i, j, 0)),
        out_shape=jax.ShapeDtypeStruct((nb, lr, D_MODEL), F32),
        compiler_params=pltpu.CompilerParams(
            dimension_semantics=("arbitrary", "arbitrary"), vmem_limit_bytes=VMEM_LIMIT),
        name="combine",
    )(h1, yg, wts, final_w)


def _rope_tables(lr):
    l = lr + N_META
    half = ROT_DIM // 2
    pos = jnp.concatenate([jnp.arange(N_META, l, dtype=F32), jnp.arange(N_META, dtype=F32)])
    inv_freq = ROPE_THETA ** (-jnp.arange(0, ROT_DIM, 2, dtype=F32) / ROT_DIM)
    ang = pos[:, None] * inv_freq[None, :]
    cos, sin = jnp.cos(ang), jnp.sin(ang)
    z = jnp.zeros((l, A_QK_DIM - ROT_DIM), F32)
    zh = jnp.zeros((l, half), F32)
    cos64 = jnp.concatenate([cos, cos, z + 1.0], axis=1)
    sa64 = jnp.concatenate([-sin, zh, z], axis=1)
    sb64 = jnp.concatenate([zh, sin, z], axis=1)
    dup = lambda a: jnp.concatenate([a, a], axis=1)
    return dup(cos64), dup(sa64), dup(sb64)


def kernel(x_prompt, x_sample, meta_tokens, norm1_w, w_in, lambda_q1, lambda_k1, lambda_q2, lambda_k2, attn_subln_w, mlstm_gate_b, mlstm_norm_w, w_br_attn, w_br_mlstm, w_out, norm2_w, router_w, router_b, w_gate, b_gate, w_up, b_up, w_down, b_down, final_norm_w):
    lr = x_prompt.shape[1]
    l = lr + N_META
    assert x_sample.shape[1:] == x_prompt.shape[1:] == (lr, D_MODEL)
    assert lr % (2 * LANES) == 0, "attention pipelines query blocks in pairs; mLSTM chunks are 128 rows"

    w_in0 = w_in[0]
    g0, g1 = OFF_GATES, OFF_GATES + 4 * M_HEADS
    w_main = jnp.concatenate([w_in0[:, :g0], w_in0[:, g1:]], axis=1).astype(BF16)
    w_gates = jnp.pad(w_in0[:, g0:g1], ((0, 0), (0, LANES - 4 * M_HEADS))).astype(BF16)
    lam = (jnp.exp(jnp.sum(lambda_q1[0] * lambda_k1[0])) - jnp.exp(jnp.sum(lambda_q2[0] * lambda_k2[0]))
           + LAM_INIT)
    lam_row = jnp.full((1, LANES), lam, F32)
    sub_w = (attn_subln_w[0] * (1.0 - LAM_INIT)).reshape(1, A_V_DIM)
    neg_row = jnp.where(jnp.arange(LANES) < N_META, 0.0, NEG_BIG).astype(F32).reshape(1, LANES)
    cos_t, sa_t, sb_t = _rope_tables(lr)
    gate_bias = jnp.broadcast_to(
        jnp.pad(mlstm_gate_b[0].T, ((0, 0), (0, 4)))[:, :, None], (M_HEADS, 8, LANES)).astype(F32)
    norm_w_m = mlstm_norm_w[0].reshape(M_HEADS, 1, M_V_DIM)
    rw_t = router_w[0].T.astype(BF16)
    rb_col = jnp.broadcast_to(router_b[0][:, None], (N_EXPERTS, LANES)).astype(F32)
    tri = (jnp.arange(LANES)[:, None] < jnp.arange(LANES)[None, :]).astype(BF16)

    w_a, w_m, w_o = w_br_attn[0].astype(BF16), w_br_mlstm[0].astype(BF16), w_out[0].astype(BF16)
    n1, n2, fw = norm1_w[0].reshape(1, D_MODEL), norm2_w[0].reshape(1, D_MODEL), final_norm_w.reshape(1, D_MODEL)
    bg3, bu3, bd3 = (b_gate[0].reshape(N_EXPERTS, 1, D_FF), b_up[0].reshape(N_EXPERTS, 1, D_FF),
                     b_down[0].reshape(N_EXPERTS, 1, D_MODEL))

    b_all = x_prompt.shape[0] + x_sample.shape[0]
    n_all = b_all * l
    x_all = jnp.concatenate([x_prompt, x_sample], axis=0)
    meta_b = jnp.broadcast_to(meta_tokens[None].astype(x_all.dtype), (b_all, N_META, D_MODEL))
    h0 = jnp.concatenate([x_all, meta_b], axis=1).reshape(n_all, D_MODEL)

    proj, gates = _inproj(h0, n1, w_main, w_gates)
    proj3 = proj.reshape(b_all, l, N_MAIN)
    o_a = _attention(proj3, cos_t, sa_t, sb_t, neg_row, lam_row, sub_w, lr).reshape(n_all, D_MODEL)

    g4 = gates.reshape(b_all, l, 4, M_HEADS).transpose(0, 3, 2, 1)
    off = jnp.array([NEG_BIG, -NEG_BIG, NEG_BIG, -NEG_BIG], F32)[None, None, :, None]
    fill = jnp.broadcast_to(off, (b_all, M_HEADS, 4, CHUNK - N_META))
    g_row = jnp.concatenate([g4[..., :lr], fill, g4[..., lr:]], axis=-1)
    g_row = jnp.pad(g_row, ((0, 0), (0, 0), (0, 4), (0, 0)))
    h_m = _mlstm(proj3, g_row, gate_bias, norm_w_m, lr).reshape(n_all, D_MODEL)

    def channel_mix(b0, b):
        n = b * l
        assert n % SC_WINDOW == 0, "SparseCore dispatch works on whole 128-token windows"
        h1, xn2, top_e, top_w, rank, cnt = _merge(
            o_a, h_m, proj, h0, w_a, w_m, w_o, n2, rw_t, rb_col, tri, b0 * l, n)

        counts = cnt[:, 0].astype(jnp.int32)
        padded = (counts + MOE_ROWS - 1) // MOE_ROWS * MOE_ROWS
        pends = jnp.cumsum(padded)
        pstarts = pends - padded
        e_ids = jnp.arange(N_EXPERTS, dtype=jnp.int32)
        pstart_of = jnp.sum(jnp.where(top_e[..., None] == e_ids, pstarts, 0), axis=-1)
        dest = pstart_of + rank
        n_blocks = (n * TOP_K + N_EXPERTS * (MOE_ROWS - 1) + MOE_ROWS - 1) // MOE_ROWS
        cap = n_blocks * MOE_ROWS
        blk_start = jnp.arange(n_blocks, dtype=jnp.int32) * MOE_ROWS
        blk_e = jnp.minimum(jnp.sum((pends[None, :] <= blk_start[:, None]).astype(jnp.int32), axis=1),
                            N_EXPERTS - 1)
        n_used = (pends[-1:] // MOE_ROWS).astype(jnp.int32)

        dest_win = dest.reshape(TOP_K, n // SC_WINDOW, SC_WINDOW).transpose(1, 0, 2)
        xs = _sc_dispatch(xn2, dest_win, cap)
        y = _experts(blk_e, n_used, xs, w_gate[0], bg3, w_up[0], bu3, w_down[0], bd3)
        yg = _sc_gather(y, dest.reshape(TOP_K * n // SC_WINDOW, SC_WINDOW)).reshape(TOP_K, b, l, ROW_WORDS)
        return _combine(h1.reshape(b, l, D_MODEL), yg, top_w.T.reshape(b, l, TOP_K), fw, 0, b, lr)

    return (channel_mix(0, x_prompt.shape[0]), channel_mix(x_prompt.shape[0], x_sample.shape[0]))
```
